```python
import math
import jax, jax.numpy as jnp
from jax import lax
import numpy as np

D_MODEL = 2048
BATCH = 16
SEQ = 256
DEPTH = 4
DEC_BATCH = 2
DEC_SEQ = 2048
PAST_LEN = 256

GRID_W = 64
N_HEADS = 4
HEAD_DIM = 128
MIX_W = N_HEADS * HEAD_DIM
N_KV = 2
KV_W = N_KV * HEAD_DIM
GQA_GROUP = N_HEADS // N_KV
GLA_RANK = 16
GLA_TAU = 16.0
CHUNK = 64
Q_BLOCK = 128
CONV_K = 5
D_FF = 5632
ROPE_THETA = 10000.0
N_BRANCH = 4
N_DIR = 2
N_MOD = 9
EPS = 1e-6

_COLS = (
    ('gla_q', MIX_W), ('gla_k', MIX_W), ('gla_v', MIX_W), ('gla_g', MIX_W),
    ('gla_lr', N_DIR * GLA_RANK),
    ('ml_q', MIX_W), ('ml_k', MIX_W), ('ml_v', MIX_W), ('ml_o', MIX_W),
    ('ml_if', N_DIR * 2 * N_HEADS),
    ('gd_qkv', 3 * MIX_W), ('gd_g', MIX_W), ('gd_ab', N_DIR * 2 * N_HEADS),
    ('at_q', MIX_W), ('at_k', KV_W), ('at_v', KV_W),
    ('merge', N_BRANCH * D_MODEL),
)
IN_W = (4 * MIX_W + N_DIR * GLA_RANK + 4 * MIX_W + N_DIR * 2 * N_HEADS
        + 4 * MIX_W + N_DIR * 2 * N_HEADS + MIX_W + 2 * KV_W + N_BRANCH * D_MODEL)

kernel_name = 'hybrid_diffusion_prefix_trunk_step'


def _rms(x, w=None):
    xf = x.astype(jnp.float32)
    y = xf * lax.rsqrt(jnp.mean(xf * xf, axis=-1, keepdims=True) + EPS)
    if w is not None:
        y = y * w.astype(jnp.float32)
    return y.astype(x.dtype)


def _l2n(x):
    xf = x.astype(jnp.float32)
    return xf * lax.rsqrt(jnp.sum(xf * xf, axis=-1, keepdims=True) + EPS)


def _split_cols(z):
    out, off = {}, 0
    for name, width in _COLS:
        out[name] = z[..., off:off + width]
        off += width
    return out


def _heads(x, n=N_HEADS):
    return x.reshape(x.shape[:-1] + (n, HEAD_DIM))


def _to_chunks(x):
    B, T = x.shape[:2]
    x = x.reshape((B, T // CHUNK, CHUNK) + x.shape[2:])
    return jnp.moveaxis(jnp.moveaxis(x, 1, 0), 3, 2)


def _from_chunks(y):
    N, B, H, C = y.shape[:4]
    y = jnp.swapaxes(jnp.moveaxis(y, 0, 1), 2, 3)
    return y.reshape((B, N * C, H) + y.shape[4:])


def _masks():
    i = jnp.arange(CHUNK)
    return i[:, None] >= i[None, :], i[:, None] > i[None, :]


def _gla_step(S, xs):
    q, k, v, lg = xs
    causal, _ = _masks()
    b = jnp.cumsum(lg, axis=2)
    diff = b[:, :, :, None, :] - b[:, :, None, :, :]
    w = jnp.exp(jnp.where(causal[:, :, None], diff, -jnp.inf))
    att = jnp.einsum('bhtd,bhsd,bhtsd->bhts', q, k, w)
    o = jnp.einsum('bhtd,bhde->bhte', q * jnp.exp(b), S) + jnp.einsum('bhts,bhse->bhte', att, v)
    b_end = b[:, :, -1:, :]
    S_new = (jnp.exp(b_end)[:, :, 0, :, None] * S
             + jnp.einsum('bhsd,bhse->bhde', k * jnp.exp(b_end - b), v))
    return S_new, o


def _mlstm_step(carry, xs):
    Cs, ns, m = carry
    q, k, v, ig, lf = xs
    causal, _ = _masks()
    F = jnp.cumsum(lf, axis=-1)
    logD = jnp.where(causal, F[..., :, None] - F[..., None, :] + ig[..., None, :], -jnp.inf)
    inter = F + m[..., None]
    m_t = jnp.maximum(inter, jnp.max(logD, axis=-1))
    D = jnp.exp(logD - m_t[..., None])
    a_in = jnp.exp(inter - m_t)
    s = jnp.einsum('bhtd,bhsd->bhts', q, k) * D
    num = a_in[..., None] * jnp.einsum('bhtd,bhde->bhte', q, Cs) + jnp.einsum('bhts,bhse->bhte', s, v)
    den = a_in * jnp.einsum('bhtd,bhd->bht', q, ns) + jnp.sum(s, axis=-1)
    h = num / jnp.maximum(jnp.abs(den), jnp.exp(-m_t))[..., None]
    m_new = m_t[..., -1]
    w_end = jnp.exp(F[..., -1:] - F + ig - m_new[..., None])
    a_end = jnp.exp(F[..., -1] + m - m_new)
    kw = k * w_end[..., None]
    C_new = a_end[..., None, None] * Cs + jnp.einsum('bhsd,bhse->bhde', kw, v)
    n_new = a_end[..., None] * ns + jnp.sum(kw, axis=2)
    return (C_new, n_new, m_new), h


def _gdn_step(S, xs):
    q, k, v, g, beta = xs
    causal, strict = _masks()
    gam = jnp.cumsum(g, axis=-1)
    decay = jnp.exp(jnp.where(causal, gam[..., :, None] - gam[..., None, :], -jnp.inf))
    kk = jnp.einsum('bhtd,bhsd->bhts', k, k)
    tri = jnp.eye(CHUNK, dtype=kk.dtype) + jnp.where(strict, beta[..., None] * kk * decay, 0.0)
    rhs = jnp.concatenate([v * beta[..., None], k * (beta * jnp.exp(gam))[..., None]], axis=-1)
    sol = lax.linalg.triangular_solve(tri, rhs, left_side=True, lower=True, unit_diagonal=True)
    dv = v.shape[-1]
    w_new = sol[..., :dv] - jnp.einsum('bhtd,bhde->bhte', sol[..., dv:], S)
    qk = jnp.einsum('bhtd,bhsd->bhts', q, k) * decay
    o = (jnp.einsum('bhtd,bhde->bhte', q * jnp.exp(gam)[..., None], S)
         + jnp.einsum('bhts,bhse->bhte', qk, w_new))
    g_end = gam[..., -1:]
    S_new = (jnp.exp(g_end)[..., None] * S
             + jnp.einsum('bhsd,bhse->bhde', k * jnp.exp(g_end - gam)[..., None], w_new))
    return S_new, o


def _run_dir(step, xs_tok, s0, reverse):
    if reverse:
        xs_tok = tuple(jnp.flip(a, axis=1) for a in xs_tok)
    s_fin, ys = lax.scan(step, s0, tuple(_to_chunks(a) for a in xs_tok))
    y = _from_chunks(ys)
    if reverse:
        y = jnp.flip(y, axis=1)
    return y, s_fin


def _short_conv(x, w):
    pad = CONV_K // 2
    return lax.conv_general_dilated(x, w[:, None, :].astype(x.dtype), window_strides=(1,),
                                    padding=[(pad, pad)], dimension_numbers=('NWC', 'WIO', 'NWC'),
                                    feature_group_count=x.shape[-1])


def _rope_2d(x):
    T = x.shape[1]
    rows = T // GRID_W
    row = jnp.repeat(jnp.arange(rows), GRID_W).astype(jnp.float32)
    col = (jnp.arange(rows * GRID_W) % GRID_W).astype(jnp.float32)
    n_pairs = HEAD_DIM // 4
    inv = ROPE_THETA ** (-jnp.arange(n_pairs, dtype=jnp.float32) / n_pairs)
    ang = jnp.concatenate([row[:, None] * inv, col[:, None] * inv], axis=-1)
    cos = jnp.cos(ang)[None, :, None, :]
    sin = jnp.sin(ang)[None, :, None, :]
    xf = x.astype(jnp.float32).reshape(x.shape[:-1] + (HEAD_DIM // 2, 2))
    x1, x2 = xf[..., 0], xf[..., 1]
    out = jnp.stack([x1 * cos - x2 * sin, x1 * sin + x2 * cos], axis=-1).reshape(x.shape)
    return out.astype(x.dtype)


def _attend(q, k, v):
    B, T = q.shape[:2]
    nb = T // Q_BLOCK
    qb = jnp.moveaxis(q.reshape((B, nb, Q_BLOCK) + q.shape[2:]), 1, 0)
    scale = HEAD_DIM ** -0.5

    def one(qblk):
        s = jnp.einsum('bqkgd,bskd->bkgqs', qblk, k).astype(jnp.float32) * scale
        p = jax.nn.softmax(s, axis=-1).astype(v.dtype)
        return jnp.einsum('bkgqs,bskd->bqkgd', p, v)

    o = lax.map(one, qb)
    return jnp.moveaxis(o, 0, 1).reshape(q.shape)


def _swiglu(h, w_in, w_out):
    g, u = jnp.split(h @ w_in, 2, axis=-1)
    return (jax.nn.silu(g) * u) @ w_out


def _mixer(h, lp, st):
    f32 = jnp.float32
    B, T, _ = h.shape
    z = _split_cols(h @ lp['w_in'])
    if st is None:
        s_gla = jnp.zeros((B, N_DIR, N_HEADS, HEAD_DIM, HEAD_DIM), f32)
        s_mc = jnp.zeros((B, N_DIR, N_HEADS, HEAD_DIM, HEAD_DIM), f32)
        s_mn = jnp.zeros((B, N_DIR, N_HEADS, HEAD_DIM), f32)
        s_mm = jnp.zeros((B, N_DIR, N_HEADS), f32)
        s_gd = jnp.zeros((B, N_DIR, N_HEADS, HEAD_DIM, HEAD_DIM), f32)
    else:
        s_gla, s_mc, s_mn, s_mm, s_gd = (st[n].astype(f32) for n in ('gla', 'mc', 'mn', 'mm', 'gd'))

    gq = _heads(z['gla_q']).astype(f32) * HEAD_DIM ** -0.5
    gk = _heads(z['gla_k']).astype(f32)
    gv = _heads(z['gla_v']).astype(f32)
    lr = z['gla_lr'].reshape(B, T, N_DIR, GLA_RANK)
    glog = jax.nn.log_sigmoid((jnp.einsum('btir,irk->btik', lr, lp['gla_w2'])
                               + lp['gla_b2']).astype(f32)) / GLA_TAU
    glog = glog.reshape(B, T, N_DIR, N_HEADS, HEAD_DIM)
    gla_out, gla_fin = [], []
    for d in range(N_DIR):
        y, s = _run_dir(_gla_step, (gq, gk, gv, glog[:, :, d]), s_gla[:, d], d == 1)
        gla_out.append(y)
        gla_fin.append(s)
    y_gla = _rms(gla_out[0] + gla_out[1], lp['gla_norm_w']) * jax.nn.silu(_heads(z['gla_g']).astype(f32))

    mq = _heads(z['ml_q']).astype(f32)
    mk = _heads(z['ml_k']).astype(f32) * HEAD_DIM ** -0.5
    mv = _heads(z['ml_v']).astype(f32)
    gates = z['ml_if'].reshape(B, T, N_DIR, 2, N_HEADS).astype(f32) + lp['ml_gate_b'].astype(f32)
    ml_out, ml_fin = [], []
    for d in range(N_DIR):
        ig = gates[:, :, d, 0]
        lf = jax.nn.log_sigmoid(gates[:, :, d, 1])
        y, s = _run_dir(_mlstm_step, (mq, mk, mv, ig, lf), (s_mc[:, d], s_mn[:, d], s_mm[:, d]), d == 1)
        ml_out.append(y)
        ml_fin.append(s)
    y_ml = _rms(ml_out[0] + ml_out[1], lp['ml_norm_w']) * jax.nn.sigmoid(_heads(z['ml_o']).astype(f32))

    qkv = jax.nn.silu(_short_conv(z['gd_qkv'], lp['gd_conv_w']))
    dq, dk, dv = jnp.split(qkv, 3, axis=-1)
    dq = _l2n(_heads(dq)) * HEAD_DIM ** -0.5
    dk = _l2n(_heads(dk))
    dv = _heads(dv).astype(f32)
    ab = z['gd_ab'].reshape(B, T, N_DIR, 2, N_HEADS).astype(f32)
    gd_out, gd_fin = [], []
    for d in range(N_DIR):
        g = -jnp.exp(lp['gd_a_log'][d].astype(f32)) * jax.nn.softplus(ab[:, :, d, 0] + lp['gd_dt_bias'][d].astype(f32))
        beta = jax.nn.sigmoid(ab[:, :, d, 1])
        y, s = _run_dir(_gdn_step, (dq, dk, dv, g, beta), s_gd[:, d], d == 1)
        gd_out.append(y)
        gd_fin.append(s)
    y_gd = _rms(gd_out[0] + gd_out[1], lp['gd_norm_w']) * jax.nn.silu(_heads(z['gd_g']).astype(f32))

    aq = _rms(_heads(z['at_q']), lp['q_norm_w'])
    ak = _rms(_heads(z['at_k'], N_KV), lp['k_norm_w'])
    av = _heads(z['at_v'], N_KV)
    if st is None:
        k_all, v_all = ak, av
    else:
        aq = _rope_2d(aq)
        k_all = jnp.concatenate([st['k'].astype(ak.dtype), _rope_2d(ak)], axis=1)
        v_all = jnp.concatenate([st['v'].astype(av.dtype), av], axis=1)
    y_at = _attend(aq.reshape(B, T, N_KV, GQA_GROUP, HEAD_DIM), k_all, v_all)

    ybr = jnp.stack([y_gla.reshape(B, T, MIX_W).astype(h.dtype), y_ml.reshape(B, T, MIX_W).astype(h.dtype),
                     y_gd.reshape(B, T, MIX_W).astype(h.dtype), y_at.reshape(B, T, MIX_W).astype(h.dtype)], axis=2)
    mg = jax.nn.sigmoid(z['merge'].reshape(B, T, N_BRANCH, D_MODEL).astype(f32)).astype(h.dtype)
    merged = jnp.sum(mg * jnp.einsum('btnc,ncd->btnd', ybr, lp['w_branch']), axis=2)
    out = merged @ lp['w_out']
    if st is None:
        ctx = dict(k=ak, v=av, gla=jnp.stack(gla_fin, axis=1),
                   mc=jnp.stack([s[0] for s in ml_fin], axis=1),
                   mn=jnp.stack([s[1] for s in ml_fin], axis=1),
                   mm=jnp.stack([s[2] for s in ml_fin], axis=1),
                   gd=jnp.stack(gd_fin, axis=1))
    else:
        ctx = None
    return out, ctx


def _layer(x, cvec, lp, st):
    mod = jax.nn.silu(cvec) @ lp['w_ada'] + lp['b_ada']
    sh1, sc1, g1, sh2, sc2, g2, sh3, sc3, g3 = jnp.split(mod[:, None, :], N_MOD, axis=-1)
    x = x + 0.5 * g1 * _swiglu(_rms(x) * (1 + sc1) + sh1, lp['w_ffn_in'][0], lp['w_ffn_out'][0])
    mix, ctx = _mixer(_rms(x) * (1 + sc2) + sh2, lp, st)
    x = x + g2 * mix
    x = x + 0.5 * g3 * _swiglu(_rms(x) * (1 + sc3) + sh3, lp['w_ffn_in'][1], lp['w_ffn_out'][1])
    return x, ctx


def setup_inputs(seed: int = 0) -> dict:
    key = jax.random.key(seed)
    ks = jax.random.split(key, 32)
    f32 = jnp.float32

    def nrm(k, shape, s):
        return jax.random.normal(k, shape, f32) * s

    st6 = (DEC_BATCH, DEPTH, N_DIR, N_HEADS, HEAD_DIM, HEAD_DIM)
    dt = jnp.exp(jax.random.uniform(ks[23], (DEPTH, N_DIR, N_HEADS), f32, math.log(1e-3), math.log(1e-1)))
    return {
        'x_prompt': nrm(ks[0], (BATCH, SEQ, D_MODEL), 1.0),
        'x_sample': nrm(ks[1], (DEC_BATCH, DEC_SEQ, D_MODEL), 1.0),
        'cache_k': nrm(ks[2], (DEC_BATCH, DEPTH, PAST_LEN, N_KV, HEAD_DIM), 1.0),
        'cache_v': nrm(ks[3], (DEC_BATCH, DEPTH, PAST_LEN, N_KV, HEAD_DIM), 1.0),
        'state_gla': nrm(ks[4], st6, 0.5),
        'state_mlstm_c': nrm(ks[5], st6, 0.5),
        'state_mlstm_n': nrm(ks[6], (DEC_BATCH, DEPTH, N_DIR, N_HEADS, HEAD_DIM), 0.5),
        'state_mlstm_m': nrm(ks[7], (DEC_BATCH, DEPTH, N_DIR, N_HEADS), 1.0),
        'state_gdn': nrm(ks[8], st6, 0.1),
        'c': nrm(ks[9], (DEC_BATCH, D_MODEL), 1.0),
        'c_ctx': nrm(ks[10], (D_MODEL,), 1.0),
        'w_ada': nrm(ks[11], (DEPTH, D_MODEL, N_MOD * D_MODEL), 0.5 * D_MODEL ** -0.5),
        'b_ada': nrm(ks[12], (DEPTH, N_MOD * D_MODEL), 0.02),
        'w_ffn_in': nrm(ks[13], (DEPTH, 2, D_MODEL, 2 * D_FF), D_MODEL ** -0.5),
        'w_ffn_out': nrm(ks[14], (DEPTH, 2, D_FF, D_MODEL), D_FF ** -0.5),
        'w_in': nrm(ks[15], (DEPTH, D_MODEL, IN_W), D_MODEL ** -0.5),
        'gla_w2': nrm(ks[16], (DEPTH, N_DIR, GLA_RANK, MIX_W), GLA_RANK ** -0.5),
        'gla_b2': nrm(ks[17], (DEPTH, N_DIR, MIX_W), 0.1),
        'gla_norm_w': 1.0 + nrm(ks[18], (DEPTH, HEAD_DIM), 0.1),
        'ml_gate_b': jnp.concatenate([nrm(ks[19], (DEPTH, N_DIR, 1, N_HEADS), 0.1),
                                      3.0 + nrm(ks[20], (DEPTH, N_DIR, 1, N_HEADS), 0.5)], axis=2),
        'ml_norm_w': 1.0 + nrm(ks[21], (DEPTH, HEAD_DIM), 0.1),
        'gd_conv_w': nrm(ks[22], (DEPTH, CONV_K, 3 * MIX_W), CONV_K ** -0.5),
        'gd_a_log': jnp.log(jax.random.uniform(ks[24], (DEPTH, N_DIR, N_HEADS), f32, 1.0, 16.0)),
        'gd_dt_bias': dt + jnp.log(-jnp.expm1(-dt)),
        'gd_norm_w': 1.0 + nrm(ks[25], (DEPTH, HEAD_DIM), 0.1),
        'q_norm_w': 1.0 + nrm(ks[26], (DEPTH, HEAD_DIM), 0.1),
        'k_norm_w': 1.0 + nrm(ks[27], (DEPTH, HEAD_DIM), 0.1),
        'w_branch': nrm(ks[28], (DEPTH, N_BRANCH, MIX_W, D_MODEL), MIX_W ** -0.5),
        'w_out': nrm(ks[29], (DEPTH, D_MODEL, D_MODEL), D_MODEL ** -0.5),
        'final_norm_w': 1.0 + nrm(ks[30], (D_MODEL,), 0.1),
    }


def reference(x_prompt, x_sample, cache_k, cache_v, state_gla, state_mlstm_c, state_mlstm_n,
              state_mlstm_m, state_gdn, c, c_ctx, w_ada, b_ada, w_ffn_in, w_ffn_out, w_in,
              gla_w2, gla_b2, gla_norm_w, ml_gate_b, ml_norm_w, gd_conv_w, gd_a_log, gd_dt_bias,
              gd_norm_w, q_norm_w, k_norm_w, w_branch, w_out, final_norm_w):
    layers = [dict(w_ada=w_ada[l], b_ada=b_ada[l], w_ffn_in=w_ffn_in[l], w_ffn_out=w_ffn_out[l],
                   w_in=w_in[l], gla_w2=gla_w2[l], gla_b2=gla_b2[l], gla_norm_w=gla_norm_w[l],
                   ml_gate_b=ml_gate_b[l], ml_norm_w=ml_norm_w[l], gd_conv_w=gd_conv_w[l],
                   gd_a_log=gd_a_log[l], gd_dt_bias=gd_dt_bias[l], gd_norm_w=gd_norm_w[l],
                   q_norm_w=q_norm_w[l], k_norm_w=k_norm_w[l], w_branch=w_branch[l], w_out=w_out[l])
              for l in range(DEPTH)]

    xp = x_prompt
    ctxs = []
    for l in range(DEPTH):
        xp, ctx = _layer(xp, c_ctx[None, :], layers[l], None)
        ctxs.append(ctx)
    y_prompt = _rms(xp, final_norm_w)

    xs = x_sample
    for l in range(DEPTH):
        st = dict(k=cache_k[:, l], v=cache_v[:, l], gla=state_gla[:, l], mc=state_mlstm_c[:, l],
                  mn=state_mlstm_n[:, l], mm=state_mlstm_m[:, l], gd=state_gdn[:, l])
        xs, _ = _layer(xs, c, layers[l], st)
    y_sample = _rms(xs, final_norm_w)

    new_cache_k = jnp.stack([cx['k'] for cx in ctxs], axis=1)
    new_cache_v = jnp.stack([cx['v'] for cx in ctxs], axis=1)
    new_state_gla = jnp.stack([cx['gla'] for cx in ctxs], axis=1)
    new_state_mlstm_c = jnp.stack([cx['mc'] for cx in ctxs], axis=1)
    new_state_mlstm_n = jnp.stack([cx['mn'] for cx in ctxs], axis=1)
    new_state_mlstm_m = jnp.stack([cx['mm'] for cx in ctxs], axis=1)
    new_state_gdn = jnp.stack([cx['gd'] for cx in ctxs], axis=1)
    return (y_prompt, y_sample, new_cache_k, new_cache_v, new_state_gla, new_state_mlstm_c,
            new_state_mlstm_n, new_state_mlstm_m, new_state_gdn)
```

```python
import functools

import numpy as np
import jax
import jax.numpy as jnp
from jax import lax
from jax.experimental import pallas as pl
from jax.experimental.pallas import tpu as pltpu

F32 = jnp.float32
BF16 = jnp.bfloat16
HI = lax.Precision.HIGHEST

D_MODEL = 2048
DEPTH = 4
GRID_W = 64
N_HEADS = 4
HEAD_DIM = 128
MIX_W = N_HEADS * HEAD_DIM
N_KV = 2
KV_W = N_KV * HEAD_DIM
GLA_RANK = 16
GLA_TAU = 16.0
CHUNK = 64
SUB = 16
Q_BLOCK = 128
CONV_K = 5
D_FF = 5632
ROPE_THETA = 10000.0
N_BRANCH = 4
N_DIR = 2
N_MOD = 9
EPS = 1e-6
SCALE = HEAD_DIM ** -0.5

Z_MERGE = 0
Z_GLA = N_BRANCH * D_MODEL
Z_ML = Z_GLA + 4 * MIX_W
Z_GDQKV = Z_ML + 4 * MIX_W
Z_GDG = Z_GDQKV + 3 * MIX_W
Z_ATQ = Z_GDG + MIX_W
Z_ATK = Z_ATQ + MIX_W
Z_ATV = Z_ATK + KV_W
Z_SMALL = Z_ATV + KV_W
NZ = 15872
LANE = 128
L_LR = 0
L_IF = N_DIR * GLA_RANK
L_AB = L_IF + N_DIR * 2 * N_HEADS

VMEM_LIMIT = 56 * 1024 * 1024


def _cparams(n_axes):
    return pltpu.CompilerParams(dimension_semantics=("arbitrary",) * n_axes,
                                vmem_limit_bytes=VMEM_LIMIT)


def _dot(a, b):
    return jnp.dot(a.astype(BF16), b.astype(BF16), preferred_element_type=F32)


def _dot_nt(a, b):
    return lax.dot_general(a.astype(BF16), b.astype(BF16), (((1,), (1,)), ((), ())),
                           preferred_element_type=F32)


def _dot_tn(a, b):
    return lax.dot_general(a.astype(BF16), b.astype(BF16), (((0,), (0,)), ((), ())),
                           preferred_element_type=F32)


def _dot_hi(a, b):
    return jnp.dot(a, b, precision=HI, preferred_element_type=F32)


def _sigmoid(x):
    return 1.0 / (1.0 + jnp.exp(-x))


def _silu(x):
    return x * _sigmoid(x)


def _softplus(x):
    return jnp.maximum(x, 0.0) + jnp.log1p(jnp.exp(-jnp.abs(x)))


def _log_sigmoid(x):
    return -_softplus(-x)


def _rmsmod(x, sc, sh):
    ms = jnp.mean(x * x, axis=-1, keepdims=True)
    return x * lax.rsqrt(ms + EPS) * (1.0 + sc) + sh


def _head_rms(x, w):
    outs = []
    for h in range(x.shape[1] // HEAD_DIM):
        seg = x[:, h * HEAD_DIM:(h + 1) * HEAD_DIM]
        ms = jnp.mean(seg * seg, axis=-1, keepdims=True)
        outs.append(seg * lax.rsqrt(ms + EPS))
    return jnp.concatenate(outs, axis=1) * w


class _Seqs:
    def __init__(self, nc, tc, nl, tl):
        self.nc, self.tc, self.nl, self.tl = nc, tc, nl, tl
        self.ctx_rows = nc * tc
        self.rows = nc * tc + nl * tl

    def mod_row(self, start):
        return jnp.where(start < self.ctx_rows, 0, 1 + (start - self.ctx_rows) // self.tl)

    def scan_tables(self):
        fwd, bwd, flg, sq = [], [], [], []
        base = 0
        for s in range(self.nc + self.nl):
            lat = s >= self.nc
            n = (self.tl if lat else self.tc) // CHUNK
            for j in range(n):
                fwd.append(base + j)
                bwd.append(base + n - 1 - j)
                flg.append((1 if j == 0 else 0) | (2 if j == n - 1 else 0) | (4 if lat else 0))
                sq.append(s)
            base += n
        return tuple(jnp.asarray(np.array(a, np.int32)) for a in (fwd, bwd, flg, sq))


def _ada_kernel(c_ref, w_ref, b_ref, o_ref):
    o_ref[...] = _dot(_silu(c_ref[...]), w_ref[...]) + b_ref[...]


def _ada(cvec, w_ada, b_ada):
    tn = 1024
    nmod = N_MOD * D_MODEL
    depth = w_ada.shape[0]
    return pl.pallas_call(
        _ada_kernel,
        grid=(depth, nmod // tn),
        in_specs=[pl.BlockSpec((8, D_MODEL), lambda l, j: (0, 0)),
                  pl.BlockSpec((None, D_MODEL, tn), lambda l, j: (l, 0, j)),
                  pl.BlockSpec((None, 1, tn), lambda l, j: (l, 0, j))],
        out_specs=pl.BlockSpec((None, 8, tn), lambda l, j: (l, 0, j)),
        out_shape=jax.ShapeDtypeStruct((depth, 8, nmod), F32),
        compiler_params=_cparams(2),
    )(cvec, w_ada, b_ada.reshape(depth, 1, nmod))


def _ffn_kernel(x_ref, sh_ref, sc_ref, g_ref, wg_ref, wu_ref, wo_ref, fw_ref, o_ref, h_ref,
                *, seqs, tm, nf, final):
    f = pl.program_id(1)
    r = seqs.mod_row(pl.program_id(0) * tm)

    @pl.when(f == 0)
    def _():
        h_ref[...] = _rmsmod(x_ref[...], sc_ref[pl.ds(r, 1), :], sh_ref[pl.ds(r, 1), :]).astype(BF16)
        o_ref[...] = jnp.zeros_like(o_ref)

    h = h_ref[...]
    g = jnp.dot(h, wg_ref[...].astype(BF16), preferred_element_type=F32)
    u = jnp.dot(h, wu_ref[...].astype(BF16), preferred_element_type=F32)
    a = (_silu(g) * u).astype(BF16)
    o_ref[...] += jnp.dot(a, wo_ref[...].astype(BF16), preferred_element_type=F32)

    @pl.when(f == nf - 1)
    def _():
        y = x_ref[...] + 0.5 * g_ref[pl.ds(r, 1), :] * o_ref[...]
        if final:
            ms = jnp.mean(y * y, axis=-1, keepdims=True)
            y = y * lax.rsqrt(ms + EPS) * fw_ref[...]
        o_ref[...] = y


def _ffn(x, mod, w_in, w_out, fw, seqs, l, i, final, tm, tf):
    rows = x.shape[0]
    nf = D_FF // tf
    j0 = 0 if i == 0 else 6
    kern = functools.partial(_ffn_kernel, seqs=seqs, tm=tm, nf=nf, final=final)
    modspec = lambda j: pl.BlockSpec((None, 8, D_MODEL), lambda m, f: (l, 0, j))
    return pl.pallas_call(
        kern,
        grid=(rows // tm, nf),
        in_specs=[pl.BlockSpec((tm, D_MODEL), lambda m, f: (m, 0), pipeline_mode=pl.Buffered(1)),
                  modspec(j0), modspec(j0 + 1), modspec(j0 + 2),
                  pl.BlockSpec((None, None, D_MODEL, tf), lambda m, f: (l, i, 0, f)),
                  pl.BlockSpec((None, None, D_MODEL, tf), lambda m, f: (l, i, 0, nf + f)),
                  pl.BlockSpec((None, None, tf, D_MODEL), lambda m, f: (l, i, f, 0)),
                  pl.BlockSpec((1, D_MODEL), lambda m, f: (0, 0))],
        out_specs=pl.BlockSpec((tm, D_MODEL), lambda m, f: (m, 0)),
        out_shape=jax.ShapeDtypeStruct((rows, D_MODEL), F32),
        scratch_shapes=[pltpu.VMEM((tm, D_MODEL), BF16)],
        compiler_params=_cparams(2),
    )(x, mod, mod, mod, w_in, w_in, w_out, fw)


def _inproj_kernel(x_ref, sh_ref, sc_ref, w_ref, o_ref, h_ref, *, seqs, tm):
    r = seqs.mod_row(pl.program_id(0) * tm)

    @pl.when(pl.program_id(1) == 0)
    def _():
        h_ref[...] = _rmsmod(x_ref[...], sc_ref[pl.ds(r, 1), :], sh_ref[pl.ds(r, 1), :]).astype(BF16)

    o_ref[...] = jnp.dot(h_ref[...], w_ref[...], preferred_element_type=F32)


def _inproj(x, mod, w_perm, seqs, l, tm, tn):
    rows = x.shape[0]
    kern = functools.partial(_inproj_kernel, seqs=seqs, tm=tm)
    modspec = lambda j: pl.BlockSpec((None, 8, D_MODEL), lambda m, n: (l, 0, j))
    return pl.pallas_call(
        kern,
        grid=(rows // tm, NZ // tn),
        in_specs=[pl.BlockSpec((tm, D_MODEL), lambda m, n: (m, 0), pipeline_mode=pl.Buffered(1)),
                  modspec(3), modspec(4),
                  pl.BlockSpec((None, D_MODEL, tn), lambda m, n: (l, 0, n))],
        out_specs=pl.BlockSpec((tm, tn), lambda m, n: (m, n)),
        out_shape=jax.ShapeDtypeStruct((rows, NZ), F32),
        scratch_shapes=[pltpu.VMEM((tm, D_MODEL), BF16)],
        compiler_params=_cparams(2),
    )(x, mod, mod, w_perm)


def _tri_masks(rev):
    ri = lax.broadcasted_iota(jnp.int32, (CHUNK, CHUNK), 0)
    ci = lax.broadcasted_iota(jnp.int32, (CHUNK, CHUNK), 1)
    if rev:
        return ri <= ci, ri < ci
    return ri >= ci, ri > ci


def _flags(flg):
    first = (flg & 1) != 0
    last = (flg & 2) != 0
    lat = (flg & 4) != 0
    return first, last, lat


def _hs(h):
    return slice(h * HEAD_DIM, (h + 1) * HEAD_DIM)


def _gla_kernel(fwd_ref, bwd_ref, flg_ref, sq_ref,
                qf, kf, vf, sf, qb, kb, vb, sb, w2f, w2b, b2f, b2b, init_ref,
                of_ref, ob_ref, st_out, st_scr, b_scr):
    i = pl.program_id(0)
    first, last, lat = _flags(flg_ref[i])

    @pl.when(jnp.logical_and(first, jnp.logical_not(lat)))
    def _():
        st_scr[...] = jnp.zeros_like(st_scr)

    @pl.when(jnp.logical_and(first, lat))
    def _():
        for d in range(N_DIR):
            for h in range(N_HEADS):
                st_scr[d, h] = init_ref[d, h].T

    dirs = ((qf, kf, vf, sf, w2f, b2f, of_ref), (qb, kb, vb, sb, w2b, b2b, ob_ref))
    for d, (q_ref, k_ref, v_ref, s_ref, w2, b2, o_ref) in enumerate(dirs):
        rev = d == 1
        incl, _ = _tri_masks(rev)
        pre = jnp.dot(s_ref[...].astype(BF16), w2[...], preferred_element_type=F32) + b2[...]
        lg = _log_sigmoid(pre) / GLA_TAU
        b = _dot_hi(incl.astype(F32), lg)
        b_scr[d] = b
        bend = jnp.sum(lg, axis=0, keepdims=True)
        q = q_ref[...] * SCALE
        k = k_ref[...]
        v = v_ref[...]
        qd = q * jnp.exp(b)
        kd = k * jnp.exp(bend - b)
        eb = jnp.exp(bend)

        ri = lax.broadcasted_iota(jnp.int32, (CHUNK, CHUNK), 0)
        ci = lax.broadcasted_iota(jnp.int32, (CHUNK, CHUNK), 1)
        if rev:
            off_mask = ci >= (ri // SUB + 1) * SUB
        else:
            off_mask = ci < (ri // SUB) * SUB
        nsub = CHUNK // SUB
        att_rows = [[] for _ in range(N_HEADS)]
        for blk in range(nsub):
            r0 = blk * SUB
            has_src = (blk < nsub - 1) if rev else (blk > 0)
            if not has_src:
                for h in range(N_HEADS):
                    att_rows[h].append(jnp.zeros((SUB, CHUNK), F32))
                continue
            edge = r0 + SUB if rev else r0 - 1
            bref = b[edge:edge + 1, :]
            qe = q[r0:r0 + SUB, :] * jnp.exp(b[r0:r0 + SUB, :] - bref)
            ke = k * jnp.exp(jnp.minimum(bref - b, 0.0))
            for h in range(N_HEADS):
                att_rows[h].append(_dot_nt(qe[:, _hs(h)], ke[:, _hs(h)]))
        o_main = []
        for h in range(N_HEADS):
            att = jnp.where(off_mask, jnp.concatenate(att_rows[h], axis=0), 0.0)
            o_main.append(_dot_nt(qd[:, _hs(h)], st_scr[d, h]) + _dot(att, v[:, _hs(h)]))
        o_main = jnp.concatenate(o_main, axis=1)

        rowi = lax.broadcasted_iota(jnp.int32, (SUB, 1), 0)
        for blk in range(nsub):
            r0 = blk * SUB
            bb = b[r0:r0 + SUB, :]
            qq = q[r0:r0 + SUB, :]

            def body(j, acc, r0=r0, bb=bb, qq=qq, d=d, k_ref=k_ref, v_ref=v_ref, rev=rev):
                s = r0 + j
                bs = b_scr[d, pl.ds(s, 1), :]
                ks = k_ref[pl.ds(s, 1), :]
                vs = v_ref[pl.ds(s, 1), :]
                p = qq * ks * jnp.exp(jnp.minimum(bb - bs, 0.0))
                valid = (rowi <= j) if rev else (rowi >= j)
                parts = []
                for h in range(N_HEADS):
                    c = jnp.sum(p[:, _hs(h)], axis=-1, keepdims=True)
                    parts.append(jnp.where(valid, c, 0.0) * vs[:, _hs(h)])
                return acc + jnp.concatenate(parts, axis=1)

            acc = lax.fori_loop(0, SUB, body, jnp.zeros((SUB, MIX_W), F32))
            o_ref[r0:r0 + SUB, :] = o_main[r0:r0 + SUB, :] + acc

        for h in range(N_HEADS):
            st_scr[d, h] = st_scr[d, h] * eb[:, _hs(h)] + _dot_tn(v[:, _hs(h)], kd[:, _hs(h)])

    @pl.when(jnp.logical_and(last, jnp.logical_not(lat)))
    def _():
        for d in range(N_DIR):
            for h in range(N_HEADS):
                st_out[d, h] = st_scr[d, h].T


def _scan_call(kern, z, seqs, col0, extra_in, extra_specs, state_ins, state_in_specs,
               out_shapes, out_specs, scratch):
    tabs = seqs.scan_tables()
    nsteps = int(tabs[0].shape[0])
    c = col0 // MIX_W

    def zspec(which, blk, width):
        if which == 0:
            return pl.BlockSpec((CHUNK, width), lambda i, fw, bw, fl, sq: (fw[i], blk))
        return pl.BlockSpec((CHUNK, width), lambda i, fw, bw, fl, sq: (bw[i], blk))

    in_specs = []
    for which in (0, 1):
        in_specs += [zspec(which, c, MIX_W), zspec(which, c + 1, MIX_W), zspec(which, c + 2, MIX_W),
                     zspec(which, Z_SMALL // LANE, LANE)]
    in_specs += list(extra_specs) + list(state_in_specs)
    gs = pltpu.PrefetchScalarGridSpec(
        num_scalar_prefetch=4, grid=(nsteps,), in_specs=in_specs, out_specs=out_specs,
        scratch_shapes=scratch)
    return pl.pallas_call(kern, grid_spec=gs, out_shape=out_shapes, compiler_params=_cparams(1))(
        *tabs, *([z] * 8), *extra_in, *state_ins)


def _const_spec(shape):
    nd = len(shape)
    return pl.BlockSpec(shape, lambda i, fw, bw, fl, sq: (0,) * nd)


def _scan_out_specs(seqs):
    o_f = pl.BlockSpec((CHUNK, MIX_W), lambda i, fw, bw, fl, sq: (fw[i], 0))
    o_b = pl.BlockSpec((CHUNK, MIX_W), lambda i, fw, bw, fl, sq: (bw[i], 0))
    return o_f, o_b


def _lat_idx(seqs, sq, i):
    return jnp.maximum(sq[i] - seqs.nc, 0)


def _ctx_idx(seqs, sq, i):
    return jnp.minimum(sq[i], seqs.nc - 1)


def _gla(z, seqs, l, w2f, w2b, b2f, b2b, state_gla):
    rows = z.shape[0]
    o_f, o_b = _scan_out_specs(seqs)
    st_shape = (N_DIR, N_HEADS, HEAD_DIM, HEAD_DIM)
    init_spec = pl.BlockSpec((None, None) + st_shape,
                             lambda i, fw, bw, fl, sq: (_lat_idx(seqs, sq, i), l, 0, 0, 0, 0))
    st_spec = pl.BlockSpec((None,) + st_shape,
                           lambda i, fw, bw, fl, sq: (_ctx_idx(seqs, sq, i), 0, 0, 0, 0))
    return _scan_call(
        _gla_kernel, z, seqs, Z_GLA,
        (w2f, w2b, b2f, b2b),
        (_const_spec((LANE, MIX_W)), _const_spec((LANE, MIX_W)), _const_spec((1, MIX_W)), _const_spec((1, MIX_W))),
        (state_gla,), (init_spec,),
        (jax.ShapeDtypeStruct((rows, MIX_W), F32), jax.ShapeDtypeStruct((rows, MIX_W), F32),
         jax.ShapeDtypeStruct((seqs.nc,) + st_shape, F32)),
        (o_f, o_b, st_spec),
        [pltpu.VMEM(st_shape, F32), pltpu.VMEM((N_DIR, CHUNK, MIX_W), F32)])


def _mlstm_kernel(fwd_ref, bwd_ref, flg_ref, sq_ref,
                  qf, kf, vf, sf, qb, kb, vb, sb, bias_ref, c0_ref, n0_ref, m0_ref,
                  of_ref, ob_ref, c_out, n_out, m_out, c_scr, n_scr, m_scr):
    i = pl.program_id(0)
    first, last, lat = _flags(flg_ref[i])

    @pl.when(jnp.logical_and(first, jnp.logical_not(lat)))
    def _():
        c_scr[...] = jnp.zeros_like(c_scr)
        n_scr[...] = jnp.zeros_like(n_scr)
        m_scr[...] = jnp.zeros_like(m_scr)

    @pl.when(jnp.logical_and(first, lat))
    def _():
        c_scr[...] = c0_ref[...]
        n_scr[...] = n0_ref[...]
        m_scr[...] = m0_ref[...]

    dirs = ((qf, kf, vf, sf, of_ref), (qb, kb, vb, sb, ob_ref))
    for d, (q_ref, k_ref, v_ref, s_ref, o_ref) in enumerate(dirs):
        rev = d == 1
        incl, _ = _tri_masks(rev)
        last_row = 0 if rev else CHUNK - 1
        gates = s_ref[...] + bias_ref[...]
        lf_all = _log_sigmoid(gates)
        f_all = _dot_hi(incl.astype(F32), lf_all)
        f_all_t = f_all.T
        gates_t = gates.T
        q = q_ref[...]
        k = k_ref[...] * SCALE
        v = v_ref[...]
        for h in range(N_HEADS):
            ji = L_IF + d * 2 * N_HEADS + h
            jf = ji + N_HEADS
            r = d * N_HEADS + h
            fc = f_all[:, jf:jf + 1]
            frow = f_all_t[jf:jf + 1, :]
            igc = gates[:, ji:ji + 1]
            igrow = gates_t[ji:ji + 1, :]
            m_prev = m_scr[r:r + 1, 0:1]
            n_prev = n_scr[r:r + 1, :]
            c_prev = c_scr[d, h]
            qh, kh, vh = q[:, _hs(h)], k[:, _hs(h)], v[:, _hs(h)]
            log_d = jnp.where(incl, fc - frow + igrow, -jnp.inf)
            inter = fc + m_prev
            m_t = jnp.maximum(inter, jnp.max(log_d, axis=-1, keepdims=True))
            dmat = jnp.exp(log_d - m_t)
            a_in = jnp.exp(inter - m_t)
            s = _dot_nt(qh, kh) * dmat
            num = a_in * _dot(qh, c_prev) + _dot(s, vh)
            den = a_in * jnp.sum(qh * n_prev, axis=-1, keepdims=True) + jnp.sum(s, axis=-1, keepdims=True)
            o_ref[:, _hs(h)] = num / jnp.maximum(jnp.abs(den), jnp.exp(-m_t))
            m_new = m_t[last_row:last_row + 1, :]
            f_end = fc[last_row:last_row + 1, :]
            w_end = jnp.exp(f_end - fc + igc - m_new)
            a_end = jnp.exp(f_end + m_prev - m_new)
            kw = kh * w_end
            c_scr[d, h] = a_end * c_prev + _dot_tn(kw, vh)
            n_scr[r:r + 1, :] = a_end * n_prev + jnp.sum(kw, axis=0, keepdims=True)
            m_scr[r:r + 1, :] = jnp.broadcast_to(m_new, (1, LANE))

    @pl.when(jnp.logical_and(last, jnp.logical_not(lat)))
    def _():
        c_out[...] = c_scr[...]
        n_out[...] = n_scr[...]
        m_out[...] = m_scr[...]


def _mlstm(z, seqs, l, bias_row, state_c, state_n, state_m):
    rows = z.shape[0]
    o_f, o_b = _scan_out_specs(seqs)
    st_shape = (N_DIR, N_HEADS, HEAD_DIM, HEAD_DIM)
    nu = N_DIR * N_HEADS
    c0_spec = pl.BlockSpec((None, None) + st_shape,
                           lambda i, fw, bw, fl, sq: (_lat_idx(seqs, sq, i), l, 0, 0, 0, 0))
    v0_spec = pl.BlockSpec((None, None, nu, LANE),
                           lambda i, fw, bw, fl, sq: (_lat_idx(seqs, sq, i), l, 0, 0))
    c_spec = pl.BlockSpec((None,) + st_shape, lambda i, fw, bw, fl, sq: (_ctx_idx(seqs, sq, i), 0, 0, 0, 0))
    v_spec = pl.BlockSpec((None, nu, LANE), lambda i, fw, bw, fl, sq: (_ctx_idx(seqs, sq, i), 0, 0))
    depth = state_c.shape[1]
    n0 = state_n.reshape(seqs.nl, depth, nu, HEAD_DIM)
    m0 = jnp.broadcast_to(state_m.reshape(seqs.nl, depth, nu, 1), (seqs.nl, depth, nu, LANE))
    return _scan_call(
        _mlstm_kernel, z, seqs, Z_ML,
        (bias_row,), (_const_spec((1, LANE)),),
        (state_c, n0, m0), (c0_spec, v0_spec, v0_spec),
        (jax.ShapeDtypeStruct((rows, MIX_W), F32), jax.ShapeDtypeStruct((rows, MIX_W), F32),
         jax.ShapeDtypeStruct((seqs.nc,) + st_shape, F32),
         jax.ShapeDtypeStruct((seqs.nc, nu, LANE), F32), jax.ShapeDtypeStruct((seqs.nc, nu, LANE), F32)),
        (o_f, o_b, c_spec, v_spec, v_spec),
        [pltpu.VMEM(st_shape, F32), pltpu.VMEM((nu, LANE), F32), pltpu.VMEM((nu, LANE), F32)])


def _unit_tri_inverse(n):
    ri = lax.broadcasted_iota(jnp.int32, (CHUNK, CHUNK), 0)
    ci = lax.broadcasted_iota(jnp.int32, (CHUNK, CHUNK), 1)
    same16 = (ri // SUB) == (ci // SUB)
    same32 = (ri // (2 * SUB)) == (ci // (2 * SUB))
    eye = (ri == ci).astype(F32)
    nd = jnp.where(same16, n, 0.0)
    n1 = jnp.where(jnp.logical_and(same32, jnp.logical_not(same16)), n, 0.0)
    n2 = jnp.where(same32, 0.0, n)
    t = eye - nd
    p = _dot_hi(nd, nd)
    t = t + _dot_hi(t, p)
    p = _dot_hi(p, p)
    t = t + _dot_hi(t, p)
    p = _dot_hi(p, p)
    t = t + _dot_hi(t, p)
    t = t - _dot_hi(t, _dot_hi(n1, t))
    t = t - _dot_hi(t, _dot_hi(n2, t))
    return t


def _gdn_kernel(fwd_ref, bwd_ref, flg_ref, sq_ref,
                qf, kf, vf, sf, qb, kb, vb, sb, alog_ref, dtb_ref, s0_ref,
                of_ref, ob_ref, s_out, s_scr):
    i = pl.program_id(0)
    first, last, lat = _flags(flg_ref[i])

    @pl.when(jnp.logical_and(first, jnp.logical_not(lat)))
    def _():
        s_scr[...] = jnp.zeros_like(s_scr)

    @pl.when(jnp.logical_and(first, lat))
    def _():
        s_scr[...] = s0_ref[...]

    dirs = ((qf, kf, vf, sf, of_ref), (qb, kb, vb, sb, ob_ref))
    for d, (q_ref, k_ref, v_ref, s_ref, o_ref) in enumerate(dirs):
        rev = d == 1
        incl, strict = _tri_masks(rev)
        last_row = 0 if rev else CHUNK - 1
        small = s_ref[...]
        g_all = -jnp.exp(alog_ref[...]) * _softplus(small + dtb_ref[...])
        beta_all = _sigmoid(small)
        gam_all = _dot_hi(incl.astype(F32), g_all)
        gam_t = gam_all.T
        q = q_ref[...]
        k = k_ref[...]
        v = v_ref[...]
        for h in range(N_HEADS):
            jg = L_AB + d * 2 * N_HEADS + h
            jb = jg + N_HEADS
            gc = gam_all[:, jg:jg + 1]
            grow = gam_t[jg:jg + 1, :]
            beta = beta_all[:, jb:jb + 1]
            qh, kh, vh = q[:, _hs(h)], k[:, _hs(h)], v[:, _hs(h)]
            s_prev = s_scr[d, h]
            decay = jnp.exp(jnp.where(incl, gc - grow, -jnp.inf))
            kk = _dot_nt(kh, kh)
            n = jnp.where(strict, beta * kk * decay, 0.0)
            t = _unit_tri_inverse(n)
            rhs = jnp.concatenate([vh * beta, kh * (beta * jnp.exp(gc))], axis=1)
            sol = _dot_hi(t, rhs)
            w_new = sol[:, :HEAD_DIM] - _dot(sol[:, HEAD_DIM:], s_prev)
            qk = _dot_nt(qh, kh) * decay
            o_ref[:, _hs(h)] = _dot(qh * jnp.exp(gc), s_prev) + _dot(qk, w_new)
            g_end = gc[last_row:last_row + 1, :]
            s_scr[d, h] = jnp.exp(g_end) * s_prev + _dot_tn(kh * jnp.exp(g_end - gc), w_new)

    @pl.when(jnp.logical_and(last, jnp.logical_not(lat)))
    def _():
        s_out[...] = s_scr[...]


def _gdn(zq, z, seqs, l, alog_row, dtb_row, state_gdn):
    rows = z.shape[0]
    o_f, o_b = _scan_out_specs(seqs)
    st_shape = (N_DIR, N_HEADS, HEAD_DIM, HEAD_DIM)
    tabs = seqs.scan_tables()
    nsteps = int(tabs[0].shape[0])

    def spec(which, blk, width):
        if which == 0:
            return pl.BlockSpec((CHUNK, width), lambda i, fw, bw, fl, sq: (fw[i], blk))
        return pl.BlockSpec((CHUNK, width), lambda i, fw, bw, fl, sq: (bw[i], blk))

    in_specs = []
    for which in (0, 1):
        in_specs += [spec(which, 0, MIX_W), spec(which, 1, MIX_W), spec(which, 2, MIX_W),
                     spec(which, Z_SMALL // LANE, LANE)]
    in_specs += [_const_spec((1, LANE)), _const_spec((1, LANE)),
                 pl.BlockSpec((None, None) + st_shape,
                              lambda i, fw, bw, fl, sq: (_lat_idx(seqs, sq, i), l, 0, 0, 0, 0))]
    st_spec = pl.BlockSpec((None,) + st_shape, lambda i, fw, bw, fl, sq: (_ctx_idx(seqs, sq, i), 0, 0, 0, 0))
    gs = pltpu.PrefetchScalarGridSpec(
        num_scalar_prefetch=4, grid=(nsteps,), in_specs=in_specs, out_specs=(o_f, o_b, st_spec),
        scratch_shapes=[pltpu.VMEM(st_shape, F32)])
    return pl.pallas_call(
        _gdn_kernel, grid_spec=gs,
        out_shape=(jax.ShapeDtypeStruct((rows, MIX_W), F32), jax.ShapeDtypeStruct((rows, MIX_W), F32),
                   jax.ShapeDtypeStruct((seqs.nc,) + st_shape, F32)),
        compiler_params=_cparams(1),
    )(*tabs, zq, zq, zq, z, zq, zq, zq, z, alog_row, dtb_row, state_gdn)


CONV_ROWS = 256
HALO = 8


def _conv_kernel(x_ref, prev_ref, next_ref, w_ref, o_ref, xe_ref, *, seqs):
    i = pl.program_id(0)
    start = i * CONV_ROWS
    in_lat = start >= seqs.ctx_rows
    off = jnp.where(in_lat, (start - seqs.ctx_rows) % seqs.tl, start % seqs.tc)
    seq_len = jnp.where(in_lat, seqs.tl, seqs.tc)
    xe_ref[0:HALO, :] = jnp.where(off > 0, prev_ref[...], 0.0)
    xe_ref[HALO:HALO + CONV_ROWS, :] = x_ref[...]
    xe_ref[HALO + CONV_ROWS:, :] = jnp.where(off + CONV_ROWS < seq_len, next_ref[...], 0.0)
    pad = CONV_K // 2
    y = jnp.zeros((CONV_ROWS, 3 * MIX_W), F32)
    for j in range(CONV_K):
        y = y + xe_ref[pl.ds(HALO - pad + j, CONV_ROWS), :] * w_ref[j:j + 1, :]
    y = _silu(y)
    outs = []
    for h in range(3 * N_HEADS):
        seg = y[:, _hs(h)]
        if h < 2 * N_HEADS:
            seg = seg * lax.rsqrt(jnp.sum(seg * seg, axis=-1, keepdims=True) + EPS)
            if h < N_HEADS:
                seg = seg * SCALE
        outs.append(seg)
    o_ref[...] = jnp.concatenate(outs, axis=1)


def _gdn_prep(z, conv_w, seqs):
    rows = z.shape[0]
    cw = 3 * MIX_W
    cb = Z_GDQKV // cw
    nb = rows // CONV_ROWS
    per = CONV_ROWS // HALO
    last8 = rows // HALO - 1
    return pl.pallas_call(
        functools.partial(_conv_kernel, seqs=seqs),
        grid=(nb,),
        in_specs=[pl.BlockSpec((CONV_ROWS, cw), lambda i: (i, cb)),
                  pl.BlockSpec((HALO, cw), lambda i: (jnp.maximum(i * per - 1, 0), cb)),
                  pl.BlockSpec((HALO, cw), lambda i: (jnp.minimum((i + 1) * per, last8), cb)),
                  pl.BlockSpec((8, cw), lambda i: (0, 0))],
        out_specs=pl.BlockSpec((CONV_ROWS, cw), lambda i: (i, 0)),
        out_shape=jax.ShapeDtypeStruct((rows, cw), F32),
        scratch_shapes=[pltpu.VMEM((CONV_ROWS + 2 * HALO, cw), F32)],
        compiler_params=_cparams(1),
    )(z, z, z, conv_w)


ATT_ROWS = 256


def _rope(y, cos, sin_signed):
    n = y.shape[1]
    lane = lax.broadcasted_iota(jnp.int32, y.shape, 1)
    partner = jnp.where(lane % 2 == 0, pltpu.roll(y, n - 1, axis=1), pltpu.roll(y, 1, axis=1))
    reps = n // HEAD_DIM
    c = jnp.concatenate([cos] * reps, axis=1)
    s = jnp.concatenate([sin_signed] * reps, axis=1)
    return y * c + partner * s


def _attn_prep_kernel(q_ref, k_ref, cos_ref, sin_ref, qw_ref, kw_ref, qo_ref, ko_ref):
    cos = cos_ref[...]
    sin = sin_ref[...]
    qo_ref[...] = _rope(_head_rms(q_ref[...], qw_ref[...]), cos, sin)
    ko_ref[...] = _rope(_head_rms(k_ref[...], kw_ref[...]), cos, sin)


def _attn_prep(z, cos_t, sin_t, qw, kw):
    rows = z.shape[0]
    return pl.pallas_call(
        _attn_prep_kernel,
        grid=(rows // ATT_ROWS,),
        in_specs=[pl.BlockSpec((ATT_ROWS, MIX_W), lambda i: (i, Z_ATQ // MIX_W)),
                  pl.BlockSpec((ATT_ROWS, KV_W), lambda i: (i, Z_ATK // KV_W)),
                  pl.BlockSpec((ATT_ROWS, HEAD_DIM), lambda i: (i, 0)),
                  pl.BlockSpec((ATT_ROWS, HEAD_DIM), lambda i: (i, 0)),
                  pl.BlockSpec((1, MIX_W), lambda i: (0, 0)),
                  pl.BlockSpec((1, KV_W), lambda i: (0, 0))],
        out_specs=(pl.BlockSpec((ATT_ROWS, MIX_W), lambda i: (i, 0)),
                   pl.BlockSpec((ATT_ROWS, KV_W), lambda i: (i, 0))),
        out_shape=(jax.ShapeDtypeStruct((rows, MIX_W), F32), jax.ShapeDtypeStruct((rows, KV_W), F32)),
        compiler_params=_cparams(1),
    )(z, z, cos_t, sin_t, qw, kw)


def _attn_kernel(*refs, has_cache):
    if has_cache:
        q_ref, k_ref, v_ref, ck_ref, cv_ref, o_ref = refs
    else:
        q_ref, k_ref, v_ref, o_ref = refs
    k = k_ref[...]
    v = v_ref[...]
    for g in range(N_HEADS // N_KV):
        qh = q_ref[:, _hs(g)]
        s = _dot_nt(qh, k) * SCALE
        m = jnp.max(s, axis=-1, keepdims=True)
        if has_cache:
            sc = _dot_nt(qh, ck_ref[...]) * SCALE
            m = jnp.maximum(m, jnp.max(sc, axis=-1, keepdims=True))
            pc = jnp.exp(sc - m)
        p = jnp.exp(s - m)
        den = jnp.sum(p, axis=-1, keepdims=True)
        if has_cache:
            den = den + jnp.sum(pc, axis=-1, keepdims=True)
        inv = 1.0 / den
        o = _dot(p * inv, v)
        if has_cache:
            o = o + _dot(pc * inv, cv_ref[...])
        o_ref[:, _hs(g)] = o


def _attend(qn, kn, z, row0, nseq, t, y_rows, cache=None, l=0):
    gw = (N_HEADS // N_KV) * HEAD_DIM
    nqb = t // Q_BLOCK
    rb0 = row0 // Q_BLOCK
    sb0 = row0 // t
    vcol = Z_ATV // HEAD_DIM
    in_specs = [pl.BlockSpec((Q_BLOCK, gw), lambda b, kv, qi: (rb0 + b * nqb + qi, kv)),
                pl.BlockSpec((t, HEAD_DIM), lambda b, kv, qi: (sb0 + b, kv)),
                pl.BlockSpec((t, HEAD_DIM), lambda b, kv, qi: (sb0 + b, vcol + kv))]
    args = [qn, kn, z]
    if cache is not None:
        ck, cv = cache
        past = ck.shape[2]
        cspec = pl.BlockSpec((None, None, past, HEAD_DIM), lambda b, kv, qi: (b, l, 0, kv))
        in_specs += [cspec, cspec]
        args += [ck, cv]
    return pl.pallas_call(
        functools.partial(_attn_kernel, has_cache=cache is not None),
        grid=(nseq, N_KV, nqb),
        in_specs=in_specs,
        out_specs=pl.BlockSpec((Q_BLOCK, gw), lambda b, kv, qi: (b * nqb + qi, kv)),
        out_shape=jax.ShapeDtypeStruct((y_rows, MIX_W), F32),
        compiler_params=_cparams(3),
    )(*args)


POST_ROWS = 512


def _post_kernel(gf, gb, mf, mb, df, db, zg, zo, zd, at, nw_ref, o_ref):
    nw = nw_ref[...]
    o_ref[0] = (_head_rms(gf[...] + gb[...], nw[0:1, :]) * _silu(zg[...])).astype(BF16)
    o_ref[1] = (_head_rms(mf[...] + mb[...], nw[1:2, :]) * _sigmoid(zo[...])).astype(BF16)
    o_ref[2] = (_head_rms(df[...] + db[...], nw[2:3, :]) * _silu(zd[...])).astype(BF16)
    o_ref[3] = at[...].astype(BF16)


def _branch_post(outs, z, y_at, norm_w):
    rows = z.shape[0]
    tm = min(POST_ROWS, rows)
    row = lambda c: pl.BlockSpec((tm, MIX_W), lambda i: (i, c))
    in_specs = [row(0)] * 6 + [row(Z_GLA // MIX_W + 3), row(Z_ML // MIX_W + 3), row(Z_GDG // MIX_W),
                               row(0), pl.BlockSpec((8, MIX_W), lambda i: (0, 0))]
    return pl.pallas_call(
        _post_kernel,
        grid=(rows // tm,),
        in_specs=in_specs,
        out_specs=pl.BlockSpec((N_BRANCH, tm, MIX_W), lambda i: (0, i, 0)),
        out_shape=jax.ShapeDtypeStruct((N_BRANCH, rows, MIX_W), BF16),
        compiler_params=_cparams(1),
    )(*outs, z, z, z, y_at, norm_w)


def _merge_kernel(x_ref, g_ref, y_ref, zm_ref, wb_ref, wo_ref, o_ref, acc_ref, *, seqs, tm):
    n = pl.program_id(1)
    r = seqs.mod_row(pl.program_id(0) * tm)
    p = _sigmoid(zm_ref[...]) * jnp.dot(y_ref[...], wb_ref[...], preferred_element_type=F32)

    @pl.when(n == 0)
    def _():
        acc_ref[...] = p

    @pl.when(n > 0)
    def _():
        acc_ref[...] += p

    @pl.when(n == N_BRANCH - 1)
    def _():
        out = jnp.dot(acc_ref[...].astype(BF16), wo_ref[...], preferred_element_type=F32)
        o_ref[...] = x_ref[...] + g_ref[pl.ds(r, 1), :] * out


def _merge(x, mod, ybr, z, wb, wo, seqs, l, tm):
    rows = x.shape[0]
    return pl.pallas_call(
        functools.partial(_merge_kernel, seqs=seqs, tm=tm),
        grid=(rows // tm, N_BRANCH),
        in_specs=[pl.BlockSpec((tm, D_MODEL), lambda m, n: (m, 0)),
                  pl.BlockSpec((None, 8, D_MODEL), lambda m, n: (l, 0, 5)),
                  pl.BlockSpec((None, tm, MIX_W), lambda m, n: (n, m, 0)),
                  pl.BlockSpec((tm, D_MODEL), lambda m, n: (m, n)),
                  pl.BlockSpec((None, None, MIX_W, D_MODEL), lambda m, n: (l, n, 0, 0)),
                  pl.BlockSpec((None, D_MODEL, D_MODEL), lambda m, n: (l, 0, 0), pipeline_mode=pl.Buffered(1))],
        out_specs=pl.BlockSpec((tm, D_MODEL), lambda m, n: (m, 0)),
        out_shape=jax.ShapeDtypeStruct((rows, D_MODEL), F32),
        scratch_shapes=[pltpu.VMEM((tm, D_MODEL), F32)],
        compiler_params=_cparams(2),
    )(x, mod, ybr, z, wb, wo)


def _permute_w_in(w_in):
    o_lr = 4 * MIX_W
    o_ml = o_lr + N_DIR * GLA_RANK
    o_if = o_ml + 4 * MIX_W
    o_gd = o_if + N_DIR * 2 * N_HEADS
    o_ab = o_gd + 4 * MIX_W
    o_at = o_ab + N_DIR * 2 * N_HEADS
    o_mg = o_at + MIX_W + 2 * KV_W
    d = w_in.shape[0]
    pad = jnp.zeros((d, D_MODEL, NZ - Z_SMALL - 64), w_in.dtype)
    parts = [w_in[..., o_mg:o_mg + N_BRANCH * D_MODEL], w_in[..., 0:o_lr], w_in[..., o_ml:o_if],
             w_in[..., o_gd:o_ab], w_in[..., o_at:o_mg],
             w_in[..., o_lr:o_ml], w_in[..., o_if:o_gd], w_in[..., o_ab:o_at], pad]
    return jnp.concatenate(parts, axis=-1).astype(BF16)


def _lane_row(vals, lane0):
    v = vals.reshape(-1).astype(F32)
    return jnp.zeros((1, LANE), F32).at[0, lane0:lane0 + v.shape[0]].set(v)


def _rope_tables(seqs):
    t = seqs.tl
    row = (np.arange(t) // GRID_W).astype(np.float32)
    col = (np.arange(t) % GRID_W).astype(np.float32)
    n_pairs = HEAD_DIM // 4
    inv = jnp.asarray(ROPE_THETA, F32) ** (-jnp.arange(n_pairs, dtype=F32) / n_pairs)
    ang = jnp.concatenate([jnp.asarray(row)[:, None] * inv, jnp.asarray(col)[:, None] * inv], axis=-1)
    cos = jnp.repeat(jnp.cos(ang), 2, axis=-1)
    sin = jnp.repeat(jnp.sin(ang), 2, axis=-1) * jnp.asarray(np.tile([-1.0, 1.0], HEAD_DIM // 2), F32)
    cos = jnp.concatenate([jnp.ones((seqs.ctx_rows, HEAD_DIM), F32)] + [cos] * seqs.nl, axis=0)
    sin = jnp.concatenate([jnp.zeros((seqs.ctx_rows, HEAD_DIM), F32)] + [sin] * seqs.nl, axis=0)
    return cos, sin


def _trunk(seqs, x, cvec, cache_k, cache_v, state_gla, state_c, state_n, state_m, state_gdn,
           w_ada, b_ada, w_ffn_in, w_ffn_out, w_in, gla_w2, gla_b2, gla_norm_w, ml_gate_b, ml_norm_w,
           gd_conv_w, gd_a_log, gd_dt_bias, gd_norm_w, q_norm_w, k_norm_w, w_branch, w_out, final_norm_w,
           tm_ffn=1024, tf=256, tm_in=1024, tn_in=512, tm_mg=256):
    depth = w_in.shape[0]
    rows = seqs.rows
    tm_ffn, tm_in, tm_mg = min(tm_ffn, seqs.tl), min(tm_in, seqs.tl), min(tm_mg, seqs.tl)
    mod = _ada(cvec, w_ada, b_ada)
    w_perm = _permute_w_in(w_in)
    wb = w_branch.astype(BF16)
    wo = w_out.astype(BF16)
    cos_t, sin_t = _rope_tables(seqs)
    fw = final_norm_w.reshape(1, D_MODEL)
    past = cache_k.shape[2]
    ck = cache_k.reshape(cache_k.shape[:2] + (past, KV_W))
    cv = cache_v.reshape(cache_v.shape[:2] + (past, KV_W))
    zpad = jnp.zeros((LANE - GLA_RANK, MIX_W), F32)
    ctx = []
    for l in range(depth):
        x = _ffn(x, mod, w_ffn_in, w_ffn_out, fw, seqs, l, 0, False, tm_ffn, tf)
        z = _inproj(x, mod, w_perm, seqs, l, tm_in, tn_in)

        w2f = jnp.concatenate([gla_w2[l, 0], zpad], axis=0).astype(BF16)
        w2b = jnp.concatenate([zpad[:GLA_RANK], gla_w2[l, 1], zpad[:LANE - 2 * GLA_RANK]], axis=0).astype(BF16)
        gf, gb, st_gla = _gla(z, seqs, l, w2f, w2b, gla_b2[l, 0:1], gla_b2[l, 1:2], state_gla)

        mf, mb, st_c, st_n, st_m = _mlstm(z, seqs, l, _lane_row(ml_gate_b[l], L_IF), state_c, state_n, state_m)

        zq = _gdn_prep(z, jnp.concatenate([gd_conv_w[l], jnp.zeros((8 - CONV_K, 3 * MIX_W), F32)], axis=0), seqs)
        ab_lanes = jnp.concatenate([gd_a_log[l], jnp.zeros((N_DIR, N_HEADS), F32)], axis=1)
        dt_lanes = jnp.concatenate([gd_dt_bias[l], jnp.zeros((N_DIR, N_HEADS), F32)], axis=1)
        df, db, st_gd = _gdn(zq, z, seqs, l, _lane_row(ab_lanes, L_AB), _lane_row(dt_lanes, L_AB), state_gdn)

        qn, kn = _attn_prep(z, cos_t, sin_t, jnp.tile(q_norm_w[l], N_HEADS)[None, :],
                            jnp.tile(k_norm_w[l], N_KV)[None, :])
        y_ctx = _attend(qn, kn, z, 0, seqs.nc, seqs.tc, seqs.ctx_rows)
        y_lat = _attend(qn, kn, z, seqs.ctx_rows, seqs.nl, seqs.tl, seqs.nl * seqs.tl, cache=(ck, cv), l=l)
        y_at = jnp.concatenate([y_ctx, y_lat], axis=0)

        norm_w = jnp.stack([jnp.tile(w[l], N_HEADS) for w in (gla_norm_w, ml_norm_w, gd_norm_w)]
                           + [jnp.zeros((MIX_W,), F32)] * 5, axis=0)
        ybr = _branch_post((gf, gb, mf, mb, df, db), z, y_at, norm_w)
        x = _merge(x, mod, ybr, z, wb, wo, seqs, l, tm_mg)
        x = _ffn(x, mod, w_ffn_in, w_ffn_out, fw, seqs, l, 1, l == depth - 1, tm_ffn, tf)

        nc, tc = seqs.nc, seqs.tc
        ctx.append(dict(
            k=kn[:seqs.ctx_rows].reshape(nc, tc, N_KV, HEAD_DIM),
            v=z[:seqs.ctx_rows, Z_ATV:Z_ATV + KV_W].reshape(nc, tc, N_KV, HEAD_DIM),
            gla=st_gla, mc=st_c,
            mn=st_n.reshape(nc, N_DIR, N_HEADS, HEAD_DIM),
            mm=st_m[:, :, 0].reshape(nc, N_DIR, N_HEADS),
            gd=st_gd))
    return x, ctx


def kernel(x_prompt, x_sample, cache_k, cache_v, state_gla, state_mlstm_c, state_mlstm_n, state_mlstm_m,
           state_gdn, c, c_ctx, w_ada, b_ada, w_ffn_in, w_ffn_out, w_in, gla_w2, gla_b2, gla_norm_w,
           ml_gate_b, ml_norm_w, gd_conv_w, gd_a_log, gd_dt_bias, gd_norm_w, q_norm_w, k_norm_w,
           w_branch, w_out, final_norm_w):
    nc, tc, _ = x_prompt.shape
    nl, tl, _ = x_sample.shape
    seqs = _Seqs(nc, tc, nl, tl)
    x = jnp.concatenate([x_prompt.reshape(nc * tc, D_MODEL), x_sample.reshape(nl * tl, D_MODEL)], axis=0)
    cvec = jnp.concatenate([c_ctx[None, :], c, jnp.zeros((8 - 1 - nl, D_MODEL), F32)], axis=0)
    y, ctx = _trunk(seqs, x, cvec, cache_k, cache_v, state_gla, state_mlstm_c, state_mlstm_n, state_mlstm_m,
                    state_gdn, w_ada, b_ada, w_ffn_in, w_ffn_out, w_in, gla_w2, gla_b2, gla_norm_w,
                    ml_gate_b, ml_norm_w, gd_conv_w, gd_a_log, gd_dt_bias, gd_norm_w, q_norm_w, k_norm_w,
                    w_branch, w_out, final_norm_w)
    y_prompt = y[:nc * tc].reshape(nc, tc, D_MODEL)
    y_sample = y[nc * tc:].reshape(nl, tl, D_MODEL)
    stack = lambda name: jnp.stack([cx[name] for cx in ctx], axis=1)
    return (y_prompt, y_sample, stack('k'), stack('v'), stack('gla'), stack('mc'), stack('mn'),
            stack('mm'), stack('gd'))
```

```python
import functools

import numpy as np
import jax
import jax.numpy as jnp
from jax import lax
from jax.experimental import pallas as pl
from jax.experimental.pallas import tpu as pltpu

F32 = jnp.float32
BF16 = jnp.bfloat16

D_MODEL = 2048
DEPTH = 4
GRID_W = 64
N_HEADS = 4
HEAD_DIM = 128
MIX_W = N_HEADS * HEAD_DIM
N_KV = 2
KV_W = N_KV * HEAD_DIM
GLA_RANK = 16
GLA_TAU = 16.0
CHUNK = 64
SUB = 16
Q_BLOCK = 128
CONV_K = 5
D_FF = 5632
ROPE_THETA = 10000.0
N_BRANCH = 4
N_DIR = 2
N_MOD = 9
EPS = 1e-6
SCALE = HEAD_DIM ** -0.5

Z_MERGE = 0
Z_GLA = N_BRANCH * D_MODEL
Z_ML = Z_GLA + 4 * MIX_W
Z_GDQKV = Z_ML + 4 * MIX_W
Z_GDG = Z_GDQKV + 3 * MIX_W
Z_ATQ = Z_GDG + MIX_W
Z_ATK = Z_ATQ + MIX_W
Z_ATV = Z_ATK + KV_W
Z_SMALL = Z_ATV + KV_W
NZ = 15872
LANE = 128
SUBLANES = 8
L_LR = 0
L_IF = N_DIR * GLA_RANK
L_AB = L_IF + N_DIR * 2 * N_HEADS

VMEM_LIMIT = 56 * 1024 * 1024


def _cparams(n_axes):
    return pltpu.CompilerParams(dimension_semantics=("arbitrary",) * n_axes,
                                vmem_limit_bytes=VMEM_LIMIT)


def _dot(a, b):
    return jnp.dot(a.astype(BF16), b.astype(BF16), preferred_element_type=F32)


def _dot_nt(a, b):
    return lax.dot_general(a.astype(BF16), b.astype(BF16), (((1,), (1,)), ((), ())),
                           preferred_element_type=F32)


def _dot_tn(a, b):
    return lax.dot_general(a.astype(BF16), b.astype(BF16), (((0,), (0,)), ((), ())),
                           preferred_element_type=F32)


def _split(a):
    hi = a.astype(BF16)
    return hi, (a - hi.astype(F32)).astype(BF16)


def _dot_split(a, b):
    ah, al = a
    bh, bl = b
    return (jnp.dot(ah, bh, preferred_element_type=F32) + jnp.dot(ah, bl, preferred_element_type=F32)
            + jnp.dot(al, bh, preferred_element_type=F32))


def _cumsum_mask(mask, x):
    m = mask.astype(BF16)
    x0 = x.astype(BF16)
    r1 = x - x0.astype(F32)
    x1 = r1.astype(BF16)
    x2 = (r1 - x1.astype(F32)).astype(BF16)
    return (jnp.dot(m, x0, preferred_element_type=F32) + jnp.dot(m, x1, preferred_element_type=F32)
            + jnp.dot(m, x2, preferred_element_type=F32))


def _sigmoid(x):
    return 1.0 / (1.0 + jnp.exp(-x))


def _silu(x):
    return x * _sigmoid(x)


def _softplus(x):
    return jnp.maximum(x, 0.0) + jnp.log1p(jnp.exp(-jnp.abs(x)))


def _log_sigmoid(x):
    return -_softplus(-x)


def _rmsmod(x, sc, sh):
    ms = jnp.mean(x * x, axis=-1, keepdims=True)
    return x * lax.rsqrt(ms + EPS) * (1.0 + sc) + sh


def _head_rms(x, w):
    outs = []
    for h in range(x.shape[1] // HEAD_DIM):
        seg = x[:, h * HEAD_DIM:(h + 1) * HEAD_DIM]
        ms = jnp.mean(seg * seg, axis=-1, keepdims=True)
        outs.append(seg * lax.rsqrt(ms + EPS))
    return jnp.concatenate(outs, axis=1) * w


class _Seqs:
    def __init__(self, nc, tc, nl, tl):
        self.nc, self.tc, self.nl, self.tl = nc, tc, nl, tl
        self.ctx_rows = nc * tc
        self.rows = nc * tc + nl * tl

    def mod_row(self, start):
        return jnp.where(start < self.ctx_rows, 0, 1 + (start - self.ctx_rows) // self.tl)

    def scan_tables(self):
        fwd, bwd, flg, sq = [], [], [], []
        base = 0
        for s in range(self.nc + self.nl):
            lat = s >= self.nc
            n = (self.tl if lat else self.tc) // CHUNK
            for j in range(n):
                fwd.append(base + j)
                bwd.append(base + n - 1 - j)
                flg.append((1 if j == 0 else 0) | (2 if j == n - 1 else 0) | (4 if lat else 0))
                sq.append(s)
            base += n
        return tuple(jnp.asarray(np.array(a, np.int32)) for a in (fwd, bwd, flg, sq))


def _ada_kernel(c_ref, w_ref, b_ref, o_ref):
    o_ref[...] = _dot(_silu(c_ref[...]), w_ref[...]) + b_ref[...]


def _ada(cvec, w_ada, b_ada):
    tn = 1024
    nmod = N_MOD * D_MODEL
    depth = w_ada.shape[0]
    return pl.pallas_call(
        _ada_kernel,
        grid=(depth, nmod // tn),
        in_specs=[pl.BlockSpec((8, D_MODEL), lambda l, j: (0, 0)),
                  pl.BlockSpec((None, D_MODEL, tn), lambda l, j: (l, 0, j)),
                  pl.BlockSpec((None, 1, tn), lambda l, j: (l, 0, j))],
        out_specs=pl.BlockSpec((None, 8, tn), lambda l, j: (l, 0, j)),
        out_shape=jax.ShapeDtypeStruct((depth, 8, nmod), F32),
        compiler_params=_cparams(2),
    )(cvec, w_ada, b_ada.reshape(depth, 1, nmod))


def _ffn_kernel(x_ref, sh_ref, sc_ref, g_ref, wg_ref, wu_ref, wo_ref, fw_ref, o_ref, h_ref,
                *, seqs, tm, nf, final):
    f = pl.program_id(1)
    r = seqs.mod_row(pl.program_id(0) * tm)

    @pl.when(f == 0)
    def _():
        h_ref[...] = _rmsmod(x_ref[...], sc_ref[pl.ds(r, 1), :], sh_ref[pl.ds(r, 1), :]).astype(BF16)
        o_ref[...] = jnp.zeros_like(o_ref)

    h = h_ref[...]
    g = jnp.dot(h, wg_ref[...].astype(BF16), preferred_element_type=F32)
    u = jnp.dot(h, wu_ref[...].astype(BF16), preferred_element_type=F32)
    a = (_silu(g) * u).astype(BF16)
    o_ref[...] += jnp.dot(a, wo_ref[...].astype(BF16), preferred_element_type=F32)

    @pl.when(f == nf - 1)
    def _():
        y = x_ref[...] + 0.5 * g_ref[pl.ds(r, 1), :] * o_ref[...]
        if final:
            ms = jnp.mean(y * y, axis=-1, keepdims=True)
            y = y * lax.rsqrt(ms + EPS) * fw_ref[...]
        o_ref[...] = y


def _ffn(x, mod, w_in, w_out, fw, seqs, l, i, final, tm, tf):
    rows = x.shape[0]
    nf = D_FF // tf
    j0 = 0 if i == 0 else 6
    kern = functools.partial(_ffn_kernel, seqs=seqs, tm=tm, nf=nf, final=final)
    modspec = lambda j: pl.BlockSpec((None, 8, D_MODEL), lambda m, f: (l, 0, j))
    return pl.pallas_call(
        kern,
        grid=(rows // tm, nf),
        in_specs=[pl.BlockSpec((tm, D_MODEL), lambda m, f: (m, 0), pipeline_mode=pl.Buffered(1)),
                  modspec(j0), modspec(j0 + 1), modspec(j0 + 2),
                  pl.BlockSpec((None, None, D_MODEL, tf), lambda m, f: (l, i, 0, f)),
                  pl.BlockSpec((None, None, D_MODEL, tf), lambda m, f: (l, i, 0, nf + f)),
                  pl.BlockSpec((None, None, tf, D_MODEL), lambda m, f: (l, i, f, 0)),
                  pl.BlockSpec((1, D_MODEL), lambda m, f: (0, 0))],
        out_specs=pl.BlockSpec((tm, D_MODEL), lambda m, f: (m, 0)),
        out_shape=jax.ShapeDtypeStruct((rows, D_MODEL), F32),
        scratch_shapes=[pltpu.VMEM((tm, D_MODEL), BF16)],
        compiler_params=_cparams(2),
    )(x, mod, mod, mod, w_in, w_in, w_out, fw)


def _inproj_kernel(x_ref, sh_ref, sc_ref, w_ref, o_ref, h_ref, *, seqs, tm):
    r = seqs.mod_row(pl.program_id(0) * tm)

    @pl.when(pl.program_id(1) == 0)
    def _():
        h_ref[...] = _rmsmod(x_ref[...], sc_ref[pl.ds(r, 1), :], sh_ref[pl.ds(r, 1), :]).astype(BF16)

    o_ref[...] = jnp.dot(h_ref[...], w_ref[...], preferred_element_type=F32)


def _inproj(x, mod, w_perm, seqs, l, tm, tn):
    rows = x.shape[0]
    kern = functools.partial(_inproj_kernel, seqs=seqs, tm=tm)
    modspec = lambda j: pl.BlockSpec((None, 8, D_MODEL), lambda m, n: (l, 0, j))
    return pl.pallas_call(
        kern,
        grid=(rows // tm, NZ // tn),
        in_specs=[pl.BlockSpec((tm, D_MODEL), lambda m, n: (m, 0), pipeline_mode=pl.Buffered(1)),
                  modspec(3), modspec(4),
                  pl.BlockSpec((None, D_MODEL, tn), lambda m, n: (l, 0, n))],
        out_specs=pl.BlockSpec((tm, tn), lambda m, n: (m, n)),
        out_shape=jax.ShapeDtypeStruct((rows, NZ), F32),
        scratch_shapes=[pltpu.VMEM((tm, D_MODEL), BF16)],
        compiler_params=_cparams(2),
    )(x, mod, mod, w_perm)


def _tri_masks(rev):
    ri = lax.broadcasted_iota(jnp.int32, (CHUNK, CHUNK), 0)
    ci = lax.broadcasted_iota(jnp.int32, (CHUNK, CHUNK), 1)
    if rev:
        return ri <= ci, ri < ci
    return ri >= ci, ri > ci


def _flags(flg):
    first = (flg & 1) != 0
    last = (flg & 2) != 0
    lat = (flg & 4) != 0
    return first, last, lat


def _hs(h):
    return slice(h * HEAD_DIM, (h + 1) * HEAD_DIM)


def _gla_kernel(fwd_ref, bwd_ref, flg_ref, sq_ref,
                qf, kf, vf, sf, qb, kb, vb, sb, w2f, w2b, b2f, b2b, init_ref,
                of_ref, ob_ref, st_out, st_scr, b_scr):
    i = pl.program_id(0)
    first, last, lat = _flags(flg_ref[i])

    @pl.when(jnp.logical_and(first, jnp.logical_not(lat)))
    def _():
        st_scr[...] = jnp.zeros_like(st_scr)

    @pl.when(jnp.logical_and(first, lat))
    def _():
        for d in range(N_DIR):
            for h in range(N_HEADS):
                st_scr[d, h] = init_ref[d, h].T

    dirs = ((qf, kf, vf, sf, w2f, b2f, of_ref), (qb, kb, vb, sb, w2b, b2b, ob_ref))
    for d, (q_ref, k_ref, v_ref, s_ref, w2, b2, o_ref) in enumerate(dirs):
        rev = d == 1
        incl, _ = _tri_masks(rev)
        pre = jnp.dot(s_ref[...].astype(BF16), w2[...], preferred_element_type=F32) + b2[...]
        lg = _log_sigmoid(pre) / GLA_TAU
        b = _cumsum_mask(incl, lg)
        b_scr[d] = b
        bend = jnp.sum(lg, axis=0, keepdims=True)
        q = q_ref[...] * SCALE
        k = k_ref[...]
        v = v_ref[...]
        qd = q * jnp.exp(b)
        kd = k * jnp.exp(bend - b)
        eb = jnp.exp(bend)

        ri = lax.broadcasted_iota(jnp.int32, (CHUNK, CHUNK), 0)
        ci = lax.broadcasted_iota(jnp.int32, (CHUNK, CHUNK), 1)
        if rev:
            off_mask = ci >= (ri // SUB + 1) * SUB
        else:
            off_mask = ci < (ri // SUB) * SUB
        nsub = CHUNK // SUB
        att_rows = [[] for _ in range(N_HEADS)]
        for blk in range(nsub):
            r0 = blk * SUB
            has_src = (blk < nsub - 1) if rev else (blk > 0)
            if not has_src:
                for h in range(N_HEADS):
                    att_rows[h].append(jnp.zeros((SUB, CHUNK), F32))
                continue
            edge = r0 + SUB if rev else r0 - 1
            bref = b[edge:edge + 1, :]
            qe = q[r0:r0 + SUB, :] * jnp.exp(b[r0:r0 + SUB, :] - bref)
            ke = k * jnp.exp(jnp.minimum(bref - b, 0.0))
            for h in range(N_HEADS):
                att_rows[h].append(_dot_nt(qe[:, _hs(h)], ke[:, _hs(h)]))
        o_main = []
        for h in range(N_HEADS):
            att = jnp.where(off_mask, jnp.concatenate(att_rows[h], axis=0), 0.0)
            o_main.append(_dot_nt(qd[:, _hs(h)], st_scr[d, h]) + _dot(att, v[:, _hs(h)]))
        o_main = jnp.concatenate(o_main, axis=1)

        rowi = lax.broadcasted_iota(jnp.int32, (SUBLANES, 1), 0)
        for r0 in range(0, CHUNK, SUBLANES):
            blk0 = (r0 // SUB) * SUB
            bb = b[r0:r0 + SUBLANES, :]
            qq = q[r0:r0 + SUBLANES, :]
            acc = None
            for s in range(blk0, blk0 + SUB):
                if (s > r0 + SUBLANES - 1 and not rev) or (s < r0 and rev):
                    continue
                bs = b_scr[d, s:s + 1, :]
                ks = k_ref[s:s + 1, :]
                vs = v_ref[s:s + 1, :]
                p = qq * ks * jnp.exp(bb - bs)
                valid = (rowi <= s - r0) if rev else (rowi >= s - r0)
                parts = []
                for h in range(N_HEADS):
                    c = jnp.sum(p[:, _hs(h)], axis=-1, keepdims=True)
                    parts.append(jnp.where(valid, c, 0.0) * vs[:, _hs(h)])
                term = jnp.concatenate(parts, axis=1)
                acc = term if acc is None else acc + term
            o_ref[r0:r0 + SUBLANES, :] = o_main[r0:r0 + SUBLANES, :] + acc

        for h in range(N_HEADS):
            st_scr[d, h] = st_scr[d, h] * eb[:, _hs(h)] + _dot_tn(v[:, _hs(h)], kd[:, _hs(h)])

    @pl.when(jnp.logical_and(last, jnp.logical_not(lat)))
    def _():
        for d in range(N_DIR):
            for h in range(N_HEADS):
                st_out[d, h] = st_scr[d, h].T


def _scan_call(kern, z, seqs, col0, extra_in, extra_specs, state_ins, state_in_specs,
               out_shapes, out_specs, scratch):
    tabs = seqs.scan_tables()
    nsteps = int(tabs[0].shape[0])
    c = col0 // MIX_W

    def zspec(which, blk, width):
        if which == 0:
            return pl.BlockSpec((CHUNK, width), lambda i, fw, bw, fl, sq: (fw[i], blk))
        return pl.BlockSpec((CHUNK, width), lambda i, fw, bw, fl, sq: (bw[i], blk))

    in_specs = []
    for which in (0, 1):
        in_specs += [zspec(which, c, MIX_W), zspec(which, c + 1, MIX_W), zspec(which, c + 2, MIX_W),
                     zspec(which, Z_SMALL // LANE, LANE)]
    in_specs += list(extra_specs) + list(state_in_specs)
    gs = pltpu.PrefetchScalarGridSpec(
        num_scalar_prefetch=4, grid=(nsteps,), in_specs=in_specs, out_specs=out_specs,
        scratch_shapes=scratch)
    return pl.pallas_call(kern, grid_spec=gs, out_shape=out_shapes, compiler_params=_cparams(1))(
        *tabs, *([z] * 8), *extra_in, *state_ins)


def _const_spec(shape):
    nd = len(shape)
    return pl.BlockSpec(shape, lambda i, fw, bw, fl, sq: (0,) * nd)


def _scan_out_specs(seqs):
    o_f = pl.BlockSpec((CHUNK, MIX_W), lambda i, fw, bw, fl, sq: (fw[i], 0))
    o_b = pl.BlockSpec((CHUNK, MIX_W), lambda i, fw, bw, fl, sq: (bw[i], 0))
    return o_f, o_b


def _lat_idx(seqs, sq, i):
    return jnp.maximum(sq[i] - seqs.nc, 0)


def _ctx_idx(seqs, sq, i):
    return jnp.minimum(sq[i], seqs.nc - 1)


def _gla(z, seqs, l, w2f, w2b, b2f, b2b, state_gla):
    rows = z.shape[0]
    o_f, o_b = _scan_out_specs(seqs)
    st_shape = (N_DIR, N_HEADS, HEAD_DIM, HEAD_DIM)
    init_spec = pl.BlockSpec((None, None) + st_shape,
                             lambda i, fw, bw, fl, sq: (_lat_idx(seqs, sq, i), l, 0, 0, 0, 0))
    st_spec = pl.BlockSpec((None,) + st_shape,
                           lambda i, fw, bw, fl, sq: (_ctx_idx(seqs, sq, i), 0, 0, 0, 0))
    return _scan_call(
        _gla_kernel, z, seqs, Z_GLA,
        (w2f, w2b, b2f, b2b),
        (_const_spec((LANE, MIX_W)), _const_spec((LANE, MIX_W)), _const_spec((1, MIX_W)), _const_spec((1, MIX_W))),
        (state_gla,), (init_spec,),
        (jax.ShapeDtypeStruct((rows, MIX_W), F32), jax.ShapeDtypeStruct((rows, MIX_W), F32),
         jax.ShapeDtypeStruct((seqs.nc,) + st_shape, F32)),
        (o_f, o_b, st_spec),
        [pltpu.VMEM(st_shape, F32), pltpu.VMEM((N_DIR, CHUNK, MIX_W), F32)])


def _mlstm_kernel(fwd_ref, bwd_ref, flg_ref, sq_ref,
                  qf, kf, vf, sf, qb, kb, vb, sb, bias_ref, c0_ref, n0_ref, m0_ref,
                  of_ref, ob_ref, c_out, n_out, m_out, c_scr, n_scr, m_scr):
    i = pl.program_id(0)
    first, last, lat = _flags(flg_ref[i])

    @pl.when(jnp.logical_and(first, jnp.logical_not(lat)))
    def _():
        c_scr[...] = jnp.zeros_like(c_scr)
        n_scr[...] = jnp.zeros_like(n_scr)
        m_scr[...] = jnp.zeros_like(m_scr)

    @pl.when(jnp.logical_and(first, lat))
    def _():
        c_scr[...] = c0_ref[...]
        n_scr[...] = n0_ref[...]
        m_scr[...] = m0_ref[...]

    dirs = ((qf, kf, vf, sf, of_ref), (qb, kb, vb, sb, ob_ref))
    for d, (q_ref, k_ref, v_ref, s_ref, o_ref) in enumerate(dirs):
        rev = d == 1
        incl, _ = _tri_masks(rev)
        last_row = 0 if rev else CHUNK - 1
        gates = s_ref[...] + bias_ref[...]
        lf_all = _log_sigmoid(gates)
        f_all = _cumsum_mask(incl, lf_all)
        f_all_t = f_all.T
        gates_t = gates.T
        q = q_ref[...]
        k = k_ref[...] * SCALE
        v = v_ref[...]
        for h in range(N_HEADS):
            ji = L_IF + d * 2 * N_HEADS + h
            jf = ji + N_HEADS
            r = d * N_HEADS + h
            fc = f_all[:, jf:jf + 1]
            frow = f_all_t[jf:jf + 1, :]
            igc = gates[:, ji:ji + 1]
            igrow = gates_t[ji:ji + 1, :]
            m_prev = m_scr[r:r + 1, 0:1]
            n_prev = n_scr[r:r + 1, :]
            c_prev = c_scr[d, h]
            qh, kh, vh = q[:, _hs(h)], k[:, _hs(h)], v[:, _hs(h)]
            log_d = jnp.where(incl, fc - frow + igrow, -jnp.inf)
            inter = fc + m_prev
            m_t = jnp.maximum(inter, jnp.max(log_d, axis=-1, keepdims=True))
            dmat = jnp.exp(log_d - m_t)
            a_in = jnp.exp(inter - m_t)
            s = _dot_nt(qh, kh) * dmat
            num = a_in * _dot(qh, c_prev) + _dot(s, vh)
            den = a_in * jnp.sum(qh * n_prev, axis=-1, keepdims=True) + jnp.sum(s, axis=-1, keepdims=True)
            o_ref[:, _hs(h)] = num / jnp.maximum(jnp.abs(den), jnp.exp(-m_t))
            m_new = m_t[last_row:last_row + 1, :]
            f_end = fc[last_row:last_row + 1, :]
            w_end = jnp.exp(f_end - fc + igc - m_new)
            a_end = jnp.exp(f_end + m_prev - m_new)
            kw = kh * w_end
            c_scr[d, h] = a_end * c_prev + _dot_tn(kw, vh)
            n_scr[r:r + 1, :] = a_end * n_prev + jnp.sum(kw, axis=0, keepdims=True)
            m_scr[r:r + 1, :] = jnp.broadcast_to(m_new, (1, LANE))

    @pl.when(jnp.logical_and(last, jnp.logical_not(lat)))
    def _():
        c_out[...] = c_scr[...]
        n_out[...] = n_scr[...]
        m_out[...] = m_scr[...]


def _mlstm(z, seqs, l, bias_row, state_c, state_n, state_m):
    rows = z.shape[0]
    o_f, o_b = _scan_out_specs(seqs)
    st_shape = (N_DIR, N_HEADS, HEAD_DIM, HEAD_DIM)
    nu = N_DIR * N_HEADS
    c0_spec = pl.BlockSpec((None, None) + st_shape,
                           lambda i, fw, bw, fl, sq: (_lat_idx(seqs, sq, i), l, 0, 0, 0, 0))
    v0_spec = pl.BlockSpec((None, None, nu, LANE),
                           lambda i, fw, bw, fl, sq: (_lat_idx(seqs, sq, i), l, 0, 0))
    c_spec = pl.BlockSpec((None,) + st_shape, lambda i, fw, bw, fl, sq: (_ctx_idx(seqs, sq, i), 0, 0, 0, 0))
    v_spec = pl.BlockSpec((None, nu, LANE), lambda i, fw, bw, fl, sq: (_ctx_idx(seqs, sq, i), 0, 0))
    depth = state_c.shape[1]
    n0 = state_n.reshape(seqs.nl, depth, nu, HEAD_DIM)
    m0 = jnp.broadcast_to(state_m.reshape(seqs.nl, depth, nu, 1), (seqs.nl, depth, nu, LANE))
    return _scan_call(
        _mlstm_kernel, z, seqs, Z_ML,
        (bias_row,), (_const_spec((1, LANE)),),
        (state_c, n0, m0), (c0_spec, v0_spec, v0_spec),
        (jax.ShapeDtypeStruct((rows, MIX_W), F32), jax.ShapeDtypeStruct((rows, MIX_W), F32),
         jax.ShapeDtypeStruct((seqs.nc,) + st_shape, F32),
         jax.ShapeDtypeStruct((seqs.nc, nu, LANE), F32), jax.ShapeDtypeStruct((seqs.nc, nu, LANE), F32)),
        (o_f, o_b, c_spec, v_spec, v_spec),
        [pltpu.VMEM(st_shape, F32), pltpu.VMEM((nu, LANE), F32), pltpu.VMEM((nu, LANE), F32)])


def _unit_tri_inverse_all(ns):
    ri = lax.broadcasted_iota(jnp.int32, (CHUNK, CHUNK), 0)
    ci = lax.broadcasted_iota(jnp.int32, (CHUNK, CHUNK), 1)
    same16 = (ri // SUB) == (ci // SUB)
    same32 = (ri // (2 * SUB)) == (ci // (2 * SUB))
    eye = (ri == ci).astype(F32)
    nd = [jnp.where(same16, n, 0.0) for n in ns]
    n1 = [_split(jnp.where(jnp.logical_and(same32, jnp.logical_not(same16)), n, 0.0)) for n in ns]
    n2 = [_split(jnp.where(same32, 0.0, n)) for n in ns]
    t = [eye - x for x in nd]
    nds = [_split(x) for x in nd]
    p = [_dot_split(x, x) for x in nds]
    for level in range(3):
        ps = [_split(x) for x in p]
        t = [x + _dot_split(_split(x), y) for x, y in zip(t, ps)]
        if level < 2:
            p = [_dot_split(y, y) for y in ps]
    for nn in (n1, n2):
        ts = [_split(x) for x in t]
        a = [_dot_split(y, x) for x, y in zip(ts, nn)]
        t = [x - _dot_split(xs, _split(y)) for x, xs, y in zip(t, ts, a)]
    return t


def _gdn_kernel(fwd_ref, bwd_ref, flg_ref, sq_ref,
                qf, kf, vf, sf, qb, kb, vb, sb, alog_ref, dtb_ref, s0_ref,
                of_ref, ob_ref, s_out, s_scr):
    i = pl.program_id(0)
    first, last, lat = _flags(flg_ref[i])

    @pl.when(jnp.logical_and(first, jnp.logical_not(lat)))
    def _():
        s_scr[...] = jnp.zeros_like(s_scr)

    @pl.when(jnp.logical_and(first, lat))
    def _():
        s_scr[...] = s0_ref[...]

    units = []
    dirs = ((qf, kf, vf, sf, of_ref), (qb, kb, vb, sb, ob_ref))
    for d, (q_ref, k_ref, v_ref, s_ref, o_ref) in enumerate(dirs):
        rev = d == 1
        incl, strict = _tri_masks(rev)
        last_row = 0 if rev else CHUNK - 1
        small = s_ref[...]
        g_all = -jnp.exp(alog_ref[...]) * _softplus(small + dtb_ref[...])
        beta_all = _sigmoid(small)
        gam_all = _cumsum_mask(incl, g_all)
        gam_t = gam_all.T
        q = q_ref[...]
        k = k_ref[...]
        v = v_ref[...]
        for h in range(N_HEADS):
            jg = L_AB + d * 2 * N_HEADS + h
            jb = jg + N_HEADS
            gc = gam_all[:, jg:jg + 1]
            units.append(dict(
                d=d, h=h, o_ref=o_ref, incl=incl, strict=strict, gc=gc, grow=gam_t[jg:jg + 1, :],
                beta=beta_all[:, jb:jb + 1], g_end=gc[last_row:last_row + 1, :],
                q=q[:, _hs(h)], k=k[:, _hs(h)], v=v[:, _hs(h)]))

    for u in units:
        u['decay'] = jnp.exp(jnp.where(u['incl'], u['gc'] - u['grow'], -jnp.inf))
    kk = [_dot_nt(u['k'], u['k']) for u in units]
    qk = [_dot_nt(u['q'], u['k']) for u in units]
    ns = [jnp.where(u['strict'], u['beta'] * x * u['decay'], 0.0) for u, x in zip(units, kk)]
    ts = _unit_tri_inverse_all(ns)
    rhs = [_split(jnp.concatenate([u['v'] * u['beta'], u['k'] * (u['beta'] * jnp.exp(u['gc']))], axis=1))
           for u in units]
    sol = [_dot_split(_split(t), r) for t, r in zip(ts, rhs)]
    s_prev = [s_scr[u['d'], u['h']] for u in units]
    w_new = [x[:, :HEAD_DIM] - _dot(x[:, HEAD_DIM:], sp) for x, sp in zip(sol, s_prev)]
    for u, x, w, sp in zip(units, qk, w_new, s_prev):
        u['o_ref'][:, _hs(u['h'])] = _dot(u['q'] * jnp.exp(u['gc']), sp) + _dot(x * u['decay'], w)
    for u, w, sp in zip(units, w_new, s_prev):
        s_scr[u['d'], u['h']] = (jnp.exp(u['g_end']) * sp
                                 + _dot_tn(u['k'] * jnp.exp(u['g_end'] - u['gc']), w))

    @pl.when(jnp.logical_and(last, jnp.logical_not(lat)))
    def _():
        s_out[...] = s_scr[...]


def _gdn(zq, z, seqs, l, alog_row, dtb_row, state_gdn):
    rows = z.shape[0]
    o_f, o_b = _scan_out_specs(seqs)
    st_shape = (N_DIR, N_HEADS, HEAD_DIM, HEAD_DIM)
    tabs = seqs.scan_tables()
    nsteps = int(tabs[0].shape[0])

    def spec(which, blk, width):
        if which == 0:
            return pl.BlockSpec((CHUNK, width), lambda i, fw, bw, fl, sq: (fw[i], blk))
        return pl.BlockSpec((CHUNK, width), lambda i, fw, bw, fl, sq: (bw[i], blk))

    in_specs = []
    for which in (0, 1):
        in_specs += [spec(which, 0, MIX_W), spec(which, 1, MIX_W), spec(which, 2, MIX_W),
                     spec(which, Z_SMALL // LANE, LANE)]
    in_specs += [_const_spec((1, LANE)), _const_spec((1, LANE)),
                 pl.BlockSpec((None, None) + st_shape,
                              lambda i, fw, bw, fl, sq: (_lat_idx(seqs, sq, i), l, 0, 0, 0, 0))]
    st_spec = pl.BlockSpec((None,) + st_shape, lambda i, fw, bw, fl, sq: (_ctx_idx(seqs, sq, i), 0, 0, 0, 0))
    gs = pltpu.PrefetchScalarGridSpec(
        num_scalar_prefetch=4, grid=(nsteps,), in_specs=in_specs, out_specs=(o_f, o_b, st_spec),
        scratch_shapes=[pltpu.VMEM(st_shape, F32)])
    return pl.pallas_call(
        _gdn_kernel, grid_spec=gs,
        out_shape=(jax.ShapeDtypeStruct((rows, MIX_W), F32), jax.ShapeDtypeStruct((rows, MIX_W), F32),
                   jax.ShapeDtypeStruct((seqs.nc,) + st_shape, F32)),
        compiler_params=_cparams(1),
    )(*tabs, zq, zq, zq, z, zq, zq, zq, z, alog_row, dtb_row, state_gdn)


CONV_ROWS = 256
HALO = 8


def _conv_kernel(x_ref, prev_ref, next_ref, w_ref, o_ref, xe_ref, *, seqs):
    i = pl.program_id(0)
    start = i * CONV_ROWS
    in_lat = start >= seqs.ctx_rows
    off = jnp.where(in_lat, (start - seqs.ctx_rows) % seqs.tl, start % seqs.tc)
    seq_len = jnp.where(in_lat, seqs.tl, seqs.tc)
    xe_ref[0:HALO, :] = jnp.where(off > 0, prev_ref[...], 0.0)
    xe_ref[HALO:HALO + CONV_ROWS, :] = x_ref[...]
    xe_ref[HALO + CONV_ROWS:, :] = jnp.where(off + CONV_ROWS < seq_len, next_ref[...], 0.0)
    pad = CONV_K // 2
    y = jnp.zeros((CONV_ROWS, 3 * MIX_W), F32)
    for j in range(CONV_K):
        y = y + xe_ref[pl.ds(HALO - pad + j, CONV_ROWS), :] * w_ref[j:j + 1, :]
    y = _silu(y)
    outs = []
    for h in range(3 * N_HEADS):
        seg = y[:, _hs(h)]
        if h < 2 * N_HEADS:
            seg = seg * lax.rsqrt(jnp.sum(seg * seg, axis=-1, keepdims=True) + EPS)
            if h < N_HEADS:
                seg = seg * SCALE
        outs.append(seg)
    o_ref[...] = jnp.concatenate(outs, axis=1)


def _gdn_prep(z, conv_w, seqs):
    rows = z.shape[0]
    cw = 3 * MIX_W
    cb = Z_GDQKV // cw
    nb = rows // CONV_ROWS
    per = CONV_ROWS // HALO
    last8 = rows // HALO - 1
    return pl.pallas_call(
        functools.partial(_conv_kernel, seqs=seqs),
        grid=(nb,),
        in_specs=[pl.BlockSpec((CONV_ROWS, cw), lambda i: (i, cb)),
                  pl.BlockSpec((HALO, cw), lambda i: (jnp.maximum(i * per - 1, 0), cb)),
                  pl.BlockSpec((HALO, cw), lambda i: (jnp.minimum((i + 1) * per, last8), cb)),
                  pl.BlockSpec((8, cw), lambda i: (0, 0))],
        out_specs=pl.BlockSpec((CONV_ROWS, cw), lambda i: (i, 0)),
        out_shape=jax.ShapeDtypeStruct((rows, cw), F32),
        scratch_shapes=[pltpu.VMEM((CONV_ROWS + 2 * HALO, cw), F32)],
        compiler_params=_cparams(1),
    )(z, z, z, conv_w)


ATT_ROWS = 256


def _rope(y, cos, sin_signed):
    n = y.shape[1]
    lane = lax.broadcasted_iota(jnp.int32, y.shape, 1)
    partner = jnp.where(lane % 2 == 0, pltpu.roll(y, n - 1, axis=1), pltpu.roll(y, 1, axis=1))
    reps = n // HEAD_DIM
    c = jnp.concatenate([cos] * reps, axis=1)
    s = jnp.concatenate([sin_signed] * reps, axis=1)
    return y * c + partner * s


def _attn_prep_kernel(q_ref, k_ref, cos_ref, sin_ref, qw_ref, kw_ref, qo_ref, ko_ref):
    cos = cos_ref[...]
    sin = sin_ref[...]
    qo_ref[...] = _rope(_head_rms(q_ref[...], qw_ref[...]), cos, sin)
    ko_ref[...] = _rope(_head_rms(k_ref[...], kw_ref[...]), cos, sin)


def _attn_prep(z, cos_t, sin_t, qw, kw):
    rows = z.shape[0]
    return pl.pallas_call(
        _attn_prep_kernel,
        grid=(rows // ATT_ROWS,),
        in_specs=[pl.BlockSpec((ATT_ROWS, MIX_W), lambda i: (i, Z_ATQ // MIX_W)),
                  pl.BlockSpec((ATT_ROWS, KV_W), lambda i: (i, Z_ATK // KV_W)),
                  pl.BlockSpec((ATT_ROWS, HEAD_DIM), lambda i: (i, 0)),
                  pl.BlockSpec((ATT_ROWS, HEAD_DIM), lambda i: (i, 0)),
                  pl.BlockSpec((1, MIX_W), lambda i: (0, 0)),
                  pl.BlockSpec((1, KV_W), lambda i: (0, 0))],
        out_specs=(pl.BlockSpec((ATT_ROWS, MIX_W), lambda i: (i, 0)),
                   pl.BlockSpec((ATT_ROWS, KV_W), lambda i: (i, 0))),
        out_shape=(jax.ShapeDtypeStruct((rows, MIX_W), F32), jax.ShapeDtypeStruct((rows, KV_W), F32)),
        compiler_params=_cparams(1),
    )(z, z, cos_t, sin_t, qw, kw)


def _attn_kernel(*refs, has_cache):
    if has_cache:
        q_ref, k_ref, v_ref, ck_ref, cv_ref, o_ref = refs
    else:
        q_ref, k_ref, v_ref, o_ref = refs
    k = k_ref[...]
    v = v_ref[...]
    for g in range(N_HEADS // N_KV):
        qh = q_ref[:, _hs(g)]
        s = _dot_nt(qh, k) * SCALE
        m = jnp.max(s, axis=-1, keepdims=True)
        if has_cache:
            sc = _dot_nt(qh, ck_ref[...]) * SCALE
            m = jnp.maximum(m, jnp.max(sc, axis=-1, keepdims=True))
            pc = jnp.exp(sc - m)
        p = jnp.exp(s - m)
        den = jnp.sum(p, axis=-1, keepdims=True)
        if has_cache:
            den = den + jnp.sum(pc, axis=-1, keepdims=True)
        inv = 1.0 / den
        o = _dot(p * inv, v)
        if has_cache:
            o = o + _dot(pc * inv, cv_ref[...])
        o_ref[:, _hs(g)] = o


def _attend(qn, kn, z, row0, nseq, t, y_rows, cache=None, l=0):
    gw = (N_HEADS // N_KV) * HEAD_DIM
    nqb = t // Q_BLOCK
    rb0 = row0 // Q_BLOCK
    sb0 = row0 // t
    vcol = Z_ATV // HEAD_DIM
    in_specs = [pl.BlockSpec((Q_BLOCK, gw), lambda b, kv, qi: (rb0 + b * nqb + qi, kv)),
                pl.BlockSpec((t, HEAD_DIM), lambda b, kv, qi: (sb0 + b, kv)),
                pl.BlockSpec((t, HEAD_DIM), lambda b, kv, qi: (sb0 + b, vcol + kv))]
    args = [qn, kn, z]
    if cache is not None:
        ck, cv = cache
        past = ck.shape[2]
        cspec = pl.BlockSpec((None, None, past, HEAD_DIM), lambda b, kv, qi: (b, l, 0, kv))
        in_specs += [cspec, cspec]
        args += [ck, cv]
    return pl.pallas_call(
        functools.partial(_attn_kernel, has_cache=cache is not None),
        grid=(nseq, N_KV, nqb),
        in_specs=in_specs,
        out_specs=pl.BlockSpec((Q_BLOCK, gw), lambda b, kv, qi: (b * nqb + qi, kv)),
        out_shape=jax.ShapeDtypeStruct((y_rows, MIX_W), F32),
        compiler_params=_cparams(3),
    )(*args)


POST_ROWS = 512


def _post_kernel(gf, gb, mf, mb, df, db, zg, zo, zd, at, nw_ref, o_ref):
    nw = nw_ref[...]
    o_ref[0] = (_head_rms(gf[...] + gb[...], nw[0:1, :]) * _silu(zg[...])).astype(BF16)
    o_ref[1] = (_head_rms(mf[...] + mb[...], nw[1:2, :]) * _sigmoid(zo[...])).astype(BF16)
    o_ref[2] = (_head_rms(df[...] + db[...], nw[2:3, :]) * _silu(zd[...])).astype(BF16)
    o_ref[3] = at[...].astype(BF16)


def _branch_post(outs, z, y_at, norm_w):
    rows = z.shape[0]
    tm = min(POST_ROWS, rows)
    row = lambda c: pl.BlockSpec((tm, MIX_W), lambda i: (i, c))
    in_specs = [row(0)] * 6 + [row(Z_GLA // MIX_W + 3), row(Z_ML // MIX_W + 3), row(Z_GDG // MIX_W),
                               row(0), pl.BlockSpec((8, MIX_W), lambda i: (0, 0))]
    return pl.pallas_call(
        _post_kernel,
        grid=(rows // tm,),
        in_specs=in_specs,
        out_specs=pl.BlockSpec((N_BRANCH, tm, MIX_W), lambda i: (0, i, 0)),
        out_shape=jax.ShapeDtypeStruct((N_BRANCH, rows, MIX_W), BF16),
        compiler_params=_cparams(1),
    )(*outs, z, z, z, y_at, norm_w)


def _merge_kernel(x_ref, g_ref, y_ref, zm_ref, wb_ref, wo_ref, o_ref, acc_ref, *, seqs, tm):
    n = pl.program_id(1)
    r = seqs.mod_row(pl.program_id(0) * tm)
    p = _sigmoid(zm_ref[...]) * jnp.dot(y_ref[...], wb_ref[...], preferred_element_type=F32)

    @pl.when(n == 0)
    def _():
        acc_ref[...] = p

    @pl.when(n > 0)
    def _():
        acc_ref[...] += p

    @pl.when(n == N_BRANCH - 1)
    def _():
        out = jnp.dot(acc_ref[...].astype(BF16), wo_ref[...], preferred_element_type=F32)
        o_ref[...] = x_ref[...] + g_ref[pl.ds(r, 1), :] * out


def _merge(x, mod, ybr, z, wb, wo, seqs, l, tm):
    rows = x.shape[0]
    return pl.pallas_call(
        functools.partial(_merge_kernel, seqs=seqs, tm=tm),
        grid=(rows // tm, N_BRANCH),
        in_specs=[pl.BlockSpec((tm, D_MODEL), lambda m, n: (m, 0)),
                  pl.BlockSpec((None, 8, D_MODEL), lambda m, n: (l, 0, 5)),
                  pl.BlockSpec((None, tm, MIX_W), lambda m, n: (n, m, 0)),
                  pl.BlockSpec((tm, D_MODEL), lambda m, n: (m, n)),
                  pl.BlockSpec((None, None, MIX_W, D_MODEL), lambda m, n: (l, n, 0, 0)),
                  pl.BlockSpec((None, D_MODEL, D_MODEL), lambda m, n: (l, 0, 0), pipeline_mode=pl.Buffered(1))],
        out_specs=pl.BlockSpec((tm, D_MODEL), lambda m, n: (m, 0)),
        out_shape=jax.ShapeDtypeStruct((rows, D_MODEL), F32),
        scratch_shapes=[pltpu.VMEM((tm, D_MODEL), F32)],
        compiler_params=_cparams(2),
    )(x, mod, ybr, z, wb, wo)


def _permute_w_in(w_in):
    o_lr = 4 * MIX_W
    o_ml = o_lr + N_DIR * GLA_RANK
    o_if = o_ml + 4 * MIX_W
    o_gd = o_if + N_DIR * 2 * N_HEADS
    o_ab = o_gd + 4 * MIX_W
    o_at = o_ab + N_DIR * 2 * N_HEADS
    o_mg = o_at + MIX_W + 2 * KV_W
    d = w_in.shape[0]
    pad = jnp.zeros((d, D_MODEL, NZ - Z_SMALL - 64), w_in.dtype)
    parts = [w_in[..., o_mg:o_mg + N_BRANCH * D_MODEL], w_in[..., 0:o_lr], w_in[..., o_ml:o_if],
             w_in[..., o_gd:o_ab], w_in[..., o_at:o_mg],
             w_in[..., o_lr:o_ml], w_in[..., o_if:o_gd], w_in[..., o_ab:o_at], pad]
    return jnp.concatenate(parts, axis=-1).astype(BF16)


def _lane_row(vals, lane0):
    v = vals.reshape(-1).astype(F32)
    return jnp.zeros((1, LANE), F32).at[0, lane0:lane0 + v.shape[0]].set(v)


def _rope_tables(seqs):
    t = seqs.tl
    row = (np.arange(t) // GRID_W).astype(np.float32)
    col = (np.arange(t) % GRID_W).astype(np.float32)
    n_pairs = HEAD_DIM // 4
    inv = jnp.asarray(ROPE_THETA, F32) ** (-jnp.arange(n_pairs, dtype=F32) / n_pairs)
    ang = jnp.concatenate([jnp.asarray(row)[:, None] * inv, jnp.asarray(col)[:, None] * inv], axis=-1)
    cos = jnp.repeat(jnp.cos(ang), 2, axis=-1)
    sin = jnp.repeat(jnp.sin(ang), 2, axis=-1) * jnp.asarray(np.tile([-1.0, 1.0], HEAD_DIM // 2), F32)
    cos = jnp.concatenate([jnp.ones((seqs.ctx_rows, HEAD_DIM), F32)] + [cos] * seqs.nl, axis=0)
    sin = jnp.concatenate([jnp.zeros((seqs.ctx_rows, HEAD_DIM), F32)] + [sin] * seqs.nl, axis=0)
    return cos, sin


def _trunk(seqs, x, cvec, cache_k, cache_v, state_gla, state_c, state_n, state_m, state_gdn,
           w_ada, b_ada, w_ffn_in, w_ffn_out, w_in, gla_w2, gla_b2, gla_norm_w, ml_gate_b, ml_norm_w,
           gd_conv_w, gd_a_log, gd_dt_bias, gd_norm_w, q_norm_w, k_norm_w, w_branch, w_out, final_norm_w,
           tm_ffn=1024, tf=256, tm_in=1024, tn_in=512, tm_mg=256):
    depth = w_in.shape[0]
    rows = seqs.rows
    tm_ffn, tm_in, tm_mg = min(tm_ffn, seqs.tl), min(tm_in, seqs.tl), min(tm_mg, seqs.tl)
    mod = _ada(cvec, w_ada, b_ada)
    w_perm = _permute_w_in(w_in)
    wb = w_branch.astype(BF16)
    wo = w_out.astype(BF16)
    cos_t, sin_t = _rope_tables(seqs)
    fw = final_norm_w.reshape(1, D_MODEL)
    past = cache_k.shape[2]
    ck = cache_k.reshape(cache_k.shape[:2] + (past, KV_W))
    cv = cache_v.reshape(cache_v.shape[:2] + (past, KV_W))
    zpad = jnp.zeros((LANE - GLA_RANK, MIX_W), F32)
    ctx = []
    for l in range(depth):
        x = _ffn(x, mod, w_ffn_in, w_ffn_out, fw, seqs, l, 0, False, tm_ffn, tf)
        z = _inproj(x, mod, w_perm, seqs, l, tm_in, tn_in)

        w2f = jnp.concatenate([gla_w2[l, 0], zpad], axis=0).astype(BF16)
        w2b = jnp.concatenate([zpad[:GLA_RANK], gla_w2[l, 1], zpad[:LANE - 2 * GLA_RANK]], axis=0).astype(BF16)
        gf, gb, st_gla = _gla(z, seqs, l, w2f, w2b, gla_b2[l, 0:1], gla_b2[l, 1:2], state_gla)

        mf, mb, st_c, st_n, st_m = _mlstm(z, seqs, l, _lane_row(ml_gate_b[l], L_IF), state_c, state_n, state_m)

        zq = _gdn_prep(z, jnp.concatenate([gd_conv_w[l], jnp.zeros((8 - CONV_K, 3 * MIX_W), F32)], axis=0), seqs)
        ab_lanes = jnp.concatenate([gd_a_log[l], jnp.zeros((N_DIR, N_HEADS), F32)], axis=1)
        dt_lanes = jnp.concatenate([gd_dt_bias[l], jnp.zeros((N_DIR, N_HEADS), F32)], axis=1)
        df, db, st_gd = _gdn(zq, z, seqs, l, _lane_row(ab_lanes, L_AB), _lane_row(dt_lanes, L_AB), state_gdn)

        qn, kn = _attn_prep(z, cos_t, sin_t, jnp.tile(q_norm_w[l], N_HEADS)[None, :],
                            jnp.tile(k_norm_w[l], N_KV)[None, :])
        y_ctx = _attend(qn, kn, z, 0, seqs.nc, seqs.tc, seqs.ctx_rows)
        y_lat = _attend(qn, kn, z, seqs.ctx_rows, seqs.nl, seqs.tl, seqs.nl * seqs.tl, cache=(ck, cv), l=l)
        y_at = jnp.concatenate([y_ctx, y_lat], axis=0)

        norm_w = jnp.stack([jnp.tile(w[l], N_HEADS) for w in (gla_norm_w, ml_norm_w, gd_norm_w)]
                           + [jnp.zeros((MIX_W,), F32)] * 5, axis=0)
        ybr = _branch_post((gf, gb, mf, mb, df, db), z, y_at, norm_w)
        x = _merge(x, mod, ybr, z, wb, wo, seqs, l, tm_mg)
        x = _ffn(x, mod, w_ffn_in, w_ffn_out, fw, seqs, l, 1, l == depth - 1, tm_ffn, tf)

        nc, tc = seqs.nc, seqs.tc
        ctx.append(dict(
            k=kn[:seqs.ctx_rows].reshape(nc, tc, N_KV, HEAD_DIM),
            v=z[:seqs.ctx_rows, Z_ATV:Z_ATV + KV_W].reshape(nc, tc, N_KV, HEAD_DIM),
            gla=st_gla, mc=st_c,
            mn=st_n.reshape(nc, N_DIR, N_HEADS, HEAD_DIM),
            mm=st_m[:, :, 0].reshape(nc, N_DIR, N_HEADS),
            gd=st_gd))
    return x, ctx


def kernel(x_prompt, x_sample, cache_k, cache_v, state_gla, state_mlstm_c, state_mlstm_n, state_mlstm_m,
           state_gdn, c, c_ctx, w_ada, b_ada, w_ffn_in, w_ffn_out, w_in, gla_w2, gla_b2, gla_norm_w,
           ml_gate_b, ml_norm_w, gd_conv_w, gd_a_log, gd_dt_bias, gd_norm_w, q_norm_w, k_norm_w,
           w_branch, w_out, final_norm_w):
    nc, tc, _ = x_prompt.shape
    nl, tl, _ = x_sample.shape
    seqs = _Seqs(nc, tc, nl, tl)
    x = jnp.concatenate([x_prompt.reshape(nc * tc, D_MODEL), x_sample.reshape(nl * tl, D_MODEL)], axis=0)
    cvec = jnp.concatenate([c_ctx[None, :], c, jnp.zeros((8 - 1 - nl, D_MODEL), F32)], axis=0)
    y, ctx = _trunk(seqs, x, cvec, cache_k, cache_v, state_gla, state_mlstm_c, state_mlstm_n, state_mlstm_m,
                    state_gdn, w_ada, b_ada, w_ffn_in, w_ffn_out, w_in, gla_w2, gla_b2, gla_norm_w,
                    ml_gate_b, ml_norm_w, gd_conv_w, gd_a_log, gd_dt_bias, gd_norm_w, q_norm_w, k_norm_w,
                    w_branch, w_out, final_norm_w)
    y_prompt = y[:nc * tc].reshape(nc, tc, D_MODEL)
    y_sample = y[nc * tc:].reshape(nl, tl, D_MODEL)
    stack = lambda name: jnp.stack([cx[name] for cx in ctx], axis=1)
    return (y_prompt, y_sample, stack('k'), stack('v'), stack('gla'), stack('mc'), stack('mn'),
            stack('mm'), stack('gd'))
```

```python
import functools

import numpy as np
import jax
import jax.numpy as jnp
from jax import lax
from jax.experimental import pallas as pl
from jax.experimental.pallas import tpu as pltpu

F32 = jnp.float32
BF16 = jnp.bfloat16

D_MODEL = 2048
DEPTH = 4
GRID_W = 64
N_HEADS = 4
HEAD_DIM = 128
MIX_W = N_HEADS * HEAD_DIM
N_KV = 2
KV_W = N_KV * HEAD_DIM
GLA_RANK = 16
GLA_TAU = 16.0
CHUNK = 64
SUB = 16
Q_BLOCK = 128
CONV_K = 5
D_FF = 5632
ROPE_THETA = 10000.0
N_BRANCH = 4
N_DIR = 2
N_MOD = 9
EPS = 1e-6
SCALE = HEAD_DIM ** -0.5

Z_MERGE = 0
Z_GLA = N_BRANCH * D_MODEL
Z_ML = Z_GLA + 4 * MIX_W
Z_GDQKV = Z_ML + 4 * MIX_W
Z_GDG = Z_GDQKV + 3 * MIX_W
Z_ATQ = Z_GDG + MIX_W
Z_ATK = Z_ATQ + MIX_W
Z_ATV = Z_ATK + KV_W
Z_SMALL = Z_ATV + KV_W
NZ = 15872
LANE = 128
SUBLANES = 8
L_LR = 0
L_IF = N_DIR * GLA_RANK
L_AB = L_IF + N_DIR * 2 * N_HEADS

VMEM_LIMIT = 56 * 1024 * 1024


def _cparams(n_axes):
    return pltpu.CompilerParams(dimension_semantics=("arbitrary",) * n_axes,
                                vmem_limit_bytes=VMEM_LIMIT)


def _dot(a, b):
    return jnp.dot(a.astype(BF16), b.astype(BF16), preferred_element_type=F32)


def _dot_nt(a, b):
    return lax.dot_general(a.astype(BF16), b.astype(BF16), (((1,), (1,)), ((), ())),
                           preferred_element_type=F32)


def _dot_tn(a, b):
    return lax.dot_general(a.astype(BF16), b.astype(BF16), (((0,), (0,)), ((), ())),
                           preferred_element_type=F32)


def _split(a):
    hi = a.astype(BF16)
    return hi, (a - hi.astype(F32)).astype(BF16)


def _dot_split(a, b):
    ah, al = a
    bh, bl = b
    return (jnp.dot(ah, bh, preferred_element_type=F32) + jnp.dot(ah, bl, preferred_element_type=F32)
            + jnp.dot(al, bh, preferred_element_type=F32))


def _cumsum_mask(mask, x):
    m = mask.astype(BF16)
    x0 = x.astype(BF16)
    r1 = x - x0.astype(F32)
    x1 = r1.astype(BF16)
    x2 = (r1 - x1.astype(F32)).astype(BF16)
    return (jnp.dot(m, x0, preferred_element_type=F32) + jnp.dot(m, x1, preferred_element_type=F32)
            + jnp.dot(m, x2, preferred_element_type=F32))


def _sigmoid(x):
    return 1.0 / (1.0 + jnp.exp(-x))


def _silu(x):
    return x * _sigmoid(x)


def _softplus(x):
    return jnp.maximum(x, 0.0) + jnp.log1p(jnp.exp(-jnp.abs(x)))


def _log_sigmoid(x):
    return -_softplus(-x)


def _rmsmod(x, sc, sh):
    ms = jnp.mean(x * x, axis=-1, keepdims=True)
    return x * lax.rsqrt(ms + EPS) * (1.0 + sc) + sh


def _head_rms(x, w):
    outs = []
    for h in range(x.shape[1] // HEAD_DIM):
        seg = x[:, h * HEAD_DIM:(h + 1) * HEAD_DIM]
        ms = jnp.mean(seg * seg, axis=-1, keepdims=True)
        outs.append(seg * lax.rsqrt(ms + EPS))
    return jnp.concatenate(outs, axis=1) * w


class _Seqs:
    def __init__(self, nc, tc, nl, tl):
        self.nc, self.tc, self.nl, self.tl = nc, tc, nl, tl
        self.ctx_rows = nc * tc
        self.rows = nc * tc + nl * tl

    def mod_row(self, start):
        return jnp.where(start < self.ctx_rows, 0, 1 + (start - self.ctx_rows) // self.tl)

    def scan_tables(self):
        fwd, bwd, flg, sq = [], [], [], []
        base = 0
        for s in range(self.nc + self.nl):
            lat = s >= self.nc
            n = (self.tl if lat else self.tc) // CHUNK
            for j in range(n):
                fwd.append(base + j)
                bwd.append(base + n - 1 - j)
                flg.append((1 if j == 0 else 0) | (2 if j == n - 1 else 0) | (4 if lat else 0))
                sq.append(s)
            base += n
        return tuple(jnp.asarray(np.array(a, np.int32)) for a in (fwd, bwd, flg, sq))


def _ada_kernel(c_ref, w_ref, b_ref, o_ref):
    o_ref[...] = _dot(_silu(c_ref[...]), w_ref[...]) + b_ref[...]


def _ada(cvec, w_ada, b_ada):
    tn = 1024
    nmod = N_MOD * D_MODEL
    depth = w_ada.shape[0]
    return pl.pallas_call(
        _ada_kernel,
        grid=(depth, nmod // tn),
        in_specs=[pl.BlockSpec((8, D_MODEL), lambda l, j: (0, 0)),
                  pl.BlockSpec((None, D_MODEL, tn), lambda l, j: (l, 0, j)),
                  pl.BlockSpec((None, 1, tn), lambda l, j: (l, 0, j))],
        out_specs=pl.BlockSpec((None, 8, tn), lambda l, j: (l, 0, j)),
        out_shape=jax.ShapeDtypeStruct((depth, 8, nmod), F32),
        compiler_params=_cparams(2),
    )(cvec, w_ada, b_ada.reshape(depth, 1, nmod))


def _ffn_kernel(x_ref, sh_ref, sc_ref, g_ref, wg_ref, wu_ref, wo_ref, fw_ref, o_ref, h_ref,
                *, seqs, tm, nf, final):
    f = pl.program_id(1)
    r = seqs.mod_row(pl.program_id(0) * tm)

    @pl.when(f == 0)
    def _():
        h_ref[...] = _rmsmod(x_ref[...], sc_ref[pl.ds(r, 1), :], sh_ref[pl.ds(r, 1), :]).astype(BF16)
        o_ref[...] = jnp.zeros_like(o_ref)

    h = h_ref[...]
    g = jnp.dot(h, wg_ref[...].astype(BF16), preferred_element_type=F32)
    u = jnp.dot(h, wu_ref[...].astype(BF16), preferred_element_type=F32)
    a = (_silu(g) * u).astype(BF16)
    o_ref[...] += jnp.dot(a, wo_ref[...].astype(BF16), preferred_element_type=F32)

    @pl.when(f == nf - 1)
    def _():
        y = x_ref[...] + 0.5 * g_ref[pl.ds(r, 1), :] * o_ref[...]
        if final:
            ms = jnp.mean(y * y, axis=-1, keepdims=True)
            y = y * lax.rsqrt(ms + EPS) * fw_ref[...]
        o_ref[...] = y


def _ffn(x, mod, w_in, w_out, fw, seqs, l, i, final, tm, tf):
    rows = x.shape[0]
    nf = D_FF // tf
    j0 = 0 if i == 0 else 6
    kern = functools.partial(_ffn_kernel, seqs=seqs, tm=tm, nf=nf, final=final)
    modspec = lambda j: pl.BlockSpec((None, 8, D_MODEL), lambda m, f: (l, 0, j))
    return pl.pallas_call(
        kern,
        grid=(rows // tm, nf),
        in_specs=[pl.BlockSpec((tm, D_MODEL), lambda m, f: (m, 0), pipeline_mode=pl.Buffered(1)),
                  modspec(j0), modspec(j0 + 1), modspec(j0 + 2),
                  pl.BlockSpec((None, None, D_MODEL, tf), lambda m, f: (l, i, 0, f)),
                  pl.BlockSpec((None, None, D_MODEL, tf), lambda m, f: (l, i, 0, nf + f)),
                  pl.BlockSpec((None, None, tf, D_MODEL), lambda m, f: (l, i, f, 0)),
                  pl.BlockSpec((1, D_MODEL), lambda m, f: (0, 0))],
        out_specs=pl.BlockSpec((tm, D_MODEL), lambda m, f: (m, 0)),
        out_shape=jax.ShapeDtypeStruct((rows, D_MODEL), F32),
        scratch_shapes=[pltpu.VMEM((tm, D_MODEL), BF16)],
        compiler_params=_cparams(2),
    )(x, mod, mod, mod, w_in, w_in, w_out, fw)


def _inproj_kernel(x_ref, sh_ref, sc_ref, w_ref, o_ref, h_ref, *, seqs, tm):
    r = seqs.mod_row(pl.program_id(0) * tm)

    @pl.when(pl.program_id(1) == 0)
    def _():
        h_ref[...] = _rmsmod(x_ref[...], sc_ref[pl.ds(r, 1), :], sh_ref[pl.ds(r, 1), :]).astype(BF16)

    o_ref[...] = jnp.dot(h_ref[...], w_ref[...], preferred_element_type=F32)


def _inproj(x, mod, w_perm, seqs, l, tm, tn):
    rows = x.shape[0]
    kern = functools.partial(_inproj_kernel, seqs=seqs, tm=tm)
    modspec = lambda j: pl.BlockSpec((None, 8, D_MODEL), lambda m, n: (l, 0, j))
    return pl.pallas_call(
        kern,
        grid=(rows // tm, NZ // tn),
        in_specs=[pl.BlockSpec((tm, D_MODEL), lambda m, n: (m, 0), pipeline_mode=pl.Buffered(1)),
                  modspec(3), modspec(4),
                  pl.BlockSpec((None, D_MODEL, tn), lambda m, n: (l, 0, n))],
        out_specs=pl.BlockSpec((tm, tn), lambda m, n: (m, n)),
        out_shape=jax.ShapeDtypeStruct((rows, NZ), F32),
        scratch_shapes=[pltpu.VMEM((tm, D_MODEL), BF16)],
        compiler_params=_cparams(2),
    )(x, mod, mod, w_perm)


def _tri_masks(rev):
    ri = lax.broadcasted_iota(jnp.int32, (CHUNK, CHUNK), 0)
    ci = lax.broadcasted_iota(jnp.int32, (CHUNK, CHUNK), 1)
    if rev:
        return ri <= ci, ri < ci
    return ri >= ci, ri > ci


def _flags(flg):
    first = (flg & 1) != 0
    last = (flg & 2) != 0
    lat = (flg & 4) != 0
    return first, last, lat


def _hs(h):
    return slice(h * HEAD_DIM, (h + 1) * HEAD_DIM)


def _gla_init(lat, init_ref, st_scr):
    if lat:
        for d in range(N_DIR):
            for h in range(N_HEADS):
                st_scr[d, h] = init_ref[d, h].T
    else:
        st_scr[...] = jnp.zeros_like(st_scr)


def _gla_final(st_out, st_scr):
    for d in range(N_DIR):
        for h in range(N_HEADS):
            st_out[d, h] = st_scr[d, h].T


def _gla_step(qf, kf, vf, sf, qb, kb, vb, sb, w2f, w2b, b2f, b2b, of_ref, ob_ref, st_scr, b_scr):
    dirs = ((qf, kf, vf, sf, w2f, b2f, of_ref), (qb, kb, vb, sb, w2b, b2b, ob_ref))
    for d, (q_ref, k_ref, v_ref, s_ref, w2, b2, o_ref) in enumerate(dirs):
        rev = d == 1
        incl, _ = _tri_masks(rev)
        pre = jnp.dot(s_ref[...].astype(BF16), w2[...], preferred_element_type=F32) + b2[...]
        lg = _log_sigmoid(pre) / GLA_TAU
        b = _cumsum_mask(incl, lg)
        b_scr[d] = b
        bend = jnp.sum(lg, axis=0, keepdims=True)
        q = q_ref[...] * SCALE
        k = k_ref[...]
        v = v_ref[...]
        qd = q * jnp.exp(b)
        kd = k * jnp.exp(bend - b)
        eb = jnp.exp(bend)

        ri = lax.broadcasted_iota(jnp.int32, (CHUNK, CHUNK), 0)
        ci = lax.broadcasted_iota(jnp.int32, (CHUNK, CHUNK), 1)
        if rev:
            off_mask = ci >= (ri // SUB + 1) * SUB
        else:
            off_mask = ci < (ri // SUB) * SUB
        nsub = CHUNK // SUB
        att_rows = [[] for _ in range(N_HEADS)]
        for blk in range(nsub):
            r0 = blk * SUB
            has_src = (blk < nsub - 1) if rev else (blk > 0)
            if not has_src:
                for h in range(N_HEADS):
                    att_rows[h].append(jnp.zeros((SUB, CHUNK), F32))
                continue
            edge = r0 + SUB if rev else r0 - 1
            bref = b[edge:edge + 1, :]
            qe = q[r0:r0 + SUB, :] * jnp.exp(b[r0:r0 + SUB, :] - bref)
            ke = k * jnp.exp(jnp.minimum(bref - b, 0.0))
            for h in range(N_HEADS):
                att_rows[h].append(_dot_nt(qe[:, _hs(h)], ke[:, _hs(h)]))
        o_main = []
        for h in range(N_HEADS):
            att = jnp.where(off_mask, jnp.concatenate(att_rows[h], axis=0), 0.0)
            o_main.append(_dot_nt(qd[:, _hs(h)], st_scr[d, h]) + _dot(att, v[:, _hs(h)]))
        o_main = jnp.concatenate(o_main, axis=1)
        yield

        rowi = lax.broadcasted_iota(jnp.int32, (SUBLANES, 1), 0)
        for r0 in range(0, CHUNK, SUBLANES):
            blk0 = (r0 // SUB) * SUB
            bb = b[r0:r0 + SUBLANES, :]
            qq = q[r0:r0 + SUBLANES, :]
            acc = None
            for s in range(blk0, blk0 + SUB):
                if (s > r0 + SUBLANES - 1 and not rev) or (s < r0 and rev):
                    continue
                bs = b_scr[d, s:s + 1, :]
                ks = k_ref[s:s + 1, :]
                vs = v_ref[s:s + 1, :]
                p = qq * ks * jnp.exp(bb - bs)
                valid = (rowi <= s - r0) if rev else (rowi >= s - r0)
                parts = []
                for h in range(N_HEADS):
                    c = jnp.sum(p[:, _hs(h)], axis=-1, keepdims=True)
                    parts.append(jnp.where(valid, c, 0.0) * vs[:, _hs(h)])
                term = jnp.concatenate(parts, axis=1)
                acc = term if acc is None else acc + term
            o_ref[r0:r0 + SUBLANES, :] = o_main[r0:r0 + SUBLANES, :] + acc
            yield

        for h in range(N_HEADS):
            st_scr[d, h] = st_scr[d, h] * eb[:, _hs(h)] + _dot_tn(v[:, _hs(h)], kd[:, _hs(h)])


def _copy_or_zero(lat, pairs):
    for src, dst in pairs:
        dst[...] = src[...] if lat else jnp.zeros_like(dst)


def _mlstm_step(qf, kf, vf, sf, qb, kb, vb, sb, bias_ref, of_ref, ob_ref, c_scr, n_scr, m_scr):
    dirs = ((qf, kf, vf, sf, of_ref), (qb, kb, vb, sb, ob_ref))
    for d, (q_ref, k_ref, v_ref, s_ref, o_ref) in enumerate(dirs):
        rev = d == 1
        incl, _ = _tri_masks(rev)
        last_row = 0 if rev else CHUNK - 1
        gates = s_ref[...] + bias_ref[...]
        lf_all = _log_sigmoid(gates)
        f_all = _cumsum_mask(incl, lf_all)
        f_all_t = f_all.T
        gates_t = gates.T
        q = q_ref[...]
        k = k_ref[...] * SCALE
        v = v_ref[...]
        for h in range(N_HEADS):
            ji = L_IF + d * 2 * N_HEADS + h
            jf = ji + N_HEADS
            r = d * N_HEADS + h
            fc = f_all[:, jf:jf + 1]
            frow = f_all_t[jf:jf + 1, :]
            igc = gates[:, ji:ji + 1]
            igrow = gates_t[ji:ji + 1, :]
            m_prev = m_scr[r:r + 1, 0:1]
            n_prev = n_scr[r:r + 1, :]
            c_prev = c_scr[d, h]
            qh, kh, vh = q[:, _hs(h)], k[:, _hs(h)], v[:, _hs(h)]
            log_d = jnp.where(incl, fc - frow + igrow, -jnp.inf)
            inter = fc + m_prev
            m_t = jnp.maximum(inter, jnp.max(log_d, axis=-1, keepdims=True))
            dmat = jnp.exp(log_d - m_t)
            a_in = jnp.exp(inter - m_t)
            s = _dot_nt(qh, kh) * dmat
            num = a_in * _dot(qh, c_prev) + _dot(s, vh)
            den = a_in * jnp.sum(qh * n_prev, axis=-1, keepdims=True) + jnp.sum(s, axis=-1, keepdims=True)
            o_ref[:, _hs(h)] = num / jnp.maximum(jnp.abs(den), jnp.exp(-m_t))
            m_new = m_t[last_row:last_row + 1, :]
            f_end = fc[last_row:last_row + 1, :]
            w_end = jnp.exp(f_end - fc + igc - m_new)
            a_end = jnp.exp(f_end + m_prev - m_new)
            kw = kh * w_end
            c_scr[d, h] = a_end * c_prev + _dot_tn(kw, vh)
            n_scr[r:r + 1, :] = a_end * n_prev + jnp.sum(kw, axis=0, keepdims=True)
            m_scr[r:r + 1, :] = jnp.broadcast_to(m_new, (1, LANE))
            yield


def _unit_tri_inverse_all(ns):
    ri = lax.broadcasted_iota(jnp.int32, (CHUNK, CHUNK), 0)
    ci = lax.broadcasted_iota(jnp.int32, (CHUNK, CHUNK), 1)
    same16 = (ri // SUB) == (ci // SUB)
    same32 = (ri // (2 * SUB)) == (ci // (2 * SUB))
    eye = (ri == ci).astype(F32)
    nd = [jnp.where(same16, n, 0.0) for n in ns]
    n1 = [_split(jnp.where(jnp.logical_and(same32, jnp.logical_not(same16)), n, 0.0)) for n in ns]
    n2 = [_split(jnp.where(same32, 0.0, n)) for n in ns]
    t = [eye - x for x in nd]
    nds = [_split(x) for x in nd]
    p = [_dot_split(x, x) for x in nds]
    yield
    for level in range(3):
        ps = [_split(x) for x in p]
        t = [x + _dot_split(_split(x), y) for x, y in zip(t, ps)]
        if level < 2:
            p = [_dot_split(y, y) for y in ps]
        yield
    for nn in (n1, n2):
        ts = [_split(x) for x in t]
        a = [_dot_split(y, x) for x, y in zip(ts, nn)]
        yield
        t = [x - _dot_split(xs, _split(y)) for x, xs, y in zip(t, ts, a)]
        yield
    return t


def _gdn_step(qf, kf, vf, sf, qb, kb, vb, sb, alog_ref, dtb_ref, of_ref, ob_ref, s_scr):
    units = []
    dirs = ((qf, kf, vf, sf, of_ref), (qb, kb, vb, sb, ob_ref))
    for d, (q_ref, k_ref, v_ref, s_ref, o_ref) in enumerate(dirs):
        rev = d == 1
        incl, strict = _tri_masks(rev)
        last_row = 0 if rev else CHUNK - 1
        small = s_ref[...]
        g_all = -jnp.exp(alog_ref[...]) * _softplus(small + dtb_ref[...])
        beta_all = _sigmoid(small)
        gam_all = _cumsum_mask(incl, g_all)
        gam_t = gam_all.T
        q = q_ref[...]
        k = k_ref[...]
        v = v_ref[...]
        for h in range(N_HEADS):
            jg = L_AB + d * 2 * N_HEADS + h
            jb = jg + N_HEADS
            gc = gam_all[:, jg:jg + 1]
            units.append(dict(
                d=d, h=h, o_ref=o_ref, incl=incl, strict=strict, gc=gc, grow=gam_t[jg:jg + 1, :],
                beta=beta_all[:, jb:jb + 1], g_end=gc[last_row:last_row + 1, :],
                q=q[:, _hs(h)], k=k[:, _hs(h)], v=v[:, _hs(h)]))

    for u in units:
        u['decay'] = jnp.exp(jnp.where(u['incl'], u['gc'] - u['grow'], -jnp.inf))
    kk = [_dot_nt(u['k'], u['k']) for u in units]
    qk = [_dot_nt(u['q'], u['k']) for u in units]
    ns = [jnp.where(u['strict'], u['beta'] * x * u['decay'], 0.0) for u, x in zip(units, kk)]
    yield
    ts = yield from _unit_tri_inverse_all(ns)
    rhs = [_split(jnp.concatenate([u['v'] * u['beta'], u['k'] * (u['beta'] * jnp.exp(u['gc']))], axis=1))
           for u in units]
    sol = [_dot_split(_split(t), r) for t, r in zip(ts, rhs)]
    yield
    s_prev = [s_scr[u['d'], u['h']] for u in units]
    w_new = [x[:, :HEAD_DIM] - _dot(x[:, HEAD_DIM:], sp) for x, sp in zip(sol, s_prev)]
    yield
    for u, x, w, sp in zip(units, qk, w_new, s_prev):
        u['o_ref'][:, _hs(u['h'])] = _dot(u['q'] * jnp.exp(u['gc']), sp) + _dot(x * u['decay'], w)
    for u, w, sp in zip(units, w_new, s_prev):
        s_scr[u['d'], u['h']] = (jnp.exp(u['g_end']) * sp
                                 + _dot_tn(u['k'] * jnp.exp(u['g_end'] - u['gc']), w))


N_UNITS = N_DIR * N_HEADS
_DONE = object()
ST_SHAPE = (N_DIR, N_HEADS, HEAD_DIM, HEAD_DIM)


def _scan_kernel(fwd_ref, bwd_ref, flg_ref, sq_ref,
                 gqf, gkf, gvf, gqb, gkb, gvb, mqf, mkf, mvf, mqb, mkb, mvb,
                 dqf, dkf, dvf, dqb, dkb, dvb, sf, sb,
                 w2f, w2b, b2f, b2b, mbias, alog, dtb,
                 g0, c0, n0, m0, s0,
                 g_of, g_ob, m_of, m_ob, d_of, d_ob, g_out, c_out, n_out, m_out, s_out,
                 g_scr, b_scr, c_scr, n_scr, m_scr, s_scr):
    first, last, lat = _flags(flg_ref[pl.program_id(0)])
    carried = ((c0, c_scr), (n0, n_scr), (m0, m_scr), (s0, s_scr))

    @pl.when(jnp.logical_and(first, jnp.logical_not(lat)))
    def _():
        _gla_init(False, g0, g_scr)
        _copy_or_zero(False, carried)

    @pl.when(jnp.logical_and(first, lat))
    def _():
        _gla_init(True, g0, g_scr)
        _copy_or_zero(True, carried)

    stages = [_gdn_step(dqf, dkf, dvf, sf, dqb, dkb, dvb, sb, alog, dtb, d_of, d_ob, s_scr),
              _mlstm_step(mqf, mkf, mvf, sf, mqb, mkb, mvb, sb, mbias, m_of, m_ob, c_scr, n_scr, m_scr),
              _gla_step(gqf, gkf, gvf, sf, gqb, gkb, gvb, sb, w2f, w2b, b2f, b2b, g_of, g_ob, g_scr, b_scr)]
    while stages:
        for g in list(stages):
            if next(g, _DONE) is _DONE:
                stages.remove(g)

    @pl.when(jnp.logical_and(last, jnp.logical_not(lat)))
    def _():
        _gla_final(g_out, g_scr)
        for src, dst in ((c_scr, c_out), (n_scr, n_out), (m_scr, m_out), (s_scr, s_out)):
            dst[...] = src[...]


def _scans(z, zq, seqs, l, w2f, w2b, b2f, b2b, mbias, alog_row, dtb_row,
           state_gla, state_c, state_n, state_m, state_gdn):
    rows = z.shape[0]
    tabs = seqs.scan_tables()
    nsteps = int(tabs[0].shape[0])

    def blk(which, col, width):
        if which == 0:
            return pl.BlockSpec((CHUNK, width), lambda i, fw, bw, fl, sq: (fw[i], col))
        return pl.BlockSpec((CHUNK, width), lambda i, fw, bw, fl, sq: (bw[i], col))

    def const(shape):
        nd = len(shape)
        return pl.BlockSpec(shape, lambda i, fw, bw, fl, sq: (0,) * nd)

    lat_i = lambda sq, i: jnp.maximum(sq[i] - seqs.nc, 0)
    ctx_i = lambda sq, i: jnp.minimum(sq[i], seqs.nc - 1)
    in_specs, args = [], []
    for src, c0 in ((z, Z_GLA // MIX_W), (z, Z_ML // MIX_W), (zq, 0)):
        for which in (0, 1):
            in_specs += [blk(which, c0, MIX_W), blk(which, c0 + 1, MIX_W), blk(which, c0 + 2, MIX_W)]
            args += [src] * 3
    in_specs += [blk(0, Z_SMALL // LANE, LANE), blk(1, Z_SMALL // LANE, LANE)]
    args += [z, z]
    in_specs += [const((LANE, MIX_W)), const((LANE, MIX_W)), const((1, MIX_W)), const((1, MIX_W)),
                 const((1, LANE)), const((1, LANE)), const((1, LANE))]
    args += [w2f, w2b, b2f, b2b, mbias, alog_row, dtb_row]
    st_in = pl.BlockSpec((None, None) + ST_SHAPE, lambda i, fw, bw, fl, sq: (lat_i(sq, i), l, 0, 0, 0, 0))
    vec_in = pl.BlockSpec((None, None, N_UNITS, LANE), lambda i, fw, bw, fl, sq: (lat_i(sq, i), l, 0, 0))
    depth = state_c.shape[1]
    n0 = state_n.reshape(seqs.nl, depth, N_UNITS, HEAD_DIM)
    m0 = jnp.broadcast_to(state_m.reshape(seqs.nl, depth, N_UNITS, 1), (seqs.nl, depth, N_UNITS, LANE))
    in_specs += [st_in, st_in, vec_in, vec_in, st_in]
    args += [state_gla, state_c, n0, m0, state_gdn]

    o_f = pl.BlockSpec((CHUNK, MIX_W), lambda i, fw, bw, fl, sq: (fw[i], 0))
    o_b = pl.BlockSpec((CHUNK, MIX_W), lambda i, fw, bw, fl, sq: (bw[i], 0))
    st_out = pl.BlockSpec((None,) + ST_SHAPE, lambda i, fw, bw, fl, sq: (ctx_i(sq, i), 0, 0, 0, 0))
    vec_out = pl.BlockSpec((None, N_UNITS, LANE), lambda i, fw, bw, fl, sq: (ctx_i(sq, i), 0, 0))
    o_sds = jax.ShapeDtypeStruct((rows, MIX_W), F32)
    st_sds = jax.ShapeDtypeStruct((seqs.nc,) + ST_SHAPE, F32)
    vec_sds = jax.ShapeDtypeStruct((seqs.nc, N_UNITS, LANE), F32)
    gs = pltpu.PrefetchScalarGridSpec(
        num_scalar_prefetch=4, grid=(nsteps,), in_specs=in_specs,
        out_specs=(o_f, o_b, o_f, o_b, o_f, o_b, st_out, st_out, vec_out, vec_out, st_out),
        scratch_shapes=[pltpu.VMEM(ST_SHAPE, F32), pltpu.VMEM((N_DIR, CHUNK, MIX_W), F32),
                        pltpu.VMEM(ST_SHAPE, F32), pltpu.VMEM((N_UNITS, LANE), F32),
                        pltpu.VMEM((N_UNITS, LANE), F32), pltpu.VMEM(ST_SHAPE, F32)])
    return pl.pallas_call(
        _scan_kernel, grid_spec=gs,
        out_shape=(o_sds,) * 6 + (st_sds, st_sds, vec_sds, vec_sds, st_sds),
        compiler_params=_cparams(1),
    )(*tabs, *args)


CONV_ROWS = 256
HALO = 8


def _conv_kernel(x_ref, prev_ref, next_ref, w_ref, o_ref, xe_ref, *, seqs):
    i = pl.program_id(0)
    start = i * CONV_ROWS
    in_lat = start >= seqs.ctx_rows
    off = jnp.where(in_lat, (start - seqs.ctx_rows) % seqs.tl, start % seqs.tc)
    seq_len = jnp.where(in_lat, seqs.tl, seqs.tc)
    xe_ref[0:HALO, :] = jnp.where(off > 0, prev_ref[...], 0.0)
    xe_ref[HALO:HALO + CONV_ROWS, :] = x_ref[...]
    xe_ref[HALO + CONV_ROWS:, :] = jnp.where(off + CONV_ROWS < seq_len, next_ref[...], 0.0)
    pad = CONV_K // 2
    y = jnp.zeros((CONV_ROWS, 3 * MIX_W), F32)
    for j in range(CONV_K):
        y = y + xe_ref[pl.ds(HALO - pad + j, CONV_ROWS), :] * w_ref[j:j + 1, :]
    y = _silu(y)
    outs = []
    for h in range(3 * N_HEADS):
        seg = y[:, _hs(h)]
        if h < 2 * N_HEADS:
            seg = seg * lax.rsqrt(jnp.sum(seg * seg, axis=-1, keepdims=True) + EPS)
            if h < N_HEADS:
                seg = seg * SCALE
        outs.append(seg)
    o_ref[...] = jnp.concatenate(outs, axis=1)


def _gdn_prep(z, conv_w, seqs):
    rows = z.shape[0]
    cw = 3 * MIX_W
    cb = Z_GDQKV // cw
    nb = rows // CONV_ROWS
    per = CONV_ROWS // HALO
    last8 = rows // HALO - 1
    return pl.pallas_call(
        functools.partial(_conv_kernel, seqs=seqs),
        grid=(nb,),
        in_specs=[pl.BlockSpec((CONV_ROWS, cw), lambda i: (i, cb)),
                  pl.BlockSpec((HALO, cw), lambda i: (jnp.maximum(i * per - 1, 0), cb)),
                  pl.BlockSpec((HALO, cw), lambda i: (jnp.minimum((i + 1) * per, last8), cb)),
                  pl.BlockSpec((8, cw), lambda i: (0, 0))],
        out_specs=pl.BlockSpec((CONV_ROWS, cw), lambda i: (i, 0)),
        out_shape=jax.ShapeDtypeStruct((rows, cw), F32),
        scratch_shapes=[pltpu.VMEM((CONV_ROWS + 2 * HALO, cw), F32)],
        compiler_params=_cparams(1),
    )(z, z, z, conv_w)


ATT_ROWS = 256


def _rope(y, cos, sin_signed):
    n = y.shape[1]
    lane = lax.broadcasted_iota(jnp.int32, y.shape, 1)
    partner = jnp.where(lane % 2 == 0, pltpu.roll(y, n - 1, axis=1), pltpu.roll(y, 1, axis=1))
    reps = n // HEAD_DIM
    c = jnp.concatenate([cos] * reps, axis=1)
    s = jnp.concatenate([sin_signed] * reps, axis=1)
    return y * c + partner * s


def _attn_prep_kernel(q_ref, k_ref, cos_ref, sin_ref, qw_ref, kw_ref, qo_ref, ko_ref):
    cos = cos_ref[...]
    sin = sin_ref[...]
    qo_ref[...] = _rope(_head_rms(q_ref[...], qw_ref[...]), cos, sin)
    ko_ref[...] = _rope(_head_rms(k_ref[...], kw_ref[...]), cos, sin)


def _attn_prep(z, cos_t, sin_t, qw, kw):
    rows = z.shape[0]
    return pl.pallas_call(
        _attn_prep_kernel,
        grid=(rows // ATT_ROWS,),
        in_specs=[pl.BlockSpec((ATT_ROWS, MIX_W), lambda i: (i, Z_ATQ // MIX_W)),
                  pl.BlockSpec((ATT_ROWS, KV_W), lambda i: (i, Z_ATK // KV_W)),
                  pl.BlockSpec((ATT_ROWS, HEAD_DIM), lambda i: (i, 0)),
                  pl.BlockSpec((ATT_ROWS, HEAD_DIM), lambda i: (i, 0)),
                  pl.BlockSpec((1, MIX_W), lambda i: (0, 0)),
                  pl.BlockSpec((1, KV_W), lambda i: (0, 0))],
        out_specs=(pl.BlockSpec((ATT_ROWS, MIX_W), lambda i: (i, 0)),
                   pl.BlockSpec((ATT_ROWS, KV_W), lambda i: (i, 0))),
        out_shape=(jax.ShapeDtypeStruct((rows, MIX_W), F32), jax.ShapeDtypeStruct((rows, KV_W), F32)),
        compiler_params=_cparams(1),
    )(z, z, cos_t, sin_t, qw, kw)


def _attn_kernel(*refs, has_cache):
    if has_cache:
        q_ref, k_ref, v_ref, ck_ref, cv_ref, o_ref = refs
    else:
        q_ref, k_ref, v_ref, o_ref = refs
    k = k_ref[...]
    v = v_ref[...]
    for g in range(N_HEADS // N_KV):
        qh = q_ref[:, _hs(g)]
        s = _dot_nt(qh, k) * SCALE
        m = jnp.max(s, axis=-1, keepdims=True)
        if has_cache:
            sc = _dot_nt(qh, ck_ref[...]) * SCALE
            m = jnp.maximum(m, jnp.max(sc, axis=-1, keepdims=True))
            pc = jnp.exp(sc - m)
        p = jnp.exp(s - m)
        den = jnp.sum(p, axis=-1, keepdims=True)
        if has_cache:
            den = den + jnp.sum(pc, axis=-1, keepdims=True)
        inv = 1.0 / den
        o = _dot(p * inv, v)
        if has_cache:
            o = o + _dot(pc * inv, cv_ref[...])
        o_ref[:, _hs(g)] = o


def _attend(qn, kn, z, row0, nseq, t, y_rows, cache=None, l=0):
    gw = (N_HEADS // N_KV) * HEAD_DIM
    nqb = t // Q_BLOCK
    rb0 = row0 // Q_BLOCK
    sb0 = row0 // t
    vcol = Z_ATV // HEAD_DIM
    in_specs = [pl.BlockSpec((Q_BLOCK, gw), lambda b, kv, qi: (rb0 + b * nqb + qi, kv)),
                pl.BlockSpec((t, HEAD_DIM), lambda b, kv, qi: (sb0 + b, kv)),
                pl.BlockSpec((t, HEAD_DIM), lambda b, kv, qi: (sb0 + b, vcol + kv))]
    args = [qn, kn, z]
    if cache is not None:
        ck, cv = cache
        past = ck.shape[2]
        cspec = pl.BlockSpec((None, None, past, HEAD_DIM), lambda b, kv, qi: (b, l, 0, kv))
        in_specs += [cspec, cspec]
        args += [ck, cv]
    return pl.pallas_call(
        functools.partial(_attn_kernel, has_cache=cache is not None),
        grid=(nseq, N_KV, nqb),
        in_specs=in_specs,
        out_specs=pl.BlockSpec((Q_BLOCK, gw), lambda b, kv, qi: (b * nqb + qi, kv)),
        out_shape=jax.ShapeDtypeStruct((y_rows, MIX_W), F32),
        compiler_params=_cparams(3),
    )(*args)


POST_ROWS = 512


def _post_kernel(gf, gb, mf, mb, df, db, zg, zo, zd, at, nw_ref, o_ref):
    nw = nw_ref[...]
    o_ref[0] = (_head_rms(gf[...] + gb[...], nw[0:1, :]) * _silu(zg[...])).astype(BF16)
    o_ref[1] = (_head_rms(mf[...] + mb[...], nw[1:2, :]) * _sigmoid(zo[...])).astype(BF16)
    o_ref[2] = (_head_rms(df[...] + db[...], nw[2:3, :]) * _silu(zd[...])).astype(BF16)
    o_ref[3] = at[...].astype(BF16)


def _branch_post(outs, z, y_at, norm_w):
    rows = z.shape[0]
    tm = min(POST_ROWS, rows)
    row = lambda c: pl.BlockSpec((tm, MIX_W), lambda i: (i, c))
    in_specs = [row(0)] * 6 + [row(Z_GLA // MIX_W + 3), row(Z_ML // MIX_W + 3), row(Z_GDG // MIX_W),
                               row(0), pl.BlockSpec((8, MIX_W), lambda i: (0, 0))]
    return pl.pallas_call(
        _post_kernel,
        grid=(rows // tm,),
        in_specs=in_specs,
        out_specs=pl.BlockSpec((N_BRANCH, tm, MIX_W), lambda i: (0, i, 0)),
        out_shape=jax.ShapeDtypeStruct((N_BRANCH, rows, MIX_W), BF16),
        compiler_params=_cparams(1),
    )(*outs, z, z, z, y_at, norm_w)


def _merge_kernel(x_ref, g_ref, y_ref, zm_ref, wb_ref, wo_ref, o_ref, acc_ref, *, seqs, tm):
    n = pl.program_id(1)
    r = seqs.mod_row(pl.program_id(0) * tm)
    p = _sigmoid(zm_ref[...]) * jnp.dot(y_ref[...], wb_ref[...], preferred_element_type=F32)

    @pl.when(n == 0)
    def _():
        acc_ref[...] = p

    @pl.when(n > 0)
    def _():
        acc_ref[...] += p

    @pl.when(n == N_BRANCH - 1)
    def _():
        out = jnp.dot(acc_ref[...].astype(BF16), wo_ref[...], preferred_element_type=F32)
        o_ref[...] = x_ref[...] + g_ref[pl.ds(r, 1), :] * out


def _merge(x, mod, ybr, z, wb, wo, seqs, l, tm):
    rows = x.shape[0]
    return pl.pallas_call(
        functools.partial(_merge_kernel, seqs=seqs, tm=tm),
        grid=(rows // tm, N_BRANCH),
        in_specs=[pl.BlockSpec((tm, D_MODEL), lambda m, n: (m, 0), pipeline_mode=pl.Buffered(1)),
                  pl.BlockSpec((None, 8, D_MODEL), lambda m, n: (l, 0, 5)),
                  pl.BlockSpec((None, tm, MIX_W), lambda m, n: (n, m, 0)),
                  pl.BlockSpec((tm, D_MODEL), lambda m, n: (m, n)),
                  pl.BlockSpec((None, None, MIX_W, D_MODEL), lambda m, n: (l, n, 0, 0)),
                  pl.BlockSpec((None, D_MODEL, D_MODEL), lambda m, n: (l, 0, 0), pipeline_mode=pl.Buffered(1))],
        out_specs=pl.BlockSpec((tm, D_MODEL), lambda m, n: (m, 0)),
        out_shape=jax.ShapeDtypeStruct((rows, D_MODEL), F32),
        scratch_shapes=[pltpu.VMEM((tm, D_MODEL), F32)],
        compiler_params=_cparams(2),
    )(x, mod, ybr, z, wb, wo)


def _permute_w_in(w_in):
    o_lr = 4 * MIX_W
    o_ml = o_lr + N_DIR * GLA_RANK
    o_if = o_ml + 4 * MIX_W
    o_gd = o_if + N_DIR * 2 * N_HEADS
    o_ab = o_gd + 4 * MIX_W
    o_at = o_ab + N_DIR * 2 * N_HEADS
    o_mg = o_at + MIX_W + 2 * KV_W
    d = w_in.shape[0]
    pad = jnp.zeros((d, D_MODEL, NZ - Z_SMALL - 64), w_in.dtype)
    parts = [w_in[..., o_mg:o_mg + N_BRANCH * D_MODEL], w_in[..., 0:o_lr], w_in[..., o_ml:o_if],
             w_in[..., o_gd:o_ab], w_in[..., o_at:o_mg],
             w_in[..., o_lr:o_ml], w_in[..., o_if:o_gd], w_in[..., o_ab:o_at], pad]
    return jnp.concatenate(parts, axis=-1).astype(BF16)


def _lane_row(vals, lane0):
    v = vals.reshape(-1).astype(F32)
    return jnp.zeros((1, LANE), F32).at[0, lane0:lane0 + v.shape[0]].set(v)


def _rope_tables(seqs):
    t = seqs.tl
    row = (np.arange(t) // GRID_W).astype(np.float32)
    col = (np.arange(t) % GRID_W).astype(np.float32)
    n_pairs = HEAD_DIM // 4
    inv = jnp.asarray(ROPE_THETA, F32) ** (-jnp.arange(n_pairs, dtype=F32) / n_pairs)
    ang = jnp.concatenate([jnp.asarray(row)[:, None] * inv, jnp.asarray(col)[:, None] * inv], axis=-1)
    cos = jnp.repeat(jnp.cos(ang), 2, axis=-1)
    sin = jnp.repeat(jnp.sin(ang), 2, axis=-1) * jnp.asarray(np.tile([-1.0, 1.0], HEAD_DIM // 2), F32)
    cos = jnp.concatenate([jnp.ones((seqs.ctx_rows, HEAD_DIM), F32)] + [cos] * seqs.nl, axis=0)
    sin = jnp.concatenate([jnp.zeros((seqs.ctx_rows, HEAD_DIM), F32)] + [sin] * seqs.nl, axis=0)
    return cos, sin


def _trunk(seqs, x, cvec, cache_k, cache_v, state_gla, state_c, state_n, state_m, state_gdn,
           w_ada, b_ada, w_ffn_in, w_ffn_out, w_in, gla_w2, gla_b2, gla_norm_w, ml_gate_b, ml_norm_w,
           gd_conv_w, gd_a_log, gd_dt_bias, gd_norm_w, q_norm_w, k_norm_w, w_branch, w_out, final_norm_w,
           tm_ffn=1024, tf=256, tm_in=2048, tn_in=512, tm_mg=512):
    depth = w_in.shape[0]
    rows = seqs.rows
    tm_ffn, tm_in, tm_mg = min(tm_ffn, seqs.tl), min(tm_in, seqs.tl), min(tm_mg, seqs.tl)
    mod = _ada(cvec, w_ada, b_ada)
    w_perm = _permute_w_in(w_in)
    wb = w_branch.astype(BF16)
    wo = w_out.astype(BF16)
    cos_t, sin_t = _rope_tables(seqs)
    fw = final_norm_w.reshape(1, D_MODEL)
    past = cache_k.shape[2]
    ck = cache_k.reshape(cache_k.shape[:2] + (past, KV_W))
    cv = cache_v.reshape(cache_v.shape[:2] + (past, KV_W))
    zpad = jnp.zeros((LANE - GLA_RANK, MIX_W), F32)
    ctx = []
    for l in range(depth):
        x = _ffn(x, mod, w_ffn_in, w_ffn_out, fw, seqs, l, 0, False, tm_ffn, tf)
        z = _inproj(x, mod, w_perm, seqs, l, tm_in, tn_in)

        w2f = jnp.concatenate([gla_w2[l, 0], zpad], axis=0).astype(BF16)
        w2b = jnp.concatenate([zpad[:GLA_RANK], gla_w2[l, 1], zpad[:LANE - 2 * GLA_RANK]], axis=0).astype(BF16)
        zq = _gdn_prep(z, jnp.concatenate([gd_conv_w[l], jnp.zeros((8 - CONV_K, 3 * MIX_W), F32)], axis=0), seqs)
        ab_lanes = jnp.concatenate([gd_a_log[l], jnp.zeros((N_DIR, N_HEADS), F32)], axis=1)
        dt_lanes = jnp.concatenate([gd_dt_bias[l], jnp.zeros((N_DIR, N_HEADS), F32)], axis=1)
        gf, gb, mf, mb, df, db, st_gla, st_c, st_n, st_m, st_gd = _scans(
            z, zq, seqs, l, w2f, w2b, gla_b2[l, 0:1], gla_b2[l, 1:2], _lane_row(ml_gate_b[l], L_IF),
            _lane_row(ab_lanes, L_AB), _lane_row(dt_lanes, L_AB),
            state_gla, state_c, state_n, state_m, state_gdn)

        qn, kn = _attn_prep(z, cos_t, sin_t, jnp.tile(q_norm_w[l], N_HEADS)[None, :],
                            jnp.tile(k_norm_w[l], N_KV)[None, :])
        y_ctx = _attend(qn, kn, z, 0, seqs.nc, seqs.tc, seqs.ctx_rows)
        y_lat = _attend(qn, kn, z, seqs.ctx_rows, seqs.nl, seqs.tl, seqs.nl * seqs.tl, cache=(ck, cv), l=l)
        y_at = jnp.concatenate([y_ctx, y_lat], axis=0)

        norm_w = jnp.stack([jnp.tile(w[l], N_HEADS) for w in (gla_norm_w, ml_norm_w, gd_norm_w)]
                           + [jnp.zeros((MIX_W,), F32)] * 5, axis=0)
        ybr = _branch_post((gf, gb, mf, mb, df, db), z, y_at, norm_w)
        x = _merge(x, mod, ybr, z, wb, wo, seqs, l, tm_mg)
        x = _ffn(x, mod, w_ffn_in, w_ffn_out, fw, seqs, l, 1, l == depth - 1, tm_ffn, tf)

        nc, tc = seqs.nc, seqs.tc
        ctx.append(dict(
            k=kn[:seqs.ctx_rows].reshape(nc, tc, N_KV, HEAD_DIM),
            v=z[:seqs.ctx_rows, Z_ATV:Z_ATV + KV_W].reshape(nc, tc, N_KV, HEAD_DIM),
            gla=st_gla, mc=st_c,
            mn=st_n.reshape(nc, N_DIR, N_HEADS, HEAD_DIM),
            mm=st_m[:, :, 0].reshape(nc, N_DIR, N_HEADS),
            gd=st_gd))
    return x, ctx


def kernel(x_prompt, x_sample, cache_k, cache_v, state_gla, state_mlstm_c, state_mlstm_n, state_mlstm_m,
           state_gdn, c, c_ctx, w_ada, b_ada, w_ffn_in, w_ffn_out, w_in, gla_w2, gla_b2, gla_norm_w,
           ml_gate_b, ml_norm_w, gd_conv_w, gd_a_log, gd_dt_bias, gd_norm_w, q_norm_w, k_norm_w,
           w_branch, w_out, final_norm_w):
    nc, tc, _ = x_prompt.shape
    nl, tl, _ = x_sample.shape
    seqs = _Seqs(nc, tc, nl, tl)
    x = jnp.concatenate([x_prompt.reshape(nc * tc, D_MODEL), x_sample.reshape(nl * tl, D_MODEL)], axis=0)
    cvec = jnp.concatenate([c_ctx[None, :], c, jnp.zeros((8 - 1 - nl, D_MODEL), F32)], axis=0)
    y, ctx = _trunk(seqs, x, cvec, cache_k, cache_v, state_gla, state_mlstm_c, state_mlstm_n, state_mlstm_m,
                    state_gdn, w_ada, b_ada, w_ffn_in, w_ffn_out, w_in, gla_w2, gla_b2, gla_norm_w,
                    ml_gate_b, ml_norm_w, gd_conv_w, gd_a_log, gd_dt_bias, gd_norm_w, q_norm_w, k_norm_w,
                    w_branch, w_out, final_norm_w)
    y_prompt = y[:nc * tc].reshape(nc, tc, D_MODEL)
    y_sample = y[nc * tc:].reshape(nl, tl, D_MODEL)
    stack = lambda name: jnp.stack([cx[name] for cx in ctx], axis=1)
    return (y_prompt, y_sample, stack('k'), stack('v'), stack('gla'), stack('mc'), stack('mn'),
            stack('mm'), stack('gd'))
```

```python
import functools

import numpy as np
import jax
import jax.numpy as jnp
from jax import lax
from jax.experimental import pallas as pl
from jax.experimental.pallas import tpu as pltpu

F32 = jnp.float32
BF16 = jnp.bfloat16

D_MODEL = 2048
DEPTH = 4
GRID_W = 64
N_HEADS = 4
HEAD_DIM = 128
MIX_W = N_HEADS * HEAD_DIM
N_KV = 2
KV_W = N_KV * HEAD_DIM
GLA_RANK = 16
GLA_TAU = 16.0
CHUNK = 64
SUB = 16
Q_BLOCK = 128
CONV_K = 5
D_FF = 5632
ROPE_THETA = 10000.0
N_BRANCH = 4
N_DIR = 2
N_MOD = 9
EPS = 1e-6
SCALE = HEAD_DIM ** -0.5

Z_MERGE = 0
Z_GLA = N_BRANCH * D_MODEL
Z_ML = Z_GLA + 4 * MIX_W
Z_GDQKV = Z_ML + 4 * MIX_W
Z_GDG = Z_GDQKV + 3 * MIX_W
Z_ATQ = Z_GDG + MIX_W
Z_ATK = Z_ATQ + MIX_W
Z_ATV = Z_ATK + KV_W
NZ = Z_ATV + KV_W
LANE = 128
SUBLANES = 8
L_LR = 0
L_IF = N_DIR * GLA_RANK
L_AB = L_IF + N_DIR * 2 * N_HEADS

W_LR = 4 * MIX_W
W_ML = W_LR + N_DIR * GLA_RANK
W_IF = W_ML + 4 * MIX_W
W_GD = W_IF + N_DIR * 2 * N_HEADS
W_AB = W_GD + 4 * MIX_W
W_AT = W_AB + N_DIR * 2 * N_HEADS
W_MG = W_AT + MIX_W + 2 * KV_W
IN_TILE = 512
TILE_SRC = ([W_MG + IN_TILE * j for j in range(N_BRANCH * D_MODEL // IN_TILE)]
            + [IN_TILE * j for j in range(4)] + [W_ML + IN_TILE * j for j in range(4)]
            + [W_GD + IN_TILE * j for j in range(4)] + [W_AT + IN_TILE * j for j in range(2)])
TILE_SHIFTS = tuple(sorted({c % IN_TILE for c in TILE_SRC}))

VMEM_LIMIT = 56 * 1024 * 1024


def _cparams(n_axes):
    return pltpu.CompilerParams(dimension_semantics=("arbitrary",) * n_axes,
                                vmem_limit_bytes=VMEM_LIMIT)


def _dot(a, b):
    return jnp.dot(a.astype(BF16), b.astype(BF16), preferred_element_type=F32)


def _dot_nt(a, b):
    return lax.dot_general(a.astype(BF16), b.astype(BF16), (((1,), (1,)), ((), ())),
                           preferred_element_type=F32)


def _dot_tn(a, b):
    return lax.dot_general(a.astype(BF16), b.astype(BF16), (((0,), (0,)), ((), ())),
                           preferred_element_type=F32)


def _split(a):
    hi = a.astype(BF16)
    return hi, (a - hi.astype(F32)).astype(BF16)


def _dot_split(a, b):
    ah, al = a
    bh, bl = b
    return (jnp.dot(ah, bh, preferred_element_type=F32) + jnp.dot(ah, bl, preferred_element_type=F32)
            + jnp.dot(al, bh, preferred_element_type=F32))


def _cumsum_mask(mask, x):
    m = mask.astype(BF16)
    x0 = x.astype(BF16)
    r1 = x - x0.astype(F32)
    x1 = r1.astype(BF16)
    x2 = (r1 - x1.astype(F32)).astype(BF16)
    return (jnp.dot(m, x0, preferred_element_type=F32) + jnp.dot(m, x1, preferred_element_type=F32)
            + jnp.dot(m, x2, preferred_element_type=F32))


def _sigmoid(x):
    return 1.0 / (1.0 + jnp.exp(-x))


def _silu(x):
    return x * _sigmoid(x)


def _softplus(x):
    return jnp.maximum(x, 0.0) + jnp.log1p(jnp.exp(-jnp.abs(x)))


def _log_sigmoid(x):
    return -_softplus(-x)


def _rmsmod(x, sc, sh):
    ms = jnp.mean(x * x, axis=-1, keepdims=True)
    return x * lax.rsqrt(ms + EPS) * (1.0 + sc) + sh


def _head_rms(x, w):
    outs = []
    for h in range(x.shape[1] // HEAD_DIM):
        seg = x[:, h * HEAD_DIM:(h + 1) * HEAD_DIM]
        ms = jnp.mean(seg * seg, axis=-1, keepdims=True)
        outs.append(seg * lax.rsqrt(ms + EPS))
    return jnp.concatenate(outs, axis=1) * w


class _Seqs:
    def __init__(self, nc, tc, nl, tl):
        self.nc, self.tc, self.nl, self.tl = nc, tc, nl, tl
        self.ctx_rows = nc * tc
        self.rows = nc * tc + nl * tl

    def mod_row(self, start):
        return jnp.where(start < self.ctx_rows, 0, 1 + (start - self.ctx_rows) // self.tl)

    def scan_tables(self):
        fwd, bwd, flg, sq = [], [], [], []
        base = 0
        for s in range(self.nc + self.nl):
            lat = s >= self.nc
            n = (self.tl if lat else self.tc) // CHUNK
            for j in range(n):
                fwd.append(base + j)
                bwd.append(base + n - 1 - j)
                flg.append((1 if j == 0 else 0) | (2 if j == n - 1 else 0) | (4 if lat else 0))
                sq.append(s)
            base += n
        return tuple(jnp.asarray(np.array(a, np.int32)) for a in (fwd, bwd, flg, sq))


def _ada_kernel(c_ref, w_ref, b_ref, o_ref):
    o_ref[...] = _dot(_silu(c_ref[...]), w_ref[...]) + b_ref[...]


def _ada(cvec, w_ada, b_ada):
    tn = 1024
    nmod = N_MOD * D_MODEL
    depth = w_ada.shape[0]
    return pl.pallas_call(
        _ada_kernel,
        grid=(depth, nmod // tn),
        in_specs=[pl.BlockSpec((8, D_MODEL), lambda l, j: (0, 0)),
                  pl.BlockSpec((None, D_MODEL, tn), lambda l, j: (l, 0, j)),
                  pl.BlockSpec((None, 1, tn), lambda l, j: (l, 0, j))],
        out_specs=pl.BlockSpec((None, 8, tn), lambda l, j: (l, 0, j)),
        out_shape=jax.ShapeDtypeStruct((depth, 8, nmod), F32),
        compiler_params=_cparams(2),
    )(cvec, w_ada, b_ada.reshape(depth, 1, nmod))


def _ffn_kernel(x_ref, sh_ref, sc_ref, g_ref, wg_ref, wu_ref, wo_ref, fw_ref, o_ref, h_ref,
                *, seqs, tm, nf, final):
    f = pl.program_id(1)
    r = seqs.mod_row(pl.program_id(0) * tm)

    @pl.when(f == 0)
    def _():
        h_ref[...] = _rmsmod(x_ref[...], sc_ref[pl.ds(r, 1), :], sh_ref[pl.ds(r, 1), :]).astype(BF16)
        o_ref[...] = jnp.zeros_like(o_ref)

    h = h_ref[...]
    g = jnp.dot(h, wg_ref[...].astype(BF16), preferred_element_type=F32)
    u = jnp.dot(h, wu_ref[...].astype(BF16), preferred_element_type=F32)
    a = (_silu(g) * u).astype(BF16)
    o_ref[...] += jnp.dot(a, wo_ref[...].astype(BF16), preferred_element_type=F32)

    @pl.when(f == nf - 1)
    def _():
        y = x_ref[...] + 0.5 * g_ref[pl.ds(r, 1), :] * o_ref[...]
        if final:
            ms = jnp.mean(y * y, axis=-1, keepdims=True)
            y = y * lax.rsqrt(ms + EPS) * fw_ref[...]
        o_ref[...] = y


def _ffn(x, mod, w_in, w_out, fw, seqs, l, i, final, tm, tf):
    rows = x.shape[0]
    nf = D_FF // tf
    j0 = 0 if i == 0 else 6
    kern = functools.partial(_ffn_kernel, seqs=seqs, tm=tm, nf=nf, final=final)
    modspec = lambda j: pl.BlockSpec((None, 8, D_MODEL), lambda m, f: (l, 0, j))
    return pl.pallas_call(
        kern,
        grid=(rows // tm, nf),
        in_specs=[pl.BlockSpec((tm, D_MODEL), lambda m, f: (m, 0), pipeline_mode=pl.Buffered(1)),
                  modspec(j0), modspec(j0 + 1), modspec(j0 + 2),
                  pl.BlockSpec((None, None, D_MODEL, tf), lambda m, f: (l, i, 0, f)),
                  pl.BlockSpec((None, None, D_MODEL, tf), lambda m, f: (l, i, 0, nf + f)),
                  pl.BlockSpec((None, None, tf, D_MODEL), lambda m, f: (l, i, f, 0)),
                  pl.BlockSpec((1, D_MODEL), lambda m, f: (0, 0))],
        out_specs=pl.BlockSpec((tm, D_MODEL), lambda m, f: (m, 0)),
        out_shape=jax.ShapeDtypeStruct((rows, D_MODEL), F32),
        scratch_shapes=[pltpu.VMEM((tm, D_MODEL), BF16)],
        compiler_params=_cparams(2),
    )(x, mod, mod, mod, w_in, w_in, w_out, fw)


def _premix_kernel(x_ref, sh_ref, sc_ref, ws_ref, h_ref, zs_ref, *, seqs, tm):
    r = seqs.mod_row(pl.program_id(0) * tm)
    h = _rmsmod(x_ref[...], sc_ref[pl.ds(r, 1), :], sh_ref[pl.ds(r, 1), :]).astype(BF16)
    h_ref[...] = h
    zs_ref[...] = jnp.dot(h, ws_ref[...], preferred_element_type=F32)


def _premix(x, mod, w_small, seqs, l, tm):
    rows = x.shape[0]
    modspec = lambda j: pl.BlockSpec((None, 8, D_MODEL), lambda m: (l, 0, j))
    return pl.pallas_call(
        functools.partial(_premix_kernel, seqs=seqs, tm=tm),
        grid=(rows // tm,),
        in_specs=[pl.BlockSpec((tm, D_MODEL), lambda m: (m, 0)), modspec(3), modspec(4),
                  pl.BlockSpec((D_MODEL, LANE), lambda m: (0, 0))],
        out_specs=(pl.BlockSpec((tm, D_MODEL), lambda m: (m, 0)), pl.BlockSpec((tm, LANE), lambda m: (m, 0))),
        out_shape=(jax.ShapeDtypeStruct((rows, D_MODEL), BF16), jax.ShapeDtypeStruct((rows, LANE), F32)),
        compiler_params=_cparams(1),
    )(x, mod, mod, w_small)


def _inproj_kernel(col_ref, h_ref, wa_ref, wb_ref, o_ref, w_scr):
    @pl.when(pl.program_id(1) == 0)
    def _():
        shift = col_ref[pl.program_id(0)] % IN_TILE
        for dv in TILE_SHIFTS:
            @pl.when(shift == dv)
            def _(dv=dv):
                wa = wa_ref[...]
                if dv == 0:
                    w_scr[...] = wa.astype(BF16)
                else:
                    ra = pltpu.roll(wa, IN_TILE - dv, axis=1)
                    rb = pltpu.roll(wb_ref[...], LANE - dv, axis=1)
                    lane = lax.broadcasted_iota(jnp.int32, (D_MODEL, LANE), 1)
                    w_scr[:, :IN_TILE - LANE] = ra[:, :IN_TILE - LANE].astype(BF16)
                    w_scr[:, IN_TILE - LANE:] = jnp.where(lane < LANE - dv, ra[:, IN_TILE - LANE:], rb).astype(BF16)

    o_ref[...] = jnp.dot(h_ref[...], w_scr[...], preferred_element_type=F32)


def _inproj(h, w_in, l, tm):
    rows = h.shape[0]
    cols = jnp.asarray(np.array(TILE_SRC, np.int32))
    per = IN_TILE // LANE
    gs = pltpu.PrefetchScalarGridSpec(
        num_scalar_prefetch=1, grid=(len(TILE_SRC), rows // tm),
        in_specs=[pl.BlockSpec((tm, D_MODEL), lambda n, m, c: (m, 0)),
                  pl.BlockSpec((None, D_MODEL, IN_TILE), lambda n, m, c: (l, 0, c[n] // IN_TILE)),
                  pl.BlockSpec((None, D_MODEL, LANE), lambda n, m, c: (l, 0, (c[n] // IN_TILE + 1) * per))],
        out_specs=pl.BlockSpec((tm, IN_TILE), lambda n, m, c: (m, n)),
        scratch_shapes=[pltpu.VMEM((D_MODEL, IN_TILE), BF16)])
    return pl.pallas_call(
        _inproj_kernel, grid_spec=gs,
        out_shape=jax.ShapeDtypeStruct((rows, NZ), F32),
        compiler_params=_cparams(2),
    )(cols, h, w_in, w_in)


def _tri_masks(rev):
    ri = lax.broadcasted_iota(jnp.int32, (CHUNK, CHUNK), 0)
    ci = lax.broadcasted_iota(jnp.int32, (CHUNK, CHUNK), 1)
    if rev:
        return ri <= ci, ri < ci
    return ri >= ci, ri > ci


def _flags(flg):
    first = (flg & 1) != 0
    last = (flg & 2) != 0
    lat = (flg & 4) != 0
    return first, last, lat


def _hs(h):
    return slice(h * HEAD_DIM, (h + 1) * HEAD_DIM)


def _gla_init(lat, init_ref, st_scr):
    if lat:
        for d in range(N_DIR):
            for h in range(N_HEADS):
                st_scr[d, h] = init_ref[d, h].T
    else:
        st_scr[...] = jnp.zeros_like(st_scr)


def _gla_final(st_out, st_scr):
    for d in range(N_DIR):
        for h in range(N_HEADS):
            st_out[d, h] = st_scr[d, h].T


def _gla_step(qf, kf, vf, sf, qb, kb, vb, sb, w2f, w2b, b2f, b2b, of_ref, ob_ref, st_scr, b_scr):
    dirs = ((qf, kf, vf, sf, w2f, b2f, of_ref), (qb, kb, vb, sb, w2b, b2b, ob_ref))
    for d, (q_ref, k_ref, v_ref, s_ref, w2, b2, o_ref) in enumerate(dirs):
        rev = d == 1
        incl, _ = _tri_masks(rev)
        pre = jnp.dot(s_ref[...].astype(BF16), w2[...], preferred_element_type=F32) + b2[...]
        lg = _log_sigmoid(pre) / GLA_TAU
        b = _cumsum_mask(incl, lg)
        b_scr[d] = b
        bend = jnp.sum(lg, axis=0, keepdims=True)
        q = q_ref[...] * SCALE
        k = k_ref[...]
        v = v_ref[...]
        qd = q * jnp.exp(b)
        kd = k * jnp.exp(bend - b)
        eb = jnp.exp(bend)

        ri = lax.broadcasted_iota(jnp.int32, (CHUNK, CHUNK), 0)
        ci = lax.broadcasted_iota(jnp.int32, (CHUNK, CHUNK), 1)
        if rev:
            off_mask = ci >= (ri // SUB + 1) * SUB
        else:
            off_mask = ci < (ri // SUB) * SUB
        nsub = CHUNK // SUB
        att_rows = [[] for _ in range(N_HEADS)]
        for blk in range(nsub):
            r0 = blk * SUB
            has_src = (blk < nsub - 1) if rev else (blk > 0)
            if not has_src:
                for h in range(N_HEADS):
                    att_rows[h].append(jnp.zeros((SUB, CHUNK), F32))
                continue
            edge = r0 + SUB if rev else r0 - 1
            bref = b[edge:edge + 1, :]
            qe = q[r0:r0 + SUB, :] * jnp.exp(b[r0:r0 + SUB, :] - bref)
            ke = k * jnp.exp(jnp.minimum(bref - b, 0.0))
            for h in range(N_HEADS):
                att_rows[h].append(_dot_nt(qe[:, _hs(h)], ke[:, _hs(h)]))
        o_main = []
        for h in range(N_HEADS):
            att = jnp.where(off_mask, jnp.concatenate(att_rows[h], axis=0), 0.0)
            o_main.append(_dot_nt(qd[:, _hs(h)], st_scr[d, h]) + _dot(att, v[:, _hs(h)]))
        o_main = jnp.concatenate(o_main, axis=1)
        yield

        rowi = lax.broadcasted_iota(jnp.int32, (SUBLANES, 1), 0)
        for r0 in range(0, CHUNK, SUBLANES):
            blk0 = (r0 // SUB) * SUB
            bb = b[r0:r0 + SUBLANES, :]
            qq = q[r0:r0 + SUBLANES, :]
            acc = None
            for s in range(blk0, blk0 + SUB):
                if (s > r0 + SUBLANES - 1 and not rev) or (s < r0 and rev):
                    continue
                bs = b_scr[d, s:s + 1, :]
                ks = k_ref[s:s + 1, :]
                vs = v_ref[s:s + 1, :]
                p = qq * ks * jnp.exp(bb - bs)
                valid = (rowi <= s - r0) if rev else (rowi >= s - r0)
                parts = []
                for h in range(N_HEADS):
                    c = jnp.sum(p[:, _hs(h)], axis=-1, keepdims=True)
                    parts.append(jnp.where(valid, c, 0.0) * vs[:, _hs(h)])
                term = jnp.concatenate(parts, axis=1)
                acc = term if acc is None else acc + term
            o_ref[r0:r0 + SUBLANES, :] = o_main[r0:r0 + SUBLANES, :] + acc
            yield

        for h in range(N_HEADS):
            st_scr[d, h] = st_scr[d, h] * eb[:, _hs(h)] + _dot_tn(v[:, _hs(h)], kd[:, _hs(h)])


def _copy_or_zero(lat, pairs):
    for src, dst in pairs:
        dst[...] = src[...] if lat else jnp.zeros_like(dst)


def _mlstm_step(qf, kf, vf, sf, qb, kb, vb, sb, bias_ref, of_ref, ob_ref, c_scr, n_scr, m_scr):
    dirs = ((qf, kf, vf, sf, of_ref), (qb, kb, vb, sb, ob_ref))
    for d, (q_ref, k_ref, v_ref, s_ref, o_ref) in enumerate(dirs):
        rev = d == 1
        incl, _ = _tri_masks(rev)
        last_row = 0 if rev else CHUNK - 1
        gates = s_ref[...] + bias_ref[...]
        lf_all = _log_sigmoid(gates)
        f_all = _cumsum_mask(incl, lf_all)
        f_all_t = f_all.T
        gates_t = gates.T
        q = q_ref[...]
        k = k_ref[...] * SCALE
        v = v_ref[...]
        for h in range(N_HEADS):
            ji = L_IF + d * 2 * N_HEADS + h
            jf = ji + N_HEADS
            r = d * N_HEADS + h
            fc = f_all[:, jf:jf + 1]
            frow = f_all_t[jf:jf + 1, :]
            igc = gates[:, ji:ji + 1]
            igrow = gates_t[ji:ji + 1, :]
            m_prev = m_scr[r:r + 1, 0:1]
            n_prev = n_scr[r:r + 1, :]
            c_prev = c_scr[d, h]
            qh, kh, vh = q[:, _hs(h)], k[:, _hs(h)], v[:, _hs(h)]
            log_d = jnp.where(incl, fc - frow + igrow, -jnp.inf)
            inter = fc + m_prev
            m_t = jnp.maximum(inter, jnp.max(log_d, axis=-1, keepdims=True))
            dmat = jnp.exp(log_d - m_t)
            a_in = jnp.exp(inter - m_t)
            s = _dot_nt(qh, kh) * dmat
            num = a_in * _dot(qh, c_prev) + _dot(s, vh)
            den = a_in * jnp.sum(qh * n_prev, axis=-1, keepdims=True) + jnp.sum(s, axis=-1, keepdims=True)
            o_ref[:, _hs(h)] = num / jnp.maximum(jnp.abs(den), jnp.exp(-m_t))
            m_new = m_t[last_row:last_row + 1, :]
            f_end = fc[last_row:last_row + 1, :]
            w_end = jnp.exp(f_end - fc + igc - m_new)
            a_end = jnp.exp(f_end + m_prev - m_new)
            kw = kh * w_end
            c_scr[d, h] = a_end * c_prev + _dot_tn(kw, vh)
            n_scr[r:r + 1, :] = a_end * n_prev + jnp.sum(kw, axis=0, keepdims=True)
            m_scr[r:r + 1, :] = jnp.broadcast_to(m_new, (1, LANE))
            yield


def _unit_tri_inverse_all(ns):
    ri = lax.broadcasted_iota(jnp.int32, (CHUNK, CHUNK), 0)
    ci = lax.broadcasted_iota(jnp.int32, (CHUNK, CHUNK), 1)
    same16 = (ri // SUB) == (ci // SUB)
    same32 = (ri // (2 * SUB)) == (ci // (2 * SUB))
    eye = (ri == ci).astype(F32)
    nd = [jnp.where(same16, n, 0.0) for n in ns]
    n1 = [_split(jnp.where(jnp.logical_and(same32, jnp.logical_not(same16)), n, 0.0)) for n in ns]
    n2 = [_split(jnp.where(same32, 0.0, n)) for n in ns]
    t = [eye - x for x in nd]
    nds = [_split(x) for x in nd]
    p = [_dot_split(x, x) for x in nds]
    yield
    for level in range(3):
        ps = [_split(x) for x in p]
        t = [x + _dot_split(_split(x), y) for x, y in zip(t, ps)]
        if level < 2:
            p = [_dot_split(y, y) for y in ps]
        yield
    for nn in (n1, n2):
        ts = [_split(x) for x in t]
        a = [_dot_split(y, x) for x, y in zip(ts, nn)]
        yield
        t = [x - _dot_split(xs, _split(y)) for x, xs, y in zip(t, ts, a)]
        yield
    return t


def _gdn_step(qf, kf, vf, sf, qb, kb, vb, sb, alog_ref, dtb_ref, of_ref, ob_ref, s_scr):
    units = []
    dirs = ((qf, kf, vf, sf, of_ref), (qb, kb, vb, sb, ob_ref))
    for d, (q_ref, k_ref, v_ref, s_ref, o_ref) in enumerate(dirs):
        rev = d == 1
        incl, strict = _tri_masks(rev)
        last_row = 0 if rev else CHUNK - 1
        small = s_ref[...]
        g_all = -jnp.exp(alog_ref[...]) * _softplus(small + dtb_ref[...])
        beta_all = _sigmoid(small)
        gam_all = _cumsum_mask(incl, g_all)
        gam_t = gam_all.T
        q = q_ref[...]
        k = k_ref[...]
        v = v_ref[...]
        for h in range(N_HEADS):
            jg = L_AB + d * 2 * N_HEADS + h
            jb = jg + N_HEADS
            gc = gam_all[:, jg:jg + 1]
            units.append(dict(
                d=d, h=h, o_ref=o_ref, incl=incl, strict=strict, gc=gc, grow=gam_t[jg:jg + 1, :],
                beta=beta_all[:, jb:jb + 1], g_end=gc[last_row:last_row + 1, :],
                q=q[:, _hs(h)], k=k[:, _hs(h)], v=v[:, _hs(h)]))

    for u in units:
        u['decay'] = jnp.exp(jnp.where(u['incl'], u['gc'] - u['grow'], -jnp.inf))
    kk = [_dot_nt(u['k'], u['k']) for u in units]
    qk = [_dot_nt(u['q'], u['k']) for u in units]
    ns = [jnp.where(u['strict'], u['beta'] * x * u['decay'], 0.0) for u, x in zip(units, kk)]
    yield
    ts = yield from _unit_tri_inverse_all(ns)
    rhs = [_split(jnp.concatenate([u['v'] * u['beta'], u['k'] * (u['beta'] * jnp.exp(u['gc']))], axis=1))
           for u in units]
    sol = [_dot_split(_split(t), r) for t, r in zip(ts, rhs)]
    yield
    s_prev = [s_scr[u['d'], u['h']] for u in units]
    w_new = [x[:, :HEAD_DIM] - _dot(x[:, HEAD_DIM:], sp) for x, sp in zip(sol, s_prev)]
    yield
    for u, x, w, sp in zip(units, qk, w_new, s_prev):
        u['o_ref'][:, _hs(u['h'])] = _dot(u['q'] * jnp.exp(u['gc']), sp) + _dot(x * u['decay'], w)
    for u, w, sp in zip(units, w_new, s_prev):
        s_scr[u['d'], u['h']] = (jnp.exp(u['g_end']) * sp
                                 + _dot_tn(u['k'] * jnp.exp(u['g_end'] - u['gc']), w))


N_UNITS = N_DIR * N_HEADS
_DONE = object()
ST_SHAPE = (N_DIR, N_HEADS, HEAD_DIM, HEAD_DIM)


def _scan_kernel(fwd_ref, bwd_ref, flg_ref, sq_ref,
                 gqf, gkf, gvf, gqb, gkb, gvb, mqf, mkf, mvf, mqb, mkb, mvb,
                 dqf, dkf, dvf, dqb, dkb, dvb, sf, sb,
                 w2f, w2b, b2f, b2b, mbias, alog, dtb,
                 g0, c0, n0, m0, s0,
                 g_of, g_ob, m_of, m_ob, d_of, d_ob, g_out, c_out, n_out, m_out, s_out,
                 g_scr, b_scr, c_scr, n_scr, m_scr, s_scr):
    first, last, lat = _flags(flg_ref[pl.program_id(0)])
    carried = ((c0, c_scr), (n0, n_scr), (m0, m_scr), (s0, s_scr))

    @pl.when(jnp.logical_and(first, jnp.logical_not(lat)))
    def _():
        _gla_init(False, g0, g_scr)
        _copy_or_zero(False, carried)

    @pl.when(jnp.logical_and(first, lat))
    def _():
        _gla_init(True, g0, g_scr)
        _copy_or_zero(True, carried)

    stages = [_gdn_step(dqf, dkf, dvf, sf, dqb, dkb, dvb, sb, alog, dtb, d_of, d_ob, s_scr),
              _mlstm_step(mqf, mkf, mvf, sf, mqb, mkb, mvb, sb, mbias, m_of, m_ob, c_scr, n_scr, m_scr),
              _gla_step(gqf, gkf, gvf, sf, gqb, gkb, gvb, sb, w2f, w2b, b2f, b2b, g_of, g_ob, g_scr, b_scr)]
    while stages:
        for g in list(stages):
            if next(g, _DONE) is _DONE:
                stages.remove(g)

    @pl.when(jnp.logical_and(last, jnp.logical_not(lat)))
    def _():
        _gla_final(g_out, g_scr)
        for src, dst in ((c_scr, c_out), (n_scr, n_out), (m_scr, m_out), (s_scr, s_out)):
            dst[...] = src[...]


def _scans(z, zq, zs, seqs, l, w2f, w2b, b2f, b2b, mbias, alog_row, dtb_row,
           state_gla, state_c, state_n, state_m, state_gdn):
    rows = z.shape[0]
    tabs = seqs.scan_tables()
    nsteps = int(tabs[0].shape[0])

    def blk(which, col, width):
        if which == 0:
            return pl.BlockSpec((CHUNK, width), lambda i, fw, bw, fl, sq: (fw[i], col))
        return pl.BlockSpec((CHUNK, width), lambda i, fw, bw, fl, sq: (bw[i], col))

    def const(shape):
        nd = len(shape)
        return pl.BlockSpec(shape, lambda i, fw, bw, fl, sq: (0,) * nd)

    lat_i = lambda sq, i: jnp.maximum(sq[i] - seqs.nc, 0)
    ctx_i = lambda sq, i: jnp.minimum(sq[i], seqs.nc - 1)
    in_specs, args = [], []
    for src, c0 in ((z, Z_GLA // MIX_W), (z, Z_ML // MIX_W), (zq, 0)):
        for which in (0, 1):
            in_specs += [blk(which, c0, MIX_W), blk(which, c0 + 1, MIX_W), blk(which, c0 + 2, MIX_W)]
            args += [src] * 3
    in_specs += [blk(0, 0, LANE), blk(1, 0, LANE)]
    args += [zs, zs]
    in_specs += [const((LANE, MIX_W)), const((LANE, MIX_W)), const((1, MIX_W)), const((1, MIX_W)),
                 const((1, LANE)), const((1, LANE)), const((1, LANE))]
    args += [w2f, w2b, b2f, b2b, mbias, alog_row, dtb_row]
    st_in = pl.BlockSpec((None, None) + ST_SHAPE, lambda i, fw, bw, fl, sq: (lat_i(sq, i), l, 0, 0, 0, 0))
    vec_in = pl.BlockSpec((None, None, N_UNITS, LANE), lambda i, fw, bw, fl, sq: (lat_i(sq, i), l, 0, 0))
    depth = state_c.shape[1]
    n0 = state_n.reshape(seqs.nl, depth, N_UNITS, HEAD_DIM)
    m0 = jnp.broadcast_to(state_m.reshape(seqs.nl, depth, N_UNITS, 1), (seqs.nl, depth, N_UNITS, LANE))
    in_specs += [st_in, st_in, vec_in, vec_in, st_in]
    args += [state_gla, state_c, n0, m0, state_gdn]

    o_f = pl.BlockSpec((CHUNK, MIX_W), lambda i, fw, bw, fl, sq: (fw[i], 0))
    o_b = pl.BlockSpec((CHUNK, MIX_W), lambda i, fw, bw, fl, sq: (bw[i], 0))
    st_out = pl.BlockSpec((None,) + ST_SHAPE, lambda i, fw, bw, fl, sq: (ctx_i(sq, i), 0, 0, 0, 0))
    vec_out = pl.BlockSpec((None, N_UNITS, LANE), lambda i, fw, bw, fl, sq: (ctx_i(sq, i), 0, 0))
    o_sds = jax.ShapeDtypeStruct((rows, MIX_W), F32)
    st_sds = jax.ShapeDtypeStruct((seqs.nc,) + ST_SHAPE, F32)
    vec_sds = jax.ShapeDtypeStruct((seqs.nc, N_UNITS, LANE), F32)
    gs = pltpu.PrefetchScalarGridSpec(
        num_scalar_prefetch=4, grid=(nsteps,), in_specs=in_specs,
        out_specs=(o_f, o_b, o_f, o_b, o_f, o_b, st_out, st_out, vec_out, vec_out, st_out),
        scratch_shapes=[pltpu.VMEM(ST_SHAPE, F32), pltpu.VMEM((N_DIR, CHUNK, MIX_W), F32),
                        pltpu.VMEM(ST_SHAPE, F32), pltpu.VMEM((N_UNITS, LANE), F32),
                        pltpu.VMEM((N_UNITS, LANE), F32), pltpu.VMEM(ST_SHAPE, F32)])
    return pl.pallas_call(
        _scan_kernel, grid_spec=gs,
        out_shape=(o_sds,) * 6 + (st_sds, st_sds, vec_sds, vec_sds, st_sds),
        compiler_params=_cparams(1),
    )(*tabs, *args)


CONV_ROWS = 256
HALO = 8


def _conv_kernel(x_ref, prev_ref, next_ref, w_ref, o_ref, xe_ref, *, seqs):
    i = pl.program_id(0)
    start = i * CONV_ROWS
    in_lat = start >= seqs.ctx_rows
    off = jnp.where(in_lat, (start - seqs.ctx_rows) % seqs.tl, start % seqs.tc)
    seq_len = jnp.where(in_lat, seqs.tl, seqs.tc)
    xe_ref[0:HALO, :] = jnp.where(off > 0, prev_ref[...], 0.0)
    xe_ref[HALO:HALO + CONV_ROWS, :] = x_ref[...]
    xe_ref[HALO + CONV_ROWS:, :] = jnp.where(off + CONV_ROWS < seq_len, next_ref[...], 0.0)
    pad = CONV_K // 2
    y = jnp.zeros((CONV_ROWS, 3 * MIX_W), F32)
    for j in range(CONV_K):
        y = y + xe_ref[pl.ds(HALO - pad + j, CONV_ROWS), :] * w_ref[j:j + 1, :]
    y = _silu(y)
    outs = []
    for h in range(3 * N_HEADS):
        seg = y[:, _hs(h)]
        if h < 2 * N_HEADS:
            seg = seg * lax.rsqrt(jnp.sum(seg * seg, axis=-1, keepdims=True) + EPS)
            if h < N_HEADS:
                seg = seg * SCALE
        outs.append(seg)
    o_ref[...] = jnp.concatenate(outs, axis=1)


def _gdn_prep(z, conv_w, seqs):
    rows = z.shape[0]
    cw = 3 * MIX_W
    cb = Z_GDQKV // cw
    nb = rows // CONV_ROWS
    per = CONV_ROWS // HALO
    last8 = rows // HALO - 1
    return pl.pallas_call(
        functools.partial(_conv_kernel, seqs=seqs),
        grid=(nb,),
        in_specs=[pl.BlockSpec((CONV_ROWS, cw), lambda i: (i, cb)),
                  pl.BlockSpec((HALO, cw), lambda i: (jnp.maximum(i * per - 1, 0), cb)),
                  pl.BlockSpec((HALO, cw), lambda i: (jnp.minimum((i + 1) * per, last8), cb)),
                  pl.BlockSpec((8, cw), lambda i: (0, 0))],
        out_specs=pl.BlockSpec((CONV_ROWS, cw), lambda i: (i, 0)),
        out_shape=jax.ShapeDtypeStruct((rows, cw), F32),
        scratch_shapes=[pltpu.VMEM((CONV_ROWS + 2 * HALO, cw), F32)],
        compiler_params=_cparams(1),
    )(z, z, z, conv_w)


ATT_ROWS = 256


def _rope(y, cos, sin_signed):
    n = y.shape[1]
    lane = lax.broadcasted_iota(jnp.int32, y.shape, 1)
    partner = jnp.where(lane % 2 == 0, pltpu.roll(y, n - 1, axis=1), pltpu.roll(y, 1, axis=1))
    reps = n // HEAD_DIM
    c = jnp.concatenate([cos] * reps, axis=1)
    s = jnp.concatenate([sin_signed] * reps, axis=1)
    return y * c + partner * s


def _attn_prep_kernel(q_ref, k_ref, cos_ref, sin_ref, qw_ref, kw_ref, qo_ref, ko_ref):
    cos = cos_ref[...]
    sin = sin_ref[...]
    qo_ref[...] = _rope(_head_rms(q_ref[...], qw_ref[...]), cos, sin)
    ko_ref[...] = _rope(_head_rms(k_ref[...], kw_ref[...]), cos, sin)


def _attn_prep(z, cos_t, sin_t, qw, kw):
    rows = z.shape[0]
    return pl.pallas_call(
        _attn_prep_kernel,
        grid=(rows // ATT_ROWS,),
        in_specs=[pl.BlockSpec((ATT_ROWS, MIX_W), lambda i: (i, Z_ATQ // MIX_W)),
                  pl.BlockSpec((ATT_ROWS, KV_W), lambda i: (i, Z_ATK // KV_W)),
                  pl.BlockSpec((ATT_ROWS, HEAD_DIM), lambda i: (i, 0)),
                  pl.BlockSpec((ATT_ROWS, HEAD_DIM), lambda i: (i, 0)),
                  pl.BlockSpec((1, MIX_W), lambda i: (0, 0)),
                  pl.BlockSpec((1, KV_W), lambda i: (0, 0))],
        out_specs=(pl.BlockSpec((ATT_ROWS, MIX_W), lambda i: (i, 0)),
                   pl.BlockSpec((ATT_ROWS, KV_W), lambda i: (i, 0))),
        out_shape=(jax.ShapeDtypeStruct((rows, MIX_W), F32), jax.ShapeDtypeStruct((rows, KV_W), F32)),
        compiler_params=_cparams(1),
    )(z, z, cos_t, sin_t, qw, kw)


def _attn_kernel(*refs, has_cache):
    if has_cache:
        q_ref, k_ref, v_ref, ck_ref, cv_ref, o_ref = refs
    else:
        q_ref, k_ref, v_ref, o_ref = refs
    k = k_ref[...]
    v = v_ref[...]
    for g in range(N_HEADS // N_KV):
        qh = q_ref[:, _hs(g)]
        s = _dot_nt(qh, k) * SCALE
        m = jnp.max(s, axis=-1, keepdims=True)
        if has_cache:
            sc = _dot_nt(qh, ck_ref[...]) * SCALE
            m = jnp.maximum(m, jnp.max(sc, axis=-1, keepdims=True))
            pc = jnp.exp(sc - m)
        p = jnp.exp(s - m)
        den = jnp.sum(p, axis=-1, keepdims=True)
        if has_cache:
            den = den + jnp.sum(pc, axis=-1, keepdims=True)
        inv = 1.0 / den
        o = _dot(p * inv, v)
        if has_cache:
            o = o + _dot(pc * inv, cv_ref[...])
        o_ref[:, _hs(g)] = o


def _attend(qn, kn, z, row0, nseq, t, y_rows, cache=None, l=0):
    gw = (N_HEADS // N_KV) * HEAD_DIM
    nqb = t // Q_BLOCK
    rb0 = row0 // Q_BLOCK
    sb0 = row0 // t
    vcol = Z_ATV // HEAD_DIM
    in_specs = [pl.BlockSpec((Q_BLOCK, gw), lambda b, kv, qi: (rb0 + b * nqb + qi, kv)),
                pl.BlockSpec((t, HEAD_DIM), lambda b, kv, qi: (sb0 + b, kv)),
                pl.BlockSpec((t, HEAD_DIM), lambda b, kv, qi: (sb0 + b, vcol + kv))]
    args = [qn, kn, z]
    if cache is not None:
        ck, cv = cache
        past = ck.shape[2]
        cspec = pl.BlockSpec((None, None, past, HEAD_DIM), lambda b, kv, qi: (b, l, 0, kv))
        in_specs += [cspec, cspec]
        args += [ck, cv]
    return pl.pallas_call(
        functools.partial(_attn_kernel, has_cache=cache is not None),
        grid=(nseq, N_KV, nqb),
        in_specs=in_specs,
        out_specs=pl.BlockSpec((Q_BLOCK, gw), lambda b, kv, qi: (b * nqb + qi, kv)),
        out_shape=jax.ShapeDtypeStruct((y_rows, MIX_W), F32),
        compiler_params=_cparams(3),
    )(*args)


POST_ROWS = 512


def _post_kernel(gf, gb, mf, mb, df, db, zg, zo, zd, at, nw_ref, o_ref):
    nw = nw_ref[...]
    o_ref[0] = (_head_rms(gf[...] + gb[...], nw[0:1, :]) * _silu(zg[...])).astype(BF16)
    o_ref[1] = (_head_rms(mf[...] + mb[...], nw[1:2, :]) * _sigmoid(zo[...])).astype(BF16)
    o_ref[2] = (_head_rms(df[...] + db[...], nw[2:3, :]) * _silu(zd[...])).astype(BF16)
    o_ref[3] = at[...].astype(BF16)


def _branch_post(outs, z, y_at, norm_w):
    rows = z.shape[0]
    tm = min(POST_ROWS, rows)
    row = lambda c: pl.BlockSpec((tm, MIX_W), lambda i: (i, c))
    in_specs = [row(0)] * 6 + [row(Z_GLA // MIX_W + 3), row(Z_ML // MIX_W + 3), row(Z_GDG // MIX_W),
                               row(0), pl.BlockSpec((8, MIX_W), lambda i: (0, 0))]
    return pl.pallas_call(
        _post_kernel,
        grid=(rows // tm,),
        in_specs=in_specs,
        out_specs=pl.BlockSpec((N_BRANCH, tm, MIX_W), lambda i: (0, i, 0)),
        out_shape=jax.ShapeDtypeStruct((N_BRANCH, rows, MIX_W), BF16),
        compiler_params=_cparams(1),
    )(*outs, z, z, z, y_at, norm_w)


def _merge_kernel(x_ref, g_ref, y_ref, zm_ref, wb_ref, wo_ref, o_ref, acc_ref, *, seqs, tm):
    n = pl.program_id(1)
    r = seqs.mod_row(pl.program_id(0) * tm)
    p = _sigmoid(zm_ref[...]) * jnp.dot(y_ref[...], wb_ref[...], preferred_element_type=F32)

    @pl.when(n == 0)
    def _():
        acc_ref[...] = p

    @pl.when(n > 0)
    def _():
        acc_ref[...] += p

    @pl.when(n == N_BRANCH - 1)
    def _():
        out = jnp.dot(acc_ref[...].astype(BF16), wo_ref[...], preferred_element_type=F32)
        o_ref[...] = x_ref[...] + g_ref[pl.ds(r, 1), :] * out


def _merge(x, mod, ybr, z, wb, wo, seqs, l, tm):
    rows = x.shape[0]
    return pl.pallas_call(
        functools.partial(_merge_kernel, seqs=seqs, tm=tm),
        grid=(rows // tm, N_BRANCH),
        in_specs=[pl.BlockSpec((tm, D_MODEL), lambda m, n: (m, 0), pipeline_mode=pl.Buffered(1)),
                  pl.BlockSpec((None, 8, D_MODEL), lambda m, n: (l, 0, 5)),
                  pl.BlockSpec((None, tm, MIX_W), lambda m, n: (n, m, 0)),
                  pl.BlockSpec((tm, D_MODEL), lambda m, n: (m, n)),
                  pl.BlockSpec((None, None, MIX_W, D_MODEL), lambda m, n: (l, n, 0, 0)),
                  pl.BlockSpec((None, D_MODEL, D_MODEL), lambda m, n: (l, 0, 0), pipeline_mode=pl.Buffered(1))],
        out_specs=pl.BlockSpec((tm, D_MODEL), lambda m, n: (m, 0)),
        out_shape=jax.ShapeDtypeStruct((rows, D_MODEL), F32),
        scratch_shapes=[pltpu.VMEM((tm, D_MODEL), F32)],
        compiler_params=_cparams(2),
    )(x, mod, ybr, z, wb, wo)


def _small_w_in(w_in):
    parts = [w_in[..., W_LR:W_ML], w_in[..., W_IF:W_GD], w_in[..., W_AB:W_AT]]
    used = sum(p.shape[-1] for p in parts)
    parts.append(jnp.zeros(w_in.shape[:2] + (LANE - used,), w_in.dtype))
    return jnp.concatenate(parts, axis=-1).astype(BF16)


def _lane_row(vals, lane0):
    v = vals.reshape(-1).astype(F32)
    return jnp.zeros((1, LANE), F32).at[0, lane0:lane0 + v.shape[0]].set(v)


def _rope_tables(seqs):
    t = seqs.tl
    row = (np.arange(t) // GRID_W).astype(np.float32)
    col = (np.arange(t) % GRID_W).astype(np.float32)
    n_pairs = HEAD_DIM // 4
    inv = jnp.asarray(ROPE_THETA, F32) ** (-jnp.arange(n_pairs, dtype=F32) / n_pairs)
    ang = jnp.concatenate([jnp.asarray(row)[:, None] * inv, jnp.asarray(col)[:, None] * inv], axis=-1)
    cos = jnp.repeat(jnp.cos(ang), 2, axis=-1)
    sin = jnp.repeat(jnp.sin(ang), 2, axis=-1) * jnp.asarray(np.tile([-1.0, 1.0], HEAD_DIM // 2), F32)
    cos = jnp.concatenate([jnp.ones((seqs.ctx_rows, HEAD_DIM), F32)] + [cos] * seqs.nl, axis=0)
    sin = jnp.concatenate([jnp.zeros((seqs.ctx_rows, HEAD_DIM), F32)] + [sin] * seqs.nl, axis=0)
    return cos, sin


def _trunk(seqs, x, cvec, cache_k, cache_v, state_gla, state_c, state_n, state_m, state_gdn,
           w_ada, b_ada, w_ffn_in, w_ffn_out, w_in, gla_w2, gla_b2, gla_norm_w, ml_gate_b, ml_norm_w,
           gd_conv_w, gd_a_log, gd_dt_bias, gd_norm_w, q_norm_w, k_norm_w, w_branch, w_out, final_norm_w,
           tm_ffn=1024, tf=256, tm_in=2048, tm_pre=512, tm_mg=512):
    depth = w_in.shape[0]
    rows = seqs.rows
    tm_ffn, tm_in, tm_mg = min(tm_ffn, seqs.tl), min(tm_in, seqs.tl), min(tm_mg, seqs.tl)
    mod = _ada(cvec, w_ada, b_ada)
    w_small = _small_w_in(w_in)
    wb = w_branch.astype(BF16)
    wo = w_out.astype(BF16)
    cos_t, sin_t = _rope_tables(seqs)
    fw = final_norm_w.reshape(1, D_MODEL)
    past = cache_k.shape[2]
    ck = cache_k.reshape(cache_k.shape[:2] + (past, KV_W))
    cv = cache_v.reshape(cache_v.shape[:2] + (past, KV_W))
    zpad = jnp.zeros((LANE - GLA_RANK, MIX_W), F32)
    ctx = []
    for l in range(depth):
        x = _ffn(x, mod, w_ffn_in, w_ffn_out, fw, seqs, l, 0, False, tm_ffn, tf)
        h, zs = _premix(x, mod, w_small[l], seqs, l, tm_pre)
        z = _inproj(h, w_in, l, tm_in)

        w2f = jnp.concatenate([gla_w2[l, 0], zpad], axis=0).astype(BF16)
        w2b = jnp.concatenate([zpad[:GLA_RANK], gla_w2[l, 1], zpad[:LANE - 2 * GLA_RANK]], axis=0).astype(BF16)
        zq = _gdn_prep(z, jnp.concatenate([gd_conv_w[l], jnp.zeros((8 - CONV_K, 3 * MIX_W), F32)], axis=0), seqs)
        ab_lanes = jnp.concatenate([gd_a_log[l], jnp.zeros((N_DIR, N_HEADS), F32)], axis=1)
        dt_lanes = jnp.concatenate([gd_dt_bias[l], jnp.zeros((N_DIR, N_HEADS), F32)], axis=1)
        gf, gb, mf, mb, df, db, st_gla, st_c, st_n, st_m, st_gd = _scans(
            z, zq, zs, seqs, l, w2f, w2b, gla_b2[l, 0:1], gla_b2[l, 1:2], _lane_row(ml_gate_b[l], L_IF),
            _lane_row(ab_lanes, L_AB), _lane_row(dt_lanes, L_AB),
            state_gla, state_c, state_n, state_m, state_gdn)

        qn, kn = _attn_prep(z, cos_t, sin_t, jnp.tile(q_norm_w[l], N_HEADS)[None, :],
                            jnp.tile(k_norm_w[l], N_KV)[None, :])
        y_ctx = _attend(qn, kn, z, 0, seqs.nc, seqs.tc, seqs.ctx_rows)
        y_lat = _attend(qn, kn, z, seqs.ctx_rows, seqs.nl, seqs.tl, seqs.nl * seqs.tl, cache=(ck, cv), l=l)
        y_at = jnp.concatenate([y_ctx, y_lat], axis=0)

        norm_w = jnp.stack([jnp.tile(w[l], N_HEADS) for w in (gla_norm_w, ml_norm_w, gd_norm_w)]
                           + [jnp.zeros((MIX_W,), F32)] * 5, axis=0)
        ybr = _branch_post((gf, gb, mf, mb, df, db), z, y_at, norm_w)
        x = _merge(x, mod, ybr, z, wb, wo, seqs, l, tm_mg)
        x = _ffn(x, mod, w_ffn_in, w_ffn_out, fw, seqs, l, 1, l == depth - 1, tm_ffn, tf)

        nc, tc = seqs.nc, seqs.tc
        ctx.append(dict(
            k=kn[:seqs.ctx_rows].reshape(nc, tc, N_KV, HEAD_DIM),
            v=z[:seqs.ctx_rows, Z_ATV:Z_ATV + KV_W].reshape(nc, tc, N_KV, HEAD_DIM),
            gla=st_gla, mc=st_c,
            mn=st_n.reshape(nc, N_DIR, N_HEADS, HEAD_DIM),
            mm=st_m[:, :, 0].reshape(nc, N_DIR, N_HEADS),
            gd=st_gd))
    return x, ctx


def kernel(x_prompt, x_sample, cache_k, cache_v, state_gla, state_mlstm_c, state_mlstm_n, state_mlstm_m,
           state_gdn, c, c_ctx, w_ada, b_ada, w_ffn_in, w_ffn_out, w_in, gla_w2, gla_b2, gla_norm_w,
           ml_gate_b, ml_norm_w, gd_conv_w, gd_a_log, gd_dt_bias, gd_norm_w, q_norm_w, k_norm_w,
           w_branch, w_out, final_norm_w):
    nc, tc, _ = x_prompt.shape
    nl, tl, _ = x_sample.shape
    seqs = _Seqs(nc, tc, nl, tl)
    x = jnp.concatenate([x_prompt.reshape(nc * tc, D_MODEL), x_sample.reshape(nl * tl, D_MODEL)], axis=0)
    cvec = jnp.concatenate([c_ctx[None, :], c, jnp.zeros((8 - 1 - nl, D_MODEL), F32)], axis=0)
    y, ctx = _trunk(seqs, x, cvec, cache_k, cache_v, state_gla, state_mlstm_c, state_mlstm_n, state_mlstm_m,
                    state_gdn, w_ada, b_ada, w_ffn_in, w_ffn_out, w_in, gla_w2, gla_b2, gla_norm_w,
                    ml_gate_b, ml_norm_w, gd_conv_w, gd_a_log, gd_dt_bias, gd_norm_w, q_norm_w, k_norm_w,
                    w_branch, w_out, final_norm_w)
    y_prompt = y[:nc * tc].reshape(nc, tc, D_MODEL)
    y_sample = y[nc * tc:].reshape(nl, tl, D_MODEL)
    stack = lambda name: jnp.stack([cx[name] for cx in ctx], axis=1)
    return (y_prompt, y_sample, stack('k'), stack('v'), stack('gla'), stack('mc'), stack('mn'),
            stack('mm'), stack('gd'))
```

```python
import functools

import numpy as np
import jax
import jax.numpy as jnp
from jax import lax
from jax.experimental import pallas as pl
from jax.experimental.pallas import tpu as pltpu

F32 = jnp.float32
BF16 = jnp.bfloat16

D_MODEL = 2048
DEPTH = 4
GRID_W = 64
N_HEADS = 4
HEAD_DIM = 128
MIX_W = N_HEADS * HEAD_DIM
N_KV = 2
KV_W = N_KV * HEAD_DIM
GLA_RANK = 16
GLA_TAU = 16.0
CHUNK = 64
SUB = 16
Q_BLOCK = 128
CONV_K = 5
D_FF = 5632
ROPE_THETA = 10000.0
N_BRANCH = 4
N_DIR = 2
N_MOD = 9
EPS = 1e-6
SCALE = HEAD_DIM ** -0.5

Z_MERGE = 0
Z_GLA = N_BRANCH * D_MODEL
Z_ML = Z_GLA + 4 * MIX_W
Z_GDQKV = Z_ML + 4 * MIX_W
Z_GDG = Z_GDQKV + 3 * MIX_W
Z_ATQ = Z_GDG + MIX_W
Z_ATK = Z_ATQ + MIX_W
Z_ATV = Z_ATK + KV_W
NZ = Z_ATV + KV_W
LANE = 128
SUBLANES = 8
L_LR = 0
L_IF = N_DIR * GLA_RANK
L_AB = L_IF + N_DIR * 2 * N_HEADS

W_LR = 4 * MIX_W
W_ML = W_LR + N_DIR * GLA_RANK
W_IF = W_ML + 4 * MIX_W
W_GD = W_IF + N_DIR * 2 * N_HEADS
W_AB = W_GD + 4 * MIX_W
W_AT = W_AB + N_DIR * 2 * N_HEADS
W_MG = W_AT + MIX_W + 2 * KV_W
IN_TILE = 512
TILE_SRC = ([W_MG + IN_TILE * j for j in range(N_BRANCH * D_MODEL // IN_TILE)]
            + [IN_TILE * j for j in range(4)] + [W_ML + IN_TILE * j for j in range(4)]
            + [W_GD + IN_TILE * j for j in range(4)] + [W_AT + IN_TILE * j for j in range(2)])

VMEM_LIMIT = 56 * 1024 * 1024


def _cparams(n_axes):
    return pltpu.CompilerParams(dimension_semantics=("arbitrary",) * n_axes,
                                vmem_limit_bytes=VMEM_LIMIT)


def _dot(a, b):
    return jnp.dot(a.astype(BF16), b.astype(BF16), preferred_element_type=F32)


def _dot_nt(a, b):
    return lax.dot_general(a.astype(BF16), b.astype(BF16), (((1,), (1,)), ((), ())),
                           preferred_element_type=F32)


def _dot_tn(a, b):
    return lax.dot_general(a.astype(BF16), b.astype(BF16), (((0,), (0,)), ((), ())),
                           preferred_element_type=F32)


def _split(a):
    hi = a.astype(BF16)
    return hi, (a - hi.astype(F32)).astype(BF16)


def _dot_split(a, b):
    ah, al = a
    bh, bl = b
    return (jnp.dot(ah, bh, preferred_element_type=F32) + jnp.dot(ah, bl, preferred_element_type=F32)
            + jnp.dot(al, bh, preferred_element_type=F32))


def _cumsum_mask(mask, x):
    m = mask.astype(BF16)
    x0 = x.astype(BF16)
    r1 = x - x0.astype(F32)
    x1 = r1.astype(BF16)
    x2 = (r1 - x1.astype(F32)).astype(BF16)
    return (jnp.dot(m, x0, preferred_element_type=F32) + jnp.dot(m, x1, preferred_element_type=F32)
            + jnp.dot(m, x2, preferred_element_type=F32))


def _sigmoid(x):
    return 1.0 / (1.0 + jnp.exp(-x))


def _silu(x):
    return x * _sigmoid(x)


def _softplus(x):
    return jnp.maximum(x, 0.0) + jnp.log1p(jnp.exp(-jnp.abs(x)))


def _log_sigmoid(x):
    return -_softplus(-x)


def _rmsmod(x, sc, sh):
    ms = jnp.mean(x * x, axis=-1, keepdims=True)
    return x * lax.rsqrt(ms + EPS) * (1.0 + sc) + sh


def _head_rms(x, w):
    outs = []
    for h in range(x.shape[1] // HEAD_DIM):
        seg = x[:, h * HEAD_DIM:(h + 1) * HEAD_DIM]
        ms = jnp.mean(seg * seg, axis=-1, keepdims=True)
        outs.append(seg * lax.rsqrt(ms + EPS))
    return jnp.concatenate(outs, axis=1) * w


class _Seqs:
    def __init__(self, nc, tc, nl, tl):
        self.nc, self.tc, self.nl, self.tl = nc, tc, nl, tl
        self.ctx_rows = nc * tc
        self.rows = nc * tc + nl * tl

    def mod_row(self, start):
        return jnp.where(start < self.ctx_rows, 0, 1 + (start - self.ctx_rows) // self.tl)

    def scan_tables(self):
        fwd, bwd, flg, sq = [], [], [], []
        base = 0
        for s in range(self.nc + self.nl):
            lat = s >= self.nc
            n = (self.tl if lat else self.tc) // CHUNK
            for j in range(n):
                fwd.append(base + j)
                bwd.append(base + n - 1 - j)
                flg.append((1 if j == 0 else 0) | (2 if j == n - 1 else 0) | (4 if lat else 0))
                sq.append(s)
            base += n
        return tuple(jnp.asarray(np.array(a, np.int32)) for a in (fwd, bwd, flg, sq))


def _ada_kernel(c_ref, w_ref, b_ref, o_ref):
    o_ref[...] = _dot(_silu(c_ref[...]), w_ref[...]) + b_ref[...]


def _ada(cvec, w_ada, b_ada):
    tn = 1024
    nmod = N_MOD * D_MODEL
    depth = w_ada.shape[0]
    return pl.pallas_call(
        _ada_kernel,
        grid=(depth, nmod // tn),
        in_specs=[pl.BlockSpec((8, D_MODEL), lambda l, j: (0, 0)),
                  pl.BlockSpec((None, D_MODEL, tn), lambda l, j: (l, 0, j)),
                  pl.BlockSpec((None, 1, tn), lambda l, j: (l, 0, j))],
        out_specs=pl.BlockSpec((None, 8, tn), lambda l, j: (l, 0, j)),
        out_shape=jax.ShapeDtypeStruct((depth, 8, nmod), F32),
        compiler_params=_cparams(2),
    )(cvec, w_ada, b_ada.reshape(depth, 1, nmod))


def _ffn_kernel(x_ref, sh_ref, sc_ref, g_ref, wg_ref, wu_ref, wo_ref, fw_ref, o_ref, h_ref,
                *, seqs, tm, nf, final):
    f = pl.program_id(1)
    r = seqs.mod_row(pl.program_id(0) * tm)

    @pl.when(f == 0)
    def _():
        h_ref[...] = _rmsmod(x_ref[...], sc_ref[pl.ds(r, 1), :], sh_ref[pl.ds(r, 1), :]).astype(BF16)
        o_ref[...] = jnp.zeros_like(o_ref)

    h = h_ref[...]
    g = jnp.dot(h, wg_ref[...].astype(BF16), preferred_element_type=F32)
    u = jnp.dot(h, wu_ref[...].astype(BF16), preferred_element_type=F32)
    a = (_silu(g) * u).astype(BF16)
    o_ref[...] += jnp.dot(a, wo_ref[...].astype(BF16), preferred_element_type=F32)

    @pl.when(f == nf - 1)
    def _():
        y = x_ref[...] + 0.5 * g_ref[pl.ds(r, 1), :] * o_ref[...]
        if final:
            ms = jnp.mean(y * y, axis=-1, keepdims=True)
            y = y * lax.rsqrt(ms + EPS) * fw_ref[...]
        o_ref[...] = y


def _ffn(x, mod, w_in, w_out, fw, seqs, l, i, final, tm, tf):
    rows = x.shape[0]
    nf = D_FF // tf
    j0 = 0 if i == 0 else 6
    kern = functools.partial(_ffn_kernel, seqs=seqs, tm=tm, nf=nf, final=final)
    modspec = lambda j: pl.BlockSpec((None, 8, D_MODEL), lambda m, f: (l, 0, j))
    return pl.pallas_call(
        kern,
        grid=(rows // tm, nf),
        in_specs=[pl.BlockSpec((tm, D_MODEL), lambda m, f: (m, 0), pipeline_mode=pl.Buffered(1)),
                  modspec(j0), modspec(j0 + 1), modspec(j0 + 2),
                  pl.BlockSpec((None, None, D_MODEL, tf), lambda m, f: (l, i, 0, f)),
                  pl.BlockSpec((None, None, D_MODEL, tf), lambda m, f: (l, i, 0, nf + f)),
                  pl.BlockSpec((None, None, tf, D_MODEL), lambda m, f: (l, i, f, 0)),
                  pl.BlockSpec((1, D_MODEL), lambda m, f: (0, 0))],
        out_specs=pl.BlockSpec((tm, D_MODEL), lambda m, f: (m, 0)),
        out_shape=jax.ShapeDtypeStruct((rows, D_MODEL), F32),
        scratch_shapes=[pltpu.VMEM((tm, D_MODEL), BF16)],
        compiler_params=_cparams(2),
    )(x, mod, mod, mod, w_in, w_in, w_out, fw)


def _premix_kernel(x_ref, sh_ref, sc_ref, wlr_ref, wif_ref, wab_ref, h_ref, zs_ref, *, seqs, tm):
    r = seqs.mod_row(pl.program_id(0) * tm)
    h = _rmsmod(x_ref[...], sc_ref[pl.ds(r, 1), :], sh_ref[pl.ds(r, 1), :]).astype(BF16)
    h_ref[...] = h
    used = wlr_ref.shape[1] + wif_ref.shape[1] + wab_ref.shape[1]
    ws = jnp.concatenate([wlr_ref[0], wif_ref[0], wab_ref[0], jnp.zeros((LANE - used, D_MODEL), F32)],
                         axis=0).astype(BF16)
    zs_ref[...] = _dot_nt(h, ws)


def _premix(x, mod, w_t, seqs, l, tm):
    rows = x.shape[0]
    modspec = lambda j: pl.BlockSpec((None, 8, D_MODEL), lambda m: (l, 0, j))
    wspec = lambda start, n: pl.BlockSpec((pl.Element(1), pl.Element(n), pl.Element(D_MODEL)),
                                          lambda m: (l, start, 0))
    return pl.pallas_call(
        functools.partial(_premix_kernel, seqs=seqs, tm=tm),
        grid=(rows // tm,),
        in_specs=[pl.BlockSpec((tm, D_MODEL), lambda m: (m, 0)), modspec(3), modspec(4),
                  wspec(W_LR, W_ML - W_LR), wspec(W_IF, W_GD - W_IF), wspec(W_AB, W_AT - W_AB)],
        out_specs=(pl.BlockSpec((tm, D_MODEL), lambda m: (m, 0)), pl.BlockSpec((tm, LANE), lambda m: (m, 0))),
        out_shape=(jax.ShapeDtypeStruct((rows, D_MODEL), BF16), jax.ShapeDtypeStruct((rows, LANE), F32)),
        compiler_params=_cparams(1),
    )(x, mod, mod, w_t, w_t, w_t)


def _inproj_kernel(row_ref, h_ref, w_ref, o_ref, w_scr):
    @pl.when(pl.program_id(1) == 0)
    def _():
        w_scr[...] = w_ref[0].astype(BF16)

    o_ref[...] = lax.dot_general(h_ref[...], w_scr[...], (((1,), (1,)), ((), ())),
                                 preferred_element_type=F32)


def _inproj(h, w_t, l, tm):
    rows = h.shape[0]
    assert all(c % SUBLANES == 0 for c in TILE_SRC)
    src = jnp.asarray(np.array(TILE_SRC, np.int32) // SUBLANES)
    gs = pltpu.PrefetchScalarGridSpec(
        num_scalar_prefetch=1, grid=(len(TILE_SRC), rows // tm),
        in_specs=[pl.BlockSpec((tm, D_MODEL), lambda n, m, c: (m, 0)),
                  pl.BlockSpec((pl.Element(1), pl.Element(IN_TILE), pl.Element(D_MODEL)),
                               lambda n, m, c: (l, c[n] * SUBLANES, 0))],
        out_specs=pl.BlockSpec((tm, IN_TILE), lambda n, m, c: (m, n)),
        scratch_shapes=[pltpu.VMEM((IN_TILE, D_MODEL), BF16)])
    return pl.pallas_call(
        _inproj_kernel, grid_spec=gs,
        out_shape=jax.ShapeDtypeStruct((rows, NZ), F32),
        compiler_params=_cparams(2),
    )(src, h, w_t)


def _tri_masks(rev):
    ri = lax.broadcasted_iota(jnp.int32, (CHUNK, CHUNK), 0)
    ci = lax.broadcasted_iota(jnp.int32, (CHUNK, CHUNK), 1)
    if rev:
        return ri <= ci, ri < ci
    return ri >= ci, ri > ci


def _flags(flg):
    first = (flg & 1) != 0
    last = (flg & 2) != 0
    lat = (flg & 4) != 0
    return first, last, lat


def _hs(h):
    return slice(h * HEAD_DIM, (h + 1) * HEAD_DIM)


def _gla_init(lat, init_ref, st_scr):
    if lat:
        for d in range(N_DIR):
            for h in range(N_HEADS):
                st_scr[d, h] = init_ref[d, h].T
    else:
        st_scr[...] = jnp.zeros_like(st_scr)


def _gla_final(st_out, st_scr):
    for d in range(N_DIR):
        for h in range(N_HEADS):
            st_out[d, h] = st_scr[d, h].T


def _gla_step(qf, kf, vf, sf, qb, kb, vb, sb, w2f, w2b, b2f, b2b, of_ref, ob_ref, st_scr, b_scr):
    dirs = ((qf, kf, vf, sf, w2f, b2f, of_ref), (qb, kb, vb, sb, w2b, b2b, ob_ref))
    for d, (q_ref, k_ref, v_ref, s_ref, w2, b2, o_ref) in enumerate(dirs):
        rev = d == 1
        incl, _ = _tri_masks(rev)
        pre = jnp.dot(s_ref[...].astype(BF16), w2[...], preferred_element_type=F32) + b2[...]
        lg = _log_sigmoid(pre) / GLA_TAU
        b = _cumsum_mask(incl, lg)
        b_scr[d] = b
        bend = jnp.sum(lg, axis=0, keepdims=True)
        q = q_ref[...] * SCALE
        k = k_ref[...]
        v = v_ref[...]
        qd = q * jnp.exp(b)
        kd = k * jnp.exp(bend - b)
        eb = jnp.exp(bend)

        ri = lax.broadcasted_iota(jnp.int32, (CHUNK, CHUNK), 0)
        ci = lax.broadcasted_iota(jnp.int32, (CHUNK, CHUNK), 1)
        if rev:
            off_mask = ci >= (ri // SUB + 1) * SUB
        else:
            off_mask = ci < (ri // SUB) * SUB
        nsub = CHUNK // SUB
        att_rows = [[] for _ in range(N_HEADS)]
        for blk in range(nsub):
            r0 = blk * SUB
            has_src = (blk < nsub - 1) if rev else (blk > 0)
            if not has_src:
                for h in range(N_HEADS):
                    att_rows[h].append(jnp.zeros((SUB, CHUNK), F32))
                continue
            edge = r0 + SUB if rev else r0 - 1
            bref = b[edge:edge + 1, :]
            qe = q[r0:r0 + SUB, :] * jnp.exp(b[r0:r0 + SUB, :] - bref)
            ke = k * jnp.exp(jnp.minimum(bref - b, 0.0))
            for h in range(N_HEADS):
                att_rows[h].append(_dot_nt(qe[:, _hs(h)], ke[:, _hs(h)]))
        o_main = []
        for h in range(N_HEADS):
            att = jnp.where(off_mask, jnp.concatenate(att_rows[h], axis=0), 0.0)
            o_main.append(_dot_nt(qd[:, _hs(h)], st_scr[d, h]) + _dot(att, v[:, _hs(h)]))
        o_main = jnp.concatenate(o_main, axis=1)
        yield

        rowi = lax.broadcasted_iota(jnp.int32, (SUBLANES, 1), 0)
        for r0 in range(0, CHUNK, SUBLANES):
            blk0 = (r0 // SUB) * SUB
            bb = b[r0:r0 + SUBLANES, :]
            qq = q[r0:r0 + SUBLANES, :]
            acc = None
            for s in range(blk0, blk0 + SUB):
                if (s > r0 + SUBLANES - 1 and not rev) or (s < r0 and rev):
                    continue
                bs = b_scr[d, s:s + 1, :]
                ks = k_ref[s:s + 1, :]
                vs = v_ref[s:s + 1, :]
                p = qq * ks * jnp.exp(bb - bs)
                valid = (rowi <= s - r0) if rev else (rowi >= s - r0)
                parts = []
                for h in range(N_HEADS):
                    c = jnp.sum(p[:, _hs(h)], axis=-1, keepdims=True)
                    parts.append(jnp.where(valid, c, 0.0) * vs[:, _hs(h)])
                term = jnp.concatenate(parts, axis=1)
                acc = term if acc is None else acc + term
            o_ref[r0:r0 + SUBLANES, :] = o_main[r0:r0 + SUBLANES, :] + acc
            yield

        for h in range(N_HEADS):
            st_scr[d, h] = st_scr[d, h] * eb[:, _hs(h)] + _dot_tn(v[:, _hs(h)], kd[:, _hs(h)])


def _copy_or_zero(lat, pairs):
    for src, dst in pairs:
        dst[...] = src[...] if lat else jnp.zeros_like(dst)


def _mlstm_step(qf, kf, vf, sf, qb, kb, vb, sb, bias_ref, of_ref, ob_ref, c_scr, n_scr, m_scr):
    dirs = ((qf, kf, vf, sf, of_ref), (qb, kb, vb, sb, ob_ref))
    for d, (q_ref, k_ref, v_ref, s_ref, o_ref) in enumerate(dirs):
        rev = d == 1
        incl, _ = _tri_masks(rev)
        last_row = 0 if rev else CHUNK - 1
        gates = s_ref[...] + bias_ref[...]
        lf_all = _log_sigmoid(gates)
        f_all = _cumsum_mask(incl, lf_all)
        f_all_t = f_all.T
        gates_t = gates.T
        q = q_ref[...]
        k = k_ref[...] * SCALE
        v = v_ref[...]
        for h in range(N_HEADS):
            ji = L_IF + d * 2 * N_HEADS + h
            jf = ji + N_HEADS
            r = d * N_HEADS + h
            fc = f_all[:, jf:jf + 1]
            frow = f_all_t[jf:jf + 1, :]
            igc = gates[:, ji:ji + 1]
            igrow = gates_t[ji:ji + 1, :]
            m_prev = m_scr[r:r + 1, 0:1]
            n_prev = n_scr[r:r + 1, :]
            c_prev = c_scr[d, h]
            qh, kh, vh = q[:, _hs(h)], k[:, _hs(h)], v[:, _hs(h)]
            log_d = jnp.where(incl, fc - frow + igrow, -jnp.inf)
            inter = fc + m_prev
            m_t = jnp.maximum(inter, jnp.max(log_d, axis=-1, keepdims=True))
            dmat = jnp.exp(log_d - m_t)
            a_in = jnp.exp(inter - m_t)
            s = _dot_nt(qh, kh) * dmat
            num = a_in * _dot(qh, c_prev) + _dot(s, vh)
            den = a_in * jnp.sum(qh * n_prev, axis=-1, keepdims=True) + jnp.sum(s, axis=-1, keepdims=True)
            o_ref[:, _hs(h)] = num / jnp.maximum(jnp.abs(den), jnp.exp(-m_t))
            m_new = m_t[last_row:last_row + 1, :]
            f_end = fc[last_row:last_row + 1, :]
            w_end = jnp.exp(f_end - fc + igc - m_new)
            a_end = jnp.exp(f_end + m_prev - m_new)
            kw = kh * w_end
            c_scr[d, h] = a_end * c_prev + _dot_tn(kw, vh)
            n_scr[r:r + 1, :] = a_end * n_prev + jnp.sum(kw, axis=0, keepdims=True)
            m_scr[r:r + 1, :] = jnp.broadcast_to(m_new, (1, LANE))
            yield


def _unit_tri_inverse_all(ns):
    ri = lax.broadcasted_iota(jnp.int32, (CHUNK, CHUNK), 0)
    ci = lax.broadcasted_iota(jnp.int32, (CHUNK, CHUNK), 1)
    same16 = (ri // SUB) == (ci // SUB)
    same32 = (ri // (2 * SUB)) == (ci // (2 * SUB))
    eye = (ri == ci).astype(F32)
    nd = [jnp.where(same16, n, 0.0) for n in ns]
    n1 = [_split(jnp.where(jnp.logical_and(same32, jnp.logical_not(same16)), n, 0.0)) for n in ns]
    n2 = [_split(jnp.where(same32, 0.0, n)) for n in ns]
    t = [eye - x for x in nd]
    nds = [_split(x) for x in nd]
    p = [_dot_split(x, x) for x in nds]
    yield
    for level in range(3):
        ps = [_split(x) for x in p]
        t = [x + _dot_split(_split(x), y) for x, y in zip(t, ps)]
        if level < 2:
            p = [_dot_split(y, y) for y in ps]
        yield
    for nn in (n1, n2):
        ts = [_split(x) for x in t]
        a = [_dot_split(y, x) for x, y in zip(ts, nn)]
        yield
        t = [x - _dot_split(xs, _split(y)) for x, xs, y in zip(t, ts, a)]
        yield
    return t


def _gdn_step(qf, kf, vf, sf, qb, kb, vb, sb, alog_ref, dtb_ref, of_ref, ob_ref, s_scr):
    units = []
    dirs = ((qf, kf, vf, sf, of_ref), (qb, kb, vb, sb, ob_ref))
    for d, (q_ref, k_ref, v_ref, s_ref, o_ref) in enumerate(dirs):
        rev = d == 1
        incl, strict = _tri_masks(rev)
        last_row = 0 if rev else CHUNK - 1
        small = s_ref[...]
        g_all = -jnp.exp(alog_ref[...]) * _softplus(small + dtb_ref[...])
        beta_all = _sigmoid(small)
        gam_all = _cumsum_mask(incl, g_all)
        gam_t = gam_all.T
        q = q_ref[...]
        k = k_ref[...]
        v = v_ref[...]
        for h in range(N_HEADS):
            jg = L_AB + d * 2 * N_HEADS + h
            jb = jg + N_HEADS
            gc = gam_all[:, jg:jg + 1]
            units.append(dict(
                d=d, h=h, o_ref=o_ref, incl=incl, strict=strict, gc=gc, grow=gam_t[jg:jg + 1, :],
                beta=beta_all[:, jb:jb + 1], g_end=gc[last_row:last_row + 1, :],
                q=q[:, _hs(h)], k=k[:, _hs(h)], v=v[:, _hs(h)]))

    for u in units:
        u['decay'] = jnp.exp(jnp.where(u['incl'], u['gc'] - u['grow'], -jnp.inf))
    kk = [_dot_nt(u['k'], u['k']) for u in units]
    qk = [_dot_nt(u['q'], u['k']) for u in units]
    ns = [jnp.where(u['strict'], u['beta'] * x * u['decay'], 0.0) for u, x in zip(units, kk)]
    yield
    ts = yield from _unit_tri_inverse_all(ns)
    rhs = [_split(jnp.concatenate([u['v'] * u['beta'], u['k'] * (u['beta'] * jnp.exp(u['gc']))], axis=1))
           for u in units]
    sol = [_dot_split(_split(t), r) for t, r in zip(ts, rhs)]
    yield
    s_prev = [s_scr[u['d'], u['h']] for u in units]
    w_new = [x[:, :HEAD_DIM] - _dot(x[:, HEAD_DIM:], sp) for x, sp in zip(sol, s_prev)]
    yield
    for u, x, w, sp in zip(units, qk, w_new, s_prev):
        u['o_ref'][:, _hs(u['h'])] = _dot(u['q'] * jnp.exp(u['gc']), sp) + _dot(x * u['decay'], w)
    for u, w, sp in zip(units, w_new, s_prev):
        s_scr[u['d'], u['h']] = (jnp.exp(u['g_end']) * sp
                                 + _dot_tn(u['k'] * jnp.exp(u['g_end'] - u['gc']), w))


N_UNITS = N_DIR * N_HEADS
_DONE = object()
ST_SHAPE = (N_DIR, N_HEADS, HEAD_DIM, HEAD_DIM)


def _scan_kernel(fwd_ref, bwd_ref, flg_ref, sq_ref,
                 gqf, gkf, gvf, gqb, gkb, gvb, mqf, mkf, mvf, mqb, mkb, mvb,
                 dqf, dkf, dvf, dqb, dkb, dvb, sf, sb,
                 w2f, w2b, b2f, b2b, mbias, alog, dtb,
                 g0, c0, n0, m0, s0,
                 g_of, g_ob, m_of, m_ob, d_of, d_ob, g_out, c_out, n_out, m_out, s_out,
                 g_scr, b_scr, c_scr, n_scr, m_scr, s_scr):
    first, last, lat = _flags(flg_ref[pl.program_id(0)])
    carried = ((c0, c_scr), (n0, n_scr), (m0, m_scr), (s0, s_scr))

    @pl.when(jnp.logical_and(first, jnp.logical_not(lat)))
    def _():
        _gla_init(False, g0, g_scr)
        _copy_or_zero(False, carried)

    @pl.when(jnp.logical_and(first, lat))
    def _():
        _gla_init(True, g0, g_scr)
        _copy_or_zero(True, carried)

    stages = [_gdn_step(dqf, dkf, dvf, sf, dqb, dkb, dvb, sb, alog, dtb, d_of, d_ob, s_scr),
              _mlstm_step(mqf, mkf, mvf, sf, mqb, mkb, mvb, sb, mbias, m_of, m_ob, c_scr, n_scr, m_scr),
              _gla_step(gqf, gkf, gvf, sf, gqb, gkb, gvb, sb, w2f, w2b, b2f, b2b, g_of, g_ob, g_scr, b_scr)]
    while stages:
        for g in list(stages):
            if next(g, _DONE) is _DONE:
                stages.remove(g)

    @pl.when(jnp.logical_and(last, jnp.logical_not(lat)))
    def _():
        _gla_final(g_out, g_scr)
        for src, dst in ((c_scr, c_out), (n_scr, n_out), (m_scr, m_out), (s_scr, s_out)):
            dst[...] = src[...]


def _scans(z, zq, zs, seqs, l, w2f, w2b, b2f, b2b, mbias, alog_row, dtb_row,
           state_gla, state_c, state_n, state_m, state_gdn):
    rows = z.shape[0]
    tabs = seqs.scan_tables()
    nsteps = int(tabs[0].shape[0])

    def blk(which, col, width):
        if which == 0:
            return pl.BlockSpec((CHUNK, width), lambda i, fw, bw, fl, sq: (fw[i], col))
        return pl.BlockSpec((CHUNK, width), lambda i, fw, bw, fl, sq: (bw[i], col))

    def const(shape):
        nd = len(shape)
        return pl.BlockSpec(shape, lambda i, fw, bw, fl, sq: (0,) * nd)

    lat_i = lambda sq, i: jnp.maximum(sq[i] - seqs.nc, 0)
    ctx_i = lambda sq, i: jnp.minimum(sq[i], seqs.nc - 1)
    in_specs, args = [], []
    for src, c0 in ((z, Z_GLA // MIX_W), (z, Z_ML // MIX_W), (zq, 0)):
        for which in (0, 1):
            in_specs += [blk(which, c0, MIX_W), blk(which, c0 + 1, MIX_W), blk(which, c0 + 2, MIX_W)]
            args += [src] * 3
    in_specs += [blk(0, 0, LANE), blk(1, 0, LANE)]
    args += [zs, zs]
    in_specs += [const((LANE, MIX_W)), const((LANE, MIX_W)), const((1, MIX_W)), const((1, MIX_W)),
                 const((1, LANE)), const((1, LANE)), const((1, LANE))]
    args += [w2f, w2b, b2f, b2b, mbias, alog_row, dtb_row]
    st_in = pl.BlockSpec((None, None) + ST_SHAPE, lambda i, fw, bw, fl, sq: (lat_i(sq, i), l, 0, 0, 0, 0))
    vec_in = pl.BlockSpec((None, None, N_UNITS, LANE), lambda i, fw, bw, fl, sq: (lat_i(sq, i), l, 0, 0))
    depth = state_c.shape[1]
    n0 = state_n.reshape(seqs.nl, depth, N_UNITS, HEAD_DIM)
    m0 = jnp.broadcast_to(state_m.reshape(seqs.nl, depth, N_UNITS, 1), (seqs.nl, depth, N_UNITS, LANE))
    in_specs += [st_in, st_in, vec_in, vec_in, st_in]
    args += [state_gla, state_c, n0, m0, state_gdn]

    o_f = pl.BlockSpec((CHUNK, MIX_W), lambda i, fw, bw, fl, sq: (fw[i], 0))
    o_b = pl.BlockSpec((CHUNK, MIX_W), lambda i, fw, bw, fl, sq: (bw[i], 0))
    st_out = pl.BlockSpec((None,) + ST_SHAPE, lambda i, fw, bw, fl, sq: (ctx_i(sq, i), 0, 0, 0, 0))
    vec_out = pl.BlockSpec((None, N_UNITS, LANE), lambda i, fw, bw, fl, sq: (ctx_i(sq, i), 0, 0))
    o_sds = jax.ShapeDtypeStruct((rows, MIX_W), F32)
    st_sds = jax.ShapeDtypeStruct((seqs.nc,) + ST_SHAPE, F32)
    vec_sds = jax.ShapeDtypeStruct((seqs.nc, N_UNITS, LANE), F32)
    gs = pltpu.PrefetchScalarGridSpec(
        num_scalar_prefetch=4, grid=(nsteps,), in_specs=in_specs,
        out_specs=(o_f, o_b, o_f, o_b, o_f, o_b, st_out, st_out, vec_out, vec_out, st_out),
        scratch_shapes=[pltpu.VMEM(ST_SHAPE, F32), pltpu.VMEM((N_DIR, CHUNK, MIX_W), F32),
                        pltpu.VMEM(ST_SHAPE, F32), pltpu.VMEM((N_UNITS, LANE), F32),
                        pltpu.VMEM((N_UNITS, LANE), F32), pltpu.VMEM(ST_SHAPE, F32)])
    return pl.pallas_call(
        _scan_kernel, grid_spec=gs,
        out_shape=(o_sds,) * 6 + (st_sds, st_sds, vec_sds, vec_sds, st_sds),
        compiler_params=_cparams(1),
    )(*tabs, *args)


CONV_ROWS = 256
HALO = 8


def _conv_kernel(x_ref, prev_ref, next_ref, w_ref, o_ref, xe_ref, *, seqs):
    i = pl.program_id(0)
    start = i * CONV_ROWS
    in_lat = start >= seqs.ctx_rows
    off = jnp.where(in_lat, (start - seqs.ctx_rows) % seqs.tl, start % seqs.tc)
    seq_len = jnp.where(in_lat, seqs.tl, seqs.tc)
    xe_ref[0:HALO, :] = jnp.where(off > 0, prev_ref[...], 0.0)
    xe_ref[HALO:HALO + CONV_ROWS, :] = x_ref[...]
    xe_ref[HALO + CONV_ROWS:, :] = jnp.where(off + CONV_ROWS < seq_len, next_ref[...], 0.0)
    pad = CONV_K // 2
    y = jnp.zeros((CONV_ROWS, 3 * MIX_W), F32)
    for j in range(CONV_K):
        y = y + xe_ref[pl.ds(HALO - pad + j, CONV_ROWS), :] * w_ref[j:j + 1, :]
    y = _silu(y)
    outs = []
    for h in range(3 * N_HEADS):
        seg = y[:, _hs(h)]
        if h < 2 * N_HEADS:
            seg = seg * lax.rsqrt(jnp.sum(seg * seg, axis=-1, keepdims=True) + EPS)
            if h < N_HEADS:
                seg = seg * SCALE
        outs.append(seg)
    o_ref[...] = jnp.concatenate(outs, axis=1)


def _gdn_prep(z, conv_w, seqs):
    rows = z.shape[0]
    cw = 3 * MIX_W
    cb = Z_GDQKV // cw
    nb = rows // CONV_ROWS
    per = CONV_ROWS // HALO
    last8 = rows // HALO - 1
    return pl.pallas_call(
        functools.partial(_conv_kernel, seqs=seqs),
        grid=(nb,),
        in_specs=[pl.BlockSpec((CONV_ROWS, cw), lambda i: (i, cb)),
                  pl.BlockSpec((HALO, cw), lambda i: (jnp.maximum(i * per - 1, 0), cb)),
                  pl.BlockSpec((HALO, cw), lambda i: (jnp.minimum((i + 1) * per, last8), cb)),
                  pl.BlockSpec((8, cw), lambda i: (0, 0))],
        out_specs=pl.BlockSpec((CONV_ROWS, cw), lambda i: (i, 0)),
        out_shape=jax.ShapeDtypeStruct((rows, cw), F32),
        scratch_shapes=[pltpu.VMEM((CONV_ROWS + 2 * HALO, cw), F32)],
        compiler_params=_cparams(1),
    )(z, z, z, conv_w)


ATT_ROWS = 256


def _rope(y, cos, sin_signed):
    n = y.shape[1]
    lane = lax.broadcasted_iota(jnp.int32, y.shape, 1)
    partner = jnp.where(lane % 2 == 0, pltpu.roll(y, n - 1, axis=1), pltpu.roll(y, 1, axis=1))
    reps = n // HEAD_DIM
    c = jnp.concatenate([cos] * reps, axis=1)
    s = jnp.concatenate([sin_signed] * reps, axis=1)
    return y * c + partner * s


def _attn_prep_kernel(q_ref, k_ref, cos_ref, sin_ref, qw_ref, kw_ref, qo_ref, ko_ref):
    cos = cos_ref[...]
    sin = sin_ref[...]
    qo_ref[...] = _rope(_head_rms(q_ref[...], qw_ref[...]), cos, sin)
    ko_ref[...] = _rope(_head_rms(k_ref[...], kw_ref[...]), cos, sin)


def _attn_prep(z, cos_t, sin_t, qw, kw):
    rows = z.shape[0]
    return pl.pallas_call(
        _attn_prep_kernel,
        grid=(rows // ATT_ROWS,),
        in_specs=[pl.BlockSpec((ATT_ROWS, MIX_W), lambda i: (i, Z_ATQ // MIX_W)),
                  pl.BlockSpec((ATT_ROWS, KV_W), lambda i: (i, Z_ATK // KV_W)),
                  pl.BlockSpec((ATT_ROWS, HEAD_DIM), lambda i: (i, 0)),
                  pl.BlockSpec((ATT_ROWS, HEAD_DIM), lambda i: (i, 0)),
                  pl.BlockSpec((1, MIX_W), lambda i: (0, 0)),
                  pl.BlockSpec((1, KV_W), lambda i: (0, 0))],
        out_specs=(pl.BlockSpec((ATT_ROWS, MIX_W), lambda i: (i, 0)),
                   pl.BlockSpec((ATT_ROWS, KV_W), lambda i: (i, 0))),
        out_shape=(jax.ShapeDtypeStruct((rows, MIX_W), F32), jax.ShapeDtypeStruct((rows, KV_W), F32)),
        compiler_params=_cparams(1),
    )(z, z, cos_t, sin_t, qw, kw)


def _attn_kernel(*refs, has_cache):
    if has_cache:
        q_ref, k_ref, v_ref, ck_ref, cv_ref, o_ref = refs
    else:
        q_ref, k_ref, v_ref, o_ref = refs
    k = k_ref[...]
    v = v_ref[...]
    for g in range(N_HEADS // N_KV):
        qh = q_ref[:, _hs(g)]
        s = _dot_nt(qh, k) * SCALE
        m = jnp.max(s, axis=-1, keepdims=True)
        if has_cache:
            sc = _dot_nt(qh, ck_ref[...]) * SCALE
            m = jnp.maximum(m, jnp.max(sc, axis=-1, keepdims=True))
            pc = jnp.exp(sc - m)
        p = jnp.exp(s - m)
        den = jnp.sum(p, axis=-1, keepdims=True)
        if has_cache:
            den = den + jnp.sum(pc, axis=-1, keepdims=True)
        inv = 1.0 / den
        o = _dot(p * inv, v)
        if has_cache:
            o = o + _dot(pc * inv, cv_ref[...])
        o_ref[:, _hs(g)] = o


def _attend(qn, kn, z, row0, nseq, t, y_rows, cache=None, l=0):
    gw = (N_HEADS // N_KV) * HEAD_DIM
    nqb = t // Q_BLOCK
    rb0 = row0 // Q_BLOCK
    sb0 = row0 // t
    vcol = Z_ATV // HEAD_DIM
    in_specs = [pl.BlockSpec((Q_BLOCK, gw), lambda b, kv, qi: (rb0 + b * nqb + qi, kv)),
                pl.BlockSpec((t, HEAD_DIM), lambda b, kv, qi: (sb0 + b, kv)),
                pl.BlockSpec((t, HEAD_DIM), lambda b, kv, qi: (sb0 + b, vcol + kv))]
    args = [qn, kn, z]
    if cache is not None:
        ck, cv = cache
        past = ck.shape[2]
        cspec = pl.BlockSpec((None, None, past, HEAD_DIM), lambda b, kv, qi: (b, l, 0, kv))
        in_specs += [cspec, cspec]
        args += [ck, cv]
    return pl.pallas_call(
        functools.partial(_attn_kernel, has_cache=cache is not None),
        grid=(nseq, N_KV, nqb),
        in_specs=in_specs,
        out_specs=pl.BlockSpec((Q_BLOCK, gw), lambda b, kv, qi: (b * nqb + qi, kv)),
        out_shape=jax.ShapeDtypeStruct((y_rows, MIX_W), F32),
        compiler_params=_cparams(3),
    )(*args)


POST_ROWS = 512


def _post_kernel(gf, gb, mf, mb, df, db, zg, zo, zd, at, nw_ref, o_ref):
    nw = nw_ref[...]
    o_ref[0] = (_head_rms(gf[...] + gb[...], nw[0:1, :]) * _silu(zg[...])).astype(BF16)
    o_ref[1] = (_head_rms(mf[...] + mb[...], nw[1:2, :]) * _sigmoid(zo[...])).astype(BF16)
    o_ref[2] = (_head_rms(df[...] + db[...], nw[2:3, :]) * _silu(zd[...])).astype(BF16)
    o_ref[3] = at[...].astype(BF16)


def _branch_post(outs, z, y_at, norm_w):
    rows = z.shape[0]
    tm = min(POST_ROWS, rows)
    row = lambda c: pl.BlockSpec((tm, MIX_W), lambda i: (i, c))
    in_specs = [row(0)] * 6 + [row(Z_GLA // MIX_W + 3), row(Z_ML // MIX_W + 3), row(Z_GDG // MIX_W),
                               row(0), pl.BlockSpec((8, MIX_W), lambda i: (0, 0))]
    return pl.pallas_call(
        _post_kernel,
        grid=(rows // tm,),
        in_specs=in_specs,
        out_specs=pl.BlockSpec((N_BRANCH, tm, MIX_W), lambda i: (0, i, 0)),
        out_shape=jax.ShapeDtypeStruct((N_BRANCH, rows, MIX_W), BF16),
        compiler_params=_cparams(1),
    )(*outs, z, z, z, y_at, norm_w)


def _merge_kernel(x_ref, g_ref, y_ref, zm_ref, wb_ref, wo_ref, o_ref, acc_ref, *, seqs, tm):
    n = pl.program_id(1)
    r = seqs.mod_row(pl.program_id(0) * tm)
    p = _sigmoid(zm_ref[...]) * jnp.dot(y_ref[...], wb_ref[...], preferred_element_type=F32)

    @pl.when(n == 0)
    def _():
        acc_ref[...] = p

    @pl.when(n > 0)
    def _():
        acc_ref[...] += p

    @pl.when(n == N_BRANCH - 1)
    def _():
        out = jnp.dot(acc_ref[...].astype(BF16), wo_ref[...], preferred_element_type=F32)
        o_ref[...] = x_ref[...] + g_ref[pl.ds(r, 1), :] * out


def _merge(x, mod, ybr, z, wb, wo, seqs, l, tm):
    rows = x.shape[0]
    return pl.pallas_call(
        functools.partial(_merge_kernel, seqs=seqs, tm=tm),
        grid=(rows // tm, N_BRANCH),
        in_specs=[pl.BlockSpec((tm, D_MODEL), lambda m, n: (m, 0), pipeline_mode=pl.Buffered(1)),
                  pl.BlockSpec((None, 8, D_MODEL), lambda m, n: (l, 0, 5)),
                  pl.BlockSpec((None, tm, MIX_W), lambda m, n: (n, m, 0)),
                  pl.BlockSpec((tm, D_MODEL), lambda m, n: (m, n)),
                  pl.BlockSpec((None, None, MIX_W, D_MODEL), lambda m, n: (l, n, 0, 0)),
                  pl.BlockSpec((None, D_MODEL, D_MODEL), lambda m, n: (l, 0, 0), pipeline_mode=pl.Buffered(1))],
        out_specs=pl.BlockSpec((tm, D_MODEL), lambda m, n: (m, 0)),
        out_shape=jax.ShapeDtypeStruct((rows, D_MODEL), F32),
        scratch_shapes=[pltpu.VMEM((tm, D_MODEL), F32)],
        compiler_params=_cparams(2),
    )(x, mod, ybr, z, wb, wo)


def _lane_row(vals, lane0):
    v = vals.reshape(-1).astype(F32)
    return jnp.zeros((1, LANE), F32).at[0, lane0:lane0 + v.shape[0]].set(v)


def _rope_tables(seqs):
    t = seqs.tl
    row = (np.arange(t) // GRID_W).astype(np.float32)
    col = (np.arange(t) % GRID_W).astype(np.float32)
    n_pairs = HEAD_DIM // 4
    inv = jnp.asarray(ROPE_THETA, F32) ** (-jnp.arange(n_pairs, dtype=F32) / n_pairs)
    ang = jnp.concatenate([jnp.asarray(row)[:, None] * inv, jnp.asarray(col)[:, None] * inv], axis=-1)
    cos = jnp.repeat(jnp.cos(ang), 2, axis=-1)
    sin = jnp.repeat(jnp.sin(ang), 2, axis=-1) * jnp.asarray(np.tile([-1.0, 1.0], HEAD_DIM // 2), F32)
    cos = jnp.concatenate([jnp.ones((seqs.ctx_rows, HEAD_DIM), F32)] + [cos] * seqs.nl, axis=0)
    sin = jnp.concatenate([jnp.zeros((seqs.ctx_rows, HEAD_DIM), F32)] + [sin] * seqs.nl, axis=0)
    return cos, sin


def _trunk(seqs, x, cvec, cache_k, cache_v, state_gla, state_c, state_n, state_m, state_gdn,
           w_ada, b_ada, w_ffn_in, w_ffn_out, w_in, gla_w2, gla_b2, gla_norm_w, ml_gate_b, ml_norm_w,
           gd_conv_w, gd_a_log, gd_dt_bias, gd_norm_w, q_norm_w, k_norm_w, w_branch, w_out, final_norm_w,
           tm_ffn=1024, tf=256, tm_in=2048, tm_pre=512, tm_mg=512):
    depth = w_in.shape[0]
    rows = seqs.rows
    tm_ffn, tm_in, tm_mg = min(tm_ffn, seqs.tl), min(tm_in, seqs.tl), min(tm_mg, seqs.tl)
    mod = _ada(cvec, w_ada, b_ada)
    w_t = jnp.swapaxes(w_in, 1, 2)
    wb = w_branch.astype(BF16)
    wo = w_out.astype(BF16)
    cos_t, sin_t = _rope_tables(seqs)
    fw = final_norm_w.reshape(1, D_MODEL)
    past = cache_k.shape[2]
    ck = cache_k.reshape(cache_k.shape[:2] + (past, KV_W))
    cv = cache_v.reshape(cache_v.shape[:2] + (past, KV_W))
    zpad = jnp.zeros((LANE - GLA_RANK, MIX_W), F32)
    ctx = []
    for l in range(depth):
        x = _ffn(x, mod, w_ffn_in, w_ffn_out, fw, seqs, l, 0, False, tm_ffn, tf)
        h, zs = _premix(x, mod, w_t, seqs, l, tm_pre)
        z = _inproj(h, w_t, l, tm_in)

        w2f = jnp.concatenate([gla_w2[l, 0], zpad], axis=0).astype(BF16)
        w2b = jnp.concatenate([zpad[:GLA_RANK], gla_w2[l, 1], zpad[:LANE - 2 * GLA_RANK]], axis=0).astype(BF16)
        zq = _gdn_prep(z, jnp.concatenate([gd_conv_w[l], jnp.zeros((8 - CONV_K, 3 * MIX_W), F32)], axis=0), seqs)
        ab_lanes = jnp.concatenate([gd_a_log[l], jnp.zeros((N_DIR, N_HEADS), F32)], axis=1)
        dt_lanes = jnp.concatenate([gd_dt_bias[l], jnp.zeros((N_DIR, N_HEADS), F32)], axis=1)
        gf, gb, mf, mb, df, db, st_gla, st_c, st_n, st_m, st_gd = _scans(
            z, zq, zs, seqs, l, w2f, w2b, gla_b2[l, 0:1], gla_b2[l, 1:2], _lane_row(ml_gate_b[l], L_IF),
            _lane_row(ab_lanes, L_AB), _lane_row(dt_lanes, L_AB),
            state_gla, state_c, state_n, state_m, state_gdn)

        qn, kn = _attn_prep(z, cos_t, sin_t, jnp.tile(q_norm_w[l], N_HEADS)[None, :],
                            jnp.tile(k_norm_w[l], N_KV)[None, :])
        y_ctx = _attend(qn, kn, z, 0, seqs.nc, seqs.tc, seqs.ctx_rows)
        y_lat = _attend(qn, kn, z, seqs.ctx_rows, seqs.nl, seqs.tl, seqs.nl * seqs.tl, cache=(ck, cv), l=l)
        y_at = jnp.concatenate([y_ctx, y_lat], axis=0)

        norm_w = jnp.stack([jnp.tile(w[l], N_HEADS) for w in (gla_norm_w, ml_norm_w, gd_norm_w)]
                           + [jnp.zeros((MIX_W,), F32)] * 5, axis=0)
        ybr = _branch_post((gf, gb, mf, mb, df, db), z, y_at, norm_w)
        x = _merge(x, mod, ybr, z, wb, wo, seqs, l, tm_mg)
        x = _ffn(x, mod, w_ffn_in, w_ffn_out, fw, seqs, l, 1, l == depth - 1, tm_ffn, tf)

        nc, tc = seqs.nc, seqs.tc
        ctx.append(dict(
            k=kn[:seqs.ctx_rows].reshape(nc, tc, N_KV, HEAD_DIM),
            v=z[:seqs.ctx_rows, Z_ATV:Z_ATV + KV_W].reshape(nc, tc, N_KV, HEAD_DIM),
            gla=st_gla, mc=st_c,
            mn=st_n.reshape(nc, N_DIR, N_HEADS, HEAD_DIM),
            mm=st_m[:, :, 0].reshape(nc, N_DIR, N_HEADS),
            gd=st_gd))
    return x, ctx


def kernel(x_prompt, x_sample, cache_k, cache_v, state_gla, state_mlstm_c, state_mlstm_n, state_mlstm_m,
           state_gdn, c, c_ctx, w_ada, b_ada, w_ffn_in, w_ffn_out, w_in, gla_w2, gla_b2, gla_norm_w,
           ml_gate_b, ml_norm_w, gd_conv_w, gd_a_log, gd_dt_bias, gd_norm_w, q_norm_w, k_norm_w,
           w_branch, w_out, final_norm_w):
    nc, tc, _ = x_prompt.shape
    nl, tl, _ = x_sample.shape
    seqs = _Seqs(nc, tc, nl, tl)
    x = jnp.concatenate([x_prompt.reshape(nc * tc, D_MODEL), x_sample.reshape(nl * tl, D_MODEL)], axis=0)
    cvec = jnp.concatenate([c_ctx[None, :], c, jnp.zeros((8 - 1 - nl, D_MODEL), F32)], axis=0)
    y, ctx = _trunk(seqs, x, cvec, cache_k, cache_v, state_gla, state_mlstm_c, state_mlstm_n, state_mlstm_m,
                    state_gdn, w_ada, b_ada, w_ffn_in, w_ffn_out, w_in, gla_w2, gla_b2, gla_norm_w,
                    ml_gate_b, ml_norm_w, gd_conv_w, gd_a_log, gd_dt_bias, gd_norm_w, q_norm_w, k_norm_w,
                    w_branch, w_out, final_norm_w)
    y_prompt = y[:nc * tc].reshape(nc, tc, D_MODEL)
    y_sample = y[nc * tc:].reshape(nl, tl, D_MODEL)
    stack = lambda name: jnp.stack([cx[name] for cx in ctx], axis=1)
    return (y_prompt, y_sample, stack('k'), stack('v'), stack('gla'), stack('mc'), stack('mn'),
            stack('mm'), stack('gd'))
```

```python
import functools

import numpy as np
import jax
import jax.numpy as jnp
from jax import lax
from jax.experimental import pallas as pl
from jax.experimental.pallas import tpu as pltpu

F32 = jnp.float32
BF16 = jnp.bfloat16

D_MODEL = 2048
DEPTH = 4
GRID_W = 64
N_HEADS = 4
HEAD_DIM = 128
MIX_W = N_HEADS * HEAD_DIM
N_KV = 2
KV_W = N_KV * HEAD_DIM
GLA_RANK = 16
GLA_TAU = 16.0
CHUNK = 64
SUB = 16
Q_BLOCK = 128
CONV_K = 5
D_FF = 5632
ROPE_THETA = 10000.0
N_BRANCH = 4
N_DIR = 2
N_MOD = 9
EPS = 1e-6
SCALE = HEAD_DIM ** -0.5

Z_MERGE = 0
Z_GLA = N_BRANCH * D_MODEL
Z_ML = Z_GLA + 4 * MIX_W
Z_GDQKV = Z_ML + 4 * MIX_W
Z_GDG = Z_GDQKV + 3 * MIX_W
Z_ATQ = Z_GDG + MIX_W
Z_ATK = Z_ATQ + MIX_W
Z_ATV = Z_ATK + KV_W
NZ = Z_ATV + KV_W
LANE = 128
SUBLANES = 8
L_LR = 0
L_IF = N_DIR * GLA_RANK
L_AB = L_IF + N_DIR * 2 * N_HEADS

W_LR = 4 * MIX_W
W_ML = W_LR + N_DIR * GLA_RANK
W_IF = W_ML + 4 * MIX_W
W_GD = W_IF + N_DIR * 2 * N_HEADS
W_AB = W_GD + 4 * MIX_W
W_AT = W_AB + N_DIR * 2 * N_HEADS
W_MG = W_AT + MIX_W + 2 * KV_W
IN_TILE = 512
TILE_SRC = ([W_MG + IN_TILE * j for j in range(N_BRANCH * D_MODEL // IN_TILE)]
            + [IN_TILE * j for j in range(4)] + [W_ML + IN_TILE * j for j in range(4)]
            + [W_GD + IN_TILE * j for j in range(4)] + [W_AT + IN_TILE * j for j in range(2)])

VMEM_LIMIT = 56 * 1024 * 1024
FFN_VMEM_LIMIT = 60 * 1024 * 1024


def _cparams(n_axes, vmem_limit=VMEM_LIMIT):
    return pltpu.CompilerParams(dimension_semantics=("arbitrary",) * n_axes,
                                vmem_limit_bytes=vmem_limit)


def _dot(a, b):
    return jnp.dot(a.astype(BF16), b.astype(BF16), preferred_element_type=F32)


def _dot_nt(a, b):
    return lax.dot_general(a.astype(BF16), b.astype(BF16), (((1,), (1,)), ((), ())),
                           preferred_element_type=F32)


def _dot_tn(a, b):
    return lax.dot_general(a.astype(BF16), b.astype(BF16), (((0,), (0,)), ((), ())),
                           preferred_element_type=F32)


def _split(a):
    hi = a.astype(BF16)
    return hi, (a - hi.astype(F32)).astype(BF16)


def _dot_split(a, b):
    ah, al = a
    bh, bl = b
    return (jnp.dot(ah, bh, preferred_element_type=F32) + jnp.dot(ah, bl, preferred_element_type=F32)
            + jnp.dot(al, bh, preferred_element_type=F32))


def _cumsum_mask(mask, x):
    m = mask.astype(BF16)
    x0 = x.astype(BF16)
    r1 = x - x0.astype(F32)
    x1 = r1.astype(BF16)
    x2 = (r1 - x1.astype(F32)).astype(BF16)
    return (jnp.dot(m, x0, preferred_element_type=F32) + jnp.dot(m, x1, preferred_element_type=F32)
            + jnp.dot(m, x2, preferred_element_type=F32))


def _sigmoid(x):
    return 1.0 / (1.0 + jnp.exp(-x))


def _silu(x):
    return x * _sigmoid(x)


def _softplus(x):
    return jnp.maximum(x, 0.0) + jnp.log1p(jnp.exp(-jnp.abs(x)))


def _log_sigmoid(x):
    return -_softplus(-x)


def _rmsmod(x, sc, sh):
    ms = jnp.mean(x * x, axis=-1, keepdims=True)
    return x * lax.rsqrt(ms + EPS) * (1.0 + sc) + sh


def _head_rms(x, w):
    outs = []
    for h in range(x.shape[1] // HEAD_DIM):
        seg = x[:, h * HEAD_DIM:(h + 1) * HEAD_DIM]
        ms = jnp.mean(seg * seg, axis=-1, keepdims=True)
        outs.append(seg * lax.rsqrt(ms + EPS))
    return jnp.concatenate(outs, axis=1) * w


class _Seqs:
    def __init__(self, nc, tc, nl, tl):
        self.nc, self.tc, self.nl, self.tl = nc, tc, nl, tl
        self.ctx_rows = nc * tc
        self.rows = nc * tc + nl * tl

    def mod_row(self, start):
        return jnp.where(start < self.ctx_rows, 0, 1 + (start - self.ctx_rows) // self.tl)

    def scan_tables(self):
        fwd, bwd, flg, sq = [], [], [], []
        base = 0
        for s in range(self.nc + self.nl):
            lat = s >= self.nc
            n = (self.tl if lat else self.tc) // CHUNK
            for j in range(n):
                fwd.append(base + j)
                bwd.append(base + n - 1 - j)
                flg.append((1 if j == 0 else 0) | (2 if j == n - 1 else 0) | (4 if lat else 0))
                sq.append(s)
            base += n
        return tuple(jnp.asarray(np.array(a, np.int32)) for a in (fwd, bwd, flg, sq))


def _ada_kernel(c_ref, w_ref, b_ref, o_ref):
    o_ref[...] = _dot(_silu(c_ref[...]), w_ref[...]) + b_ref[...]


def _ada(cvec, w_ada, b_ada):
    tn = 1024
    nmod = N_MOD * D_MODEL
    depth = w_ada.shape[0]
    return pl.pallas_call(
        _ada_kernel,
        grid=(depth, nmod // tn),
        in_specs=[pl.BlockSpec((8, D_MODEL), lambda l, j: (0, 0)),
                  pl.BlockSpec((None, D_MODEL, tn), lambda l, j: (l, 0, j)),
                  pl.BlockSpec((None, 1, tn), lambda l, j: (l, 0, j))],
        out_specs=pl.BlockSpec((None, 8, tn), lambda l, j: (l, 0, j)),
        out_shape=jax.ShapeDtypeStruct((depth, 8, nmod), F32),
        compiler_params=_cparams(2),
    )(cvec, w_ada, b_ada.reshape(depth, 1, nmod))


def _ffn_kernel(x_ref, sh_ref, sc_ref, g_ref, wg_ref, wu_ref, wo_ref, fw_ref, o_ref, h_ref,
                *, seqs, tm, nf, final):
    f = pl.program_id(1)
    r = seqs.mod_row(pl.program_id(0) * tm)

    @pl.when(f == 0)
    def _():
        h_ref[...] = _rmsmod(x_ref[...], sc_ref[pl.ds(r, 1), :], sh_ref[pl.ds(r, 1), :]).astype(BF16)
        o_ref[...] = jnp.zeros_like(o_ref)

    h = h_ref[...]
    g = jnp.dot(h, wg_ref[...].astype(BF16), preferred_element_type=F32)
    u = jnp.dot(h, wu_ref[...].astype(BF16), preferred_element_type=F32)
    a = (_silu(g) * u).astype(BF16)
    o_ref[...] += jnp.dot(a, wo_ref[...].astype(BF16), preferred_element_type=F32)

    @pl.when(f == nf - 1)
    def _():
        y = x_ref[...] + 0.5 * g_ref[pl.ds(r, 1), :] * o_ref[...]
        if final:
            ms = jnp.mean(y * y, axis=-1, keepdims=True)
            y = y * lax.rsqrt(ms + EPS) * fw_ref[...]
        o_ref[...] = y


def _ffn(x, mod, w_in, w_out, fw, seqs, l, i, final, tm, tf):
    rows = x.shape[0]
    nf = D_FF // tf
    j0 = 0 if i == 0 else 6
    kern = functools.partial(_ffn_kernel, seqs=seqs, tm=tm, nf=nf, final=final)
    modspec = lambda j: pl.BlockSpec((None, 8, D_MODEL), lambda m, f: (l, 0, j))
    return pl.pallas_call(
        kern,
        grid=(rows // tm, nf),
        in_specs=[pl.BlockSpec((tm, D_MODEL), lambda m, f: (m, 0), pipeline_mode=pl.Buffered(1)),
                  modspec(j0), modspec(j0 + 1), modspec(j0 + 2),
                  pl.BlockSpec((None, None, D_MODEL, tf), lambda m, f: (l, i, 0, f)),
                  pl.BlockSpec((None, None, D_MODEL, tf), lambda m, f: (l, i, 0, nf + f)),
                  pl.BlockSpec((None, None, tf, D_MODEL), lambda m, f: (l, i, f, 0)),
                  pl.BlockSpec((1, D_MODEL), lambda m, f: (0, 0))],
        out_specs=pl.BlockSpec((tm, D_MODEL), lambda m, f: (m, 0), pipeline_mode=pl.Buffered(1)),
        out_shape=jax.ShapeDtypeStruct((rows, D_MODEL), F32),
        scratch_shapes=[pltpu.VMEM((tm, D_MODEL), BF16)],
        compiler_params=_cparams(2, FFN_VMEM_LIMIT),
    )(x, mod, mod, mod, w_in, w_in, w_out, fw)


def _premix_kernel(x_ref, sh_ref, sc_ref, wlr_ref, wif_ref, wab_ref, h_ref, zs_ref, *, seqs, tm):
    r = seqs.mod_row(pl.program_id(0) * tm)
    h = _rmsmod(x_ref[...], sc_ref[pl.ds(r, 1), :], sh_ref[pl.ds(r, 1), :]).astype(BF16)
    h_ref[...] = h
    used = wlr_ref.shape[1] + wif_ref.shape[1] + wab_ref.shape[1]
    ws = jnp.concatenate([wlr_ref[0], wif_ref[0], wab_ref[0], jnp.zeros((LANE - used, D_MODEL), F32)],
                         axis=0).astype(BF16)
    zs_ref[...] = _dot_nt(h, ws)


def _premix(x, mod, w_t, seqs, l, tm):
    rows = x.shape[0]
    modspec = lambda j: pl.BlockSpec((None, 8, D_MODEL), lambda m: (l, 0, j))
    wspec = lambda start, n: pl.BlockSpec((pl.Element(1), pl.Element(n), pl.Element(D_MODEL)),
                                          lambda m: (l, start, 0))
    return pl.pallas_call(
        functools.partial(_premix_kernel, seqs=seqs, tm=tm),
        grid=(rows // tm,),
        in_specs=[pl.BlockSpec((tm, D_MODEL), lambda m: (m, 0)), modspec(3), modspec(4),
                  wspec(W_LR, W_ML - W_LR), wspec(W_IF, W_GD - W_IF), wspec(W_AB, W_AT - W_AB)],
        out_specs=(pl.BlockSpec((tm, D_MODEL), lambda m: (m, 0)), pl.BlockSpec((tm, LANE), lambda m: (m, 0))),
        out_shape=(jax.ShapeDtypeStruct((rows, D_MODEL), BF16), jax.ShapeDtypeStruct((rows, LANE), F32)),
        compiler_params=_cparams(1),
    )(x, mod, mod, w_t, w_t, w_t)


def _inproj_kernel(row_ref, h_ref, w_ref, o_ref, w_scr):
    @pl.when(pl.program_id(1) == 0)
    def _():
        w_scr[...] = w_ref[0].astype(BF16)

    o_ref[...] = lax.dot_general(h_ref[...], w_scr[...], (((1,), (1,)), ((), ())),
                                 preferred_element_type=F32)


def _inproj(h, w_t, l, tm):
    rows = h.shape[0]
    assert all(c % SUBLANES == 0 for c in TILE_SRC)
    src = jnp.asarray(np.array(TILE_SRC, np.int32) // SUBLANES)
    gs = pltpu.PrefetchScalarGridSpec(
        num_scalar_prefetch=1, grid=(len(TILE_SRC), rows // tm),
        in_specs=[pl.BlockSpec((tm, D_MODEL), lambda n, m, c: (m, 0)),
                  pl.BlockSpec((pl.Element(1), pl.Element(IN_TILE), pl.Element(D_MODEL)),
                               lambda n, m, c: (l, c[n] * SUBLANES, 0))],
        out_specs=pl.BlockSpec((tm, IN_TILE), lambda n, m, c: (m, n)),
        scratch_shapes=[pltpu.VMEM((IN_TILE, D_MODEL), BF16)])
    return pl.pallas_call(
        _inproj_kernel, grid_spec=gs,
        out_shape=jax.ShapeDtypeStruct((rows, NZ), F32),
        compiler_params=_cparams(2),
    )(src, h, w_t)


def _tri_masks(rev):
    ri = lax.broadcasted_iota(jnp.int32, (CHUNK, CHUNK), 0)
    ci = lax.broadcasted_iota(jnp.int32, (CHUNK, CHUNK), 1)
    if rev:
        return ri <= ci, ri < ci
    return ri >= ci, ri > ci


def _flags(flg):
    first = (flg & 1) != 0
    last = (flg & 2) != 0
    lat = (flg & 4) != 0
    return first, last, lat


def _hs(h):
    return slice(h * HEAD_DIM, (h + 1) * HEAD_DIM)


def _gla_init(lat, init_ref, st_scr):
    if lat:
        for d in range(N_DIR):
            for h in range(N_HEADS):
                st_scr[d, h] = init_ref[d, h].T
    else:
        st_scr[...] = jnp.zeros_like(st_scr)


def _gla_final(st_out, st_scr):
    for d in range(N_DIR):
        for h in range(N_HEADS):
            st_out[d, h] = st_scr[d, h].T


def _gla_step(qf, kf, vf, sf, qb, kb, vb, sb, w2f, w2b, b2f, b2b, of_ref, ob_ref, st_scr, b_scr):
    dirs = ((qf, kf, vf, sf, w2f, b2f, of_ref), (qb, kb, vb, sb, w2b, b2b, ob_ref))
    for d, (q_ref, k_ref, v_ref, s_ref, w2, b2, o_ref) in enumerate(dirs):
        rev = d == 1
        incl, _ = _tri_masks(rev)
        pre = jnp.dot(s_ref[...].astype(BF16), w2[...], preferred_element_type=F32) + b2[...]
        lg = _log_sigmoid(pre) / GLA_TAU
        b = _cumsum_mask(incl, lg)
        b_scr[d] = b
        bend = jnp.sum(lg, axis=0, keepdims=True)
        q = q_ref[...] * SCALE
        k = k_ref[...]
        v = v_ref[...]
        qd = q * jnp.exp(b)
        kd = k * jnp.exp(bend - b)
        eb = jnp.exp(bend)

        ri = lax.broadcasted_iota(jnp.int32, (CHUNK, CHUNK), 0)
        ci = lax.broadcasted_iota(jnp.int32, (CHUNK, CHUNK), 1)
        if rev:
            off_mask = ci >= (ri // SUB + 1) * SUB
        else:
            off_mask = ci < (ri // SUB) * SUB
        nsub = CHUNK // SUB
        att_rows = [[] for _ in range(N_HEADS)]
        for blk in range(nsub):
            r0 = blk * SUB
            has_src = (blk < nsub - 1) if rev else (blk > 0)
            if not has_src:
                for h in range(N_HEADS):
                    att_rows[h].append(jnp.zeros((SUB, CHUNK), F32))
                continue
            edge = r0 + SUB if rev else r0 - 1
            bref = b[edge:edge + 1, :]
            qe = q[r0:r0 + SUB, :] * jnp.exp(b[r0:r0 + SUB, :] - bref)
            ke = k * jnp.exp(jnp.minimum(bref - b, 0.0))
            for h in range(N_HEADS):
                att_rows[h].append(_dot_nt(qe[:, _hs(h)], ke[:, _hs(h)]))
        o_main = []
        for h in range(N_HEADS):
            att = jnp.where(off_mask, jnp.concatenate(att_rows[h], axis=0), 0.0)
            o_main.append(_dot_nt(qd[:, _hs(h)], st_scr[d, h]) + _dot(att, v[:, _hs(h)]))
        o_main = jnp.concatenate(o_main, axis=1)
        yield

        rowi = lax.broadcasted_iota(jnp.int32, (SUBLANES, 1), 0)
        for r0 in range(0, CHUNK, SUBLANES):
            blk0 = (r0 // SUB) * SUB
            bb = b[r0:r0 + SUBLANES, :]
            qq = q[r0:r0 + SUBLANES, :]
            acc = None
            for s in range(blk0, blk0 + SUB):
                if (s > r0 + SUBLANES - 1 and not rev) or (s < r0 and rev):
                    continue
                bs = b_scr[d, s:s + 1, :]
                ks = k_ref[s:s + 1, :]
                vs = v_ref[s:s + 1, :]
                p = qq * ks * jnp.exp(bb - bs)
                valid = (rowi <= s - r0) if rev else (rowi >= s - r0)
                parts = []
                for h in range(N_HEADS):
                    c = jnp.sum(p[:, _hs(h)], axis=-1, keepdims=True)
                    parts.append(jnp.where(valid, c, 0.0) * vs[:, _hs(h)])
                term = jnp.concatenate(parts, axis=1)
                acc = term if acc is None else acc + term
            o_ref[r0:r0 + SUBLANES, :] = o_main[r0:r0 + SUBLANES, :] + acc
            yield

        for h in range(N_HEADS):
            st_scr[d, h] = st_scr[d, h] * eb[:, _hs(h)] + _dot_tn(v[:, _hs(h)], kd[:, _hs(h)])


def _copy_or_zero(lat, pairs):
    for src, dst in pairs:
        dst[...] = src[...] if lat else jnp.zeros_like(dst)


def _mlstm_step(qf, kf, vf, sf, qb, kb, vb, sb, bias_ref, of_ref, ob_ref, c_scr, n_scr, m_scr):
    dirs = ((qf, kf, vf, sf, of_ref), (qb, kb, vb, sb, ob_ref))
    for d, (q_ref, k_ref, v_ref, s_ref, o_ref) in enumerate(dirs):
        rev = d == 1
        incl, _ = _tri_masks(rev)
        last_row = 0 if rev else CHUNK - 1
        gates = s_ref[...] + bias_ref[...]
        lf_all = _log_sigmoid(gates)
        f_all = _cumsum_mask(incl, lf_all)
        f_all_t = f_all.T
        gates_t = gates.T
        q = q_ref[...]
        k = k_ref[...] * SCALE
        v = v_ref[...]
        for h in range(N_HEADS):
            ji = L_IF + d * 2 * N_HEADS + h
            jf = ji + N_HEADS
            r = d * N_HEADS + h
            fc = f_all[:, jf:jf + 1]
            frow = f_all_t[jf:jf + 1, :]
            igc = gates[:, ji:ji + 1]
            igrow = gates_t[ji:ji + 1, :]
            m_prev = m_scr[r:r + 1, 0:1]
            n_prev = n_scr[r:r + 1, :]
            c_prev = c_scr[d, h]
            qh, kh, vh = q[:, _hs(h)], k[:, _hs(h)], v[:, _hs(h)]
            log_d = jnp.where(incl, fc - frow + igrow, -jnp.inf)
            inter = fc + m_prev
            m_t = jnp.maximum(inter, jnp.max(log_d, axis=-1, keepdims=True))
            dmat = jnp.exp(log_d - m_t)
            a_in = jnp.exp(inter - m_t)
            s = _dot_nt(qh, kh) * dmat
            num = a_in * _dot(qh, c_prev) + _dot(s, vh)
            den = a_in * jnp.sum(qh * n_prev, axis=-1, keepdims=True) + jnp.sum(s, axis=-1, keepdims=True)
            o_ref[:, _hs(h)] = num / jnp.maximum(jnp.abs(den), jnp.exp(-m_t))
            m_new = m_t[last_row:last_row + 1, :]
            f_end = fc[last_row:last_row + 1, :]
            w_end = jnp.exp(f_end - fc + igc - m_new)
            a_end = jnp.exp(f_end + m_prev - m_new)
            kw = kh * w_end
            c_scr[d, h] = a_end * c_prev + _dot_tn(kw, vh)
            n_scr[r:r + 1, :] = a_end * n_prev + jnp.sum(kw, axis=0, keepdims=True)
            m_scr[r:r + 1, :] = jnp.broadcast_to(m_new, (1, LANE))
            yield


def _unit_tri_inverse_all(ns):
    ri = lax.broadcasted_iota(jnp.int32, (CHUNK, CHUNK), 0)
    ci = lax.broadcasted_iota(jnp.int32, (CHUNK, CHUNK), 1)
    same16 = (ri // SUB) == (ci // SUB)
    same32 = (ri // (2 * SUB)) == (ci // (2 * SUB))
    eye = (ri == ci).astype(F32)
    nd = [jnp.where(same16, n, 0.0) for n in ns]
    n1 = [_split(jnp.where(jnp.logical_and(same32, jnp.logical_not(same16)), n, 0.0)) for n in ns]
    n2 = [_split(jnp.where(same32, 0.0, n)) for n in ns]
    t = [eye - x for x in nd]
    nds = [_split(x) for x in nd]
    p = [_dot_split(x, x) for x in nds]
    yield
    for level in range(3):
        ps = [_split(x) for x in p]
        t = [x + _dot_split(_split(x), y) for x, y in zip(t, ps)]
        if level < 2:
            p = [_dot_split(y, y) for y in ps]
        yield
    for nn in (n1, n2):
        ts = [_split(x) for x in t]
        a = [_dot_split(y, x) for x, y in zip(ts, nn)]
        yield
        t = [x - _dot_split(xs, _split(y)) for x, xs, y in zip(t, ts, a)]
        yield
    return t


def _gdn_step(qf, kf, vf, sf, qb, kb, vb, sb, alog_ref, dtb_ref, of_ref, ob_ref, s_scr):
    units = []
    dirs = ((qf, kf, vf, sf, of_ref), (qb, kb, vb, sb, ob_ref))
    for d, (q_ref, k_ref, v_ref, s_ref, o_ref) in enumerate(dirs):
        rev = d == 1
        incl, strict = _tri_masks(rev)
        last_row = 0 if rev else CHUNK - 1
        small = s_ref[...]
        g_all = -jnp.exp(alog_ref[...]) * _softplus(small + dtb_ref[...])
        beta_all = _sigmoid(small)
        gam_all = _cumsum_mask(incl, g_all)
        gam_t = gam_all.T
        q = q_ref[...]
        k = k_ref[...]
        v = v_ref[...]
        for h in range(N_HEADS):
            jg = L_AB + d * 2 * N_HEADS + h
            jb = jg + N_HEADS
            gc = gam_all[:, jg:jg + 1]
            units.append(dict(
                d=d, h=h, o_ref=o_ref, incl=incl, strict=strict, gc=gc, grow=gam_t[jg:jg + 1, :],
                beta=beta_all[:, jb:jb + 1], g_end=gc[last_row:last_row + 1, :],
                q=q[:, _hs(h)], k=k[:, _hs(h)], v=v[:, _hs(h)]))

    for u in units:
        u['decay'] = jnp.exp(jnp.where(u['incl'], u['gc'] - u['grow'], -jnp.inf))
    kk = [_dot_nt(u['k'], u['k']) for u in units]
    qk = [_dot_nt(u['q'], u['k']) for u in units]
    ns = [jnp.where(u['strict'], u['beta'] * x * u['decay'], 0.0) for u, x in zip(units, kk)]
    yield
    ts = yield from _unit_tri_inverse_all(ns)
    rhs = [_split(jnp.concatenate([u['v'] * u['beta'], u['k'] * (u['beta'] * jnp.exp(u['gc']))], axis=1))
           for u in units]
    sol = [_dot_split(_split(t), r) for t, r in zip(ts, rhs)]
    yield
    s_prev = [s_scr[u['d'], u['h']] for u in units]
    w_new = [x[:, :HEAD_DIM] - _dot(x[:, HEAD_DIM:], sp) for x, sp in zip(sol, s_prev)]
    yield
    for u, x, w, sp in zip(units, qk, w_new, s_prev):
        u['o_ref'][:, _hs(u['h'])] = _dot(u['q'] * jnp.exp(u['gc']), sp) + _dot(x * u['decay'], w)
    for u, w, sp in zip(units, w_new, s_prev):
        s_scr[u['d'], u['h']] = (jnp.exp(u['g_end']) * sp
                                 + _dot_tn(u['k'] * jnp.exp(u['g_end'] - u['gc']), w))


N_UNITS = N_DIR * N_HEADS
_DONE = object()
ST_SHAPE = (N_DIR, N_HEADS, HEAD_DIM, HEAD_DIM)


def _scan_kernel(fwd_ref, bwd_ref, flg_ref, sq_ref,
                 gqf, gkf, gvf, gqb, gkb, gvb, mqf, mkf, mvf, mqb, mkb, mvb,
                 dqf, dkf, dvf, dqb, dkb, dvb, sf, sb,
                 w2f, w2b, b2f, b2b, mbias, alog, dtb,
                 g0, c0, n0, m0, s0,
                 g_of, g_ob, m_of, m_ob, d_of, d_ob, g_out, c_out, n_out, m_out, s_out,
                 g_scr, b_scr, c_scr, n_scr, m_scr, s_scr):
    first, last, lat = _flags(flg_ref[pl.program_id(0)])
    carried = ((c0, c_scr), (n0, n_scr), (m0, m_scr), (s0, s_scr))

    @pl.when(jnp.logical_and(first, jnp.logical_not(lat)))
    def _():
        _gla_init(False, g0, g_scr)
        _copy_or_zero(False, carried)

    @pl.when(jnp.logical_and(first, lat))
    def _():
        _gla_init(True, g0, g_scr)
        _copy_or_zero(True, carried)

    stages = [_gdn_step(dqf, dkf, dvf, sf, dqb, dkb, dvb, sb, alog, dtb, d_of, d_ob, s_scr),
              _mlstm_step(mqf, mkf, mvf, sf, mqb, mkb, mvb, sb, mbias, m_of, m_ob, c_scr, n_scr, m_scr),
              _gla_step(gqf, gkf, gvf, sf, gqb, gkb, gvb, sb, w2f, w2b, b2f, b2b, g_of, g_ob, g_scr, b_scr)]
    while stages:
        for g in list(stages):
            if next(g, _DONE) is _DONE:
                stages.remove(g)

    @pl.when(jnp.logical_and(last, jnp.logical_not(lat)))
    def _():
        _gla_final(g_out, g_scr)
        for src, dst in ((c_scr, c_out), (n_scr, n_out), (m_scr, m_out), (s_scr, s_out)):
            dst[...] = src[...]


def _scans(z, zq, zs, seqs, l, w2f, w2b, b2f, b2b, mbias, alog_row, dtb_row,
           state_gla, state_c, state_n, state_m, state_gdn):
    rows = z.shape[0]
    tabs = seqs.scan_tables()
    nsteps = int(tabs[0].shape[0])

    def blk(which, col, width):
        if which == 0:
            return pl.BlockSpec((CHUNK, width), lambda i, fw, bw, fl, sq: (fw[i], col))
        return pl.BlockSpec((CHUNK, width), lambda i, fw, bw, fl, sq: (bw[i], col))

    def const(shape):
        nd = len(shape)
        return pl.BlockSpec(shape, lambda i, fw, bw, fl, sq: (0,) * nd)

    lat_i = lambda sq, i: jnp.maximum(sq[i] - seqs.nc, 0)
    ctx_i = lambda sq, i: jnp.minimum(sq[i], seqs.nc - 1)
    in_specs, args = [], []
    for src, c0 in ((z, Z_GLA // MIX_W), (z, Z_ML // MIX_W), (zq, 0)):
        for which in (0, 1):
            in_specs += [blk(which, c0, MIX_W), blk(which, c0 + 1, MIX_W), blk(which, c0 + 2, MIX_W)]
            args += [src] * 3
    in_specs += [blk(0, 0, LANE), blk(1, 0, LANE)]
    args += [zs, zs]
    in_specs += [const((LANE, MIX_W)), const((LANE, MIX_W)), const((1, MIX_W)), const((1, MIX_W)),
                 const((1, LANE)), const((1, LANE)), const((1, LANE))]
    args += [w2f, w2b, b2f, b2b, mbias, alog_row, dtb_row]
    st_in = pl.BlockSpec((None, None) + ST_SHAPE, lambda i, fw, bw, fl, sq: (lat_i(sq, i), l, 0, 0, 0, 0))
    vec_in = pl.BlockSpec((None, None, N_UNITS, LANE), lambda i, fw, bw, fl, sq: (lat_i(sq, i), l, 0, 0))
    depth = state_c.shape[1]
    n0 = state_n.reshape(seqs.nl, depth, N_UNITS, HEAD_DIM)
    m0 = jnp.broadcast_to(state_m.reshape(seqs.nl, depth, N_UNITS, 1), (seqs.nl, depth, N_UNITS, LANE))
    in_specs += [st_in, st_in, vec_in, vec_in, st_in]
    args += [state_gla, state_c, n0, m0, state_gdn]

    o_f = pl.BlockSpec((CHUNK, MIX_W), lambda i, fw, bw, fl, sq: (fw[i], 0))
    o_b = pl.BlockSpec((CHUNK, MIX_W), lambda i, fw, bw, fl, sq: (bw[i], 0))
    st_out = pl.BlockSpec((None,) + ST_SHAPE, lambda i, fw, bw, fl, sq: (ctx_i(sq, i), 0, 0, 0, 0))
    vec_out = pl.BlockSpec((None, N_UNITS, LANE), lambda i, fw, bw, fl, sq: (ctx_i(sq, i), 0, 0))
    o_sds = jax.ShapeDtypeStruct((rows, MIX_W), F32)
    st_sds = jax.ShapeDtypeStruct((seqs.nc,) + ST_SHAPE, F32)
    vec_sds = jax.ShapeDtypeStruct((seqs.nc, N_UNITS, LANE), F32)
    gs = pltpu.PrefetchScalarGridSpec(
        num_scalar_prefetch=4, grid=(nsteps,), in_specs=in_specs,
        out_specs=(o_f, o_b, o_f, o_b, o_f, o_b, st_out, st_out, vec_out, vec_out, st_out),
        scratch_shapes=[pltpu.VMEM(ST_SHAPE, F32), pltpu.VMEM((N_DIR, CHUNK, MIX_W), F32),
                        pltpu.VMEM(ST_SHAPE, F32), pltpu.VMEM((N_UNITS, LANE), F32),
                        pltpu.VMEM((N_UNITS, LANE), F32), pltpu.VMEM(ST_SHAPE, F32)])
    return pl.pallas_call(
        _scan_kernel, grid_spec=gs,
        out_shape=(o_sds,) * 6 + (st_sds, st_sds, vec_sds, vec_sds, st_sds),
        compiler_params=_cparams(1),
    )(*tabs, *args)


CONV_ROWS = 256
HALO = 8


def _conv_kernel(x_ref, prev_ref, next_ref, w_ref, o_ref, xe_ref, *, seqs):
    i = pl.program_id(0)
    start = i * CONV_ROWS
    in_lat = start >= seqs.ctx_rows
    off = jnp.where(in_lat, (start - seqs.ctx_rows) % seqs.tl, start % seqs.tc)
    seq_len = jnp.where(in_lat, seqs.tl, seqs.tc)
    xe_ref[0:HALO, :] = jnp.where(off > 0, prev_ref[...], 0.0)
    xe_ref[HALO:HALO + CONV_ROWS, :] = x_ref[...]
    xe_ref[HALO + CONV_ROWS:, :] = jnp.where(off + CONV_ROWS < seq_len, next_ref[...], 0.0)
    pad = CONV_K // 2
    y = jnp.zeros((CONV_ROWS, 3 * MIX_W), F32)
    for j in range(CONV_K):
        y = y + xe_ref[pl.ds(HALO - pad + j, CONV_ROWS), :] * w_ref[j:j + 1, :]
    y = _silu(y)
    outs = []
    for h in range(3 * N_HEADS):
        seg = y[:, _hs(h)]
        if h < 2 * N_HEADS:
            seg = seg * lax.rsqrt(jnp.sum(seg * seg, axis=-1, keepdims=True) + EPS)
            if h < N_HEADS:
                seg = seg * SCALE
        outs.append(seg)
    o_ref[...] = jnp.concatenate(outs, axis=1)


def _gdn_prep(z, conv_w, seqs):
    rows = z.shape[0]
    cw = 3 * MIX_W
    cb = Z_GDQKV // cw
    nb = rows // CONV_ROWS
    per = CONV_ROWS // HALO
    last8 = rows // HALO - 1
    return pl.pallas_call(
        functools.partial(_conv_kernel, seqs=seqs),
        grid=(nb,),
        in_specs=[pl.BlockSpec((CONV_ROWS, cw), lambda i: (i, cb)),
                  pl.BlockSpec((HALO, cw), lambda i: (jnp.maximum(i * per - 1, 0), cb)),
                  pl.BlockSpec((HALO, cw), lambda i: (jnp.minimum((i + 1) * per, last8), cb)),
                  pl.BlockSpec((8, cw), lambda i: (0, 0))],
        out_specs=pl.BlockSpec((CONV_ROWS, cw), lambda i: (i, 0)),
        out_shape=jax.ShapeDtypeStruct((rows, cw), F32),
        scratch_shapes=[pltpu.VMEM((CONV_ROWS + 2 * HALO, cw), F32)],
        compiler_params=_cparams(1),
    )(z, z, z, conv_w)


ATT_ROWS = 256


def _rope(y, cos, sin_signed):
    n = y.shape[1]
    lane = lax.broadcasted_iota(jnp.int32, y.shape, 1)
    partner = jnp.where(lane % 2 == 0, pltpu.roll(y, n - 1, axis=1), pltpu.roll(y, 1, axis=1))
    reps = n // HEAD_DIM
    c = jnp.concatenate([cos] * reps, axis=1)
    s = jnp.concatenate([sin_signed] * reps, axis=1)
    return y * c + partner * s


def _attn_prep_kernel(q_ref, k_ref, cos_ref, sin_ref, qw_ref, kw_ref, qo_ref, ko_ref):
    cos = cos_ref[...]
    sin = sin_ref[...]
    qo_ref[...] = _rope(_head_rms(q_ref[...], qw_ref[...]), cos, sin)
    ko_ref[...] = _rope(_head_rms(k_ref[...], kw_ref[...]), cos, sin)


def _attn_prep(z, cos_t, sin_t, qw, kw):
    rows = z.shape[0]
    return pl.pallas_call(
        _attn_prep_kernel,
        grid=(rows // ATT_ROWS,),
        in_specs=[pl.BlockSpec((ATT_ROWS, MIX_W), lambda i: (i, Z_ATQ // MIX_W)),
                  pl.BlockSpec((ATT_ROWS, KV_W), lambda i: (i, Z_ATK // KV_W)),
                  pl.BlockSpec((ATT_ROWS, HEAD_DIM), lambda i: (i, 0)),
                  pl.BlockSpec((ATT_ROWS, HEAD_DIM), lambda i: (i, 0)),
                  pl.BlockSpec((1, MIX_W), lambda i: (0, 0)),
                  pl.BlockSpec((1, KV_W), lambda i: (0, 0))],
        out_specs=(pl.BlockSpec((ATT_ROWS, MIX_W), lambda i: (i, 0)),
                   pl.BlockSpec((ATT_ROWS, KV_W), lambda i: (i, 0))),
        out_shape=(jax.ShapeDtypeStruct((rows, MIX_W), F32), jax.ShapeDtypeStruct((rows, KV_W), F32)),
        compiler_params=_cparams(1),
    )(z, z, cos_t, sin_t, qw, kw)


def _attn_kernel(*refs, has_cache):
    if has_cache:
        q_ref, k_ref, v_ref, ck_ref, cv_ref, o_ref = refs
    else:
        q_ref, k_ref, v_ref, o_ref = refs
    k = k_ref[...]
    v = v_ref[...]
    for g in range(N_HEADS // N_KV):
        qh = q_ref[:, _hs(g)]
        s = _dot_nt(qh, k) * SCALE
        m = jnp.max(s, axis=-1, keepdims=True)
        if has_cache:
            sc = _dot_nt(qh, ck_ref[...]) * SCALE
            m = jnp.maximum(m, jnp.max(sc, axis=-1, keepdims=True))
            pc = jnp.exp(sc - m)
        p = jnp.exp(s - m)
        den = jnp.sum(p, axis=-1, keepdims=True)
        if has_cache:
            den = den + jnp.sum(pc, axis=-1, keepdims=True)
        inv = 1.0 / den
        o = _dot(p * inv, v)
        if has_cache:
            o = o + _dot(pc * inv, cv_ref[...])
        o_ref[:, _hs(g)] = o


def _attend(qn, kn, z, row0, nseq, t, y_rows, cache=None, l=0):
    gw = (N_HEADS // N_KV) * HEAD_DIM
    nqb = t // Q_BLOCK
    rb0 = row0 // Q_BLOCK
    sb0 = row0 // t
    vcol = Z_ATV // HEAD_DIM
    in_specs = [pl.BlockSpec((Q_BLOCK, gw), lambda b, kv, qi: (rb0 + b * nqb + qi, kv)),
                pl.BlockSpec((t, HEAD_DIM), lambda b, kv, qi: (sb0 + b, kv)),
                pl.BlockSpec((t, HEAD_DIM), lambda b, kv, qi: (sb0 + b, vcol + kv))]
    args = [qn, kn, z]
    if cache is not None:
        ck, cv = cache
        past = ck.shape[2]
        cspec = pl.BlockSpec((None, None, past, HEAD_DIM), lambda b, kv, qi: (b, l, 0, kv))
        in_specs += [cspec, cspec]
        args += [ck, cv]
    return pl.pallas_call(
        functools.partial(_attn_kernel, has_cache=cache is not None),
        grid=(nseq, N_KV, nqb),
        in_specs=in_specs,
        out_specs=pl.BlockSpec((Q_BLOCK, gw), lambda b, kv, qi: (b * nqb + qi, kv)),
        out_shape=jax.ShapeDtypeStruct((y_rows, MIX_W), F32),
        compiler_params=_cparams(3),
    )(*args)


POST_ROWS = 512


def _post_kernel(gf, gb, mf, mb, df, db, zg, zo, zd, atc, atl, nw_ref, o_ref, *, ctx_blocks):
    nw = nw_ref[...]
    o_ref[0] = (_head_rms(gf[...] + gb[...], nw[0:1, :]) * _silu(zg[...])).astype(BF16)
    o_ref[1] = (_head_rms(mf[...] + mb[...], nw[1:2, :]) * _sigmoid(zo[...])).astype(BF16)
    o_ref[2] = (_head_rms(df[...] + db[...], nw[2:3, :]) * _silu(zd[...])).astype(BF16)
    at = jnp.where(pl.program_id(0) < ctx_blocks, atc[...], atl[...])
    o_ref[3] = at.astype(BF16)


def _branch_post(outs, z, y_ctx, y_lat, norm_w):
    rows = z.shape[0]
    tm = min(POST_ROWS, y_ctx.shape[0], y_lat.shape[0])
    ctx_blocks = y_ctx.shape[0] // tm
    row = lambda c: pl.BlockSpec((tm, MIX_W), lambda i: (i, c))
    in_specs = [row(0)] * 6 + [row(Z_GLA // MIX_W + 3), row(Z_ML // MIX_W + 3), row(Z_GDG // MIX_W),
                               pl.BlockSpec((tm, MIX_W), lambda i: (jnp.minimum(i, ctx_blocks - 1), 0)),
                               pl.BlockSpec((tm, MIX_W), lambda i: (jnp.maximum(i - ctx_blocks, 0), 0)),
                               pl.BlockSpec((8, MIX_W), lambda i: (0, 0))]
    return pl.pallas_call(
        functools.partial(_post_kernel, ctx_blocks=ctx_blocks),
        grid=(rows // tm,),
        in_specs=in_specs,
        out_specs=pl.BlockSpec((N_BRANCH, tm, MIX_W), lambda i: (0, i, 0)),
        out_shape=jax.ShapeDtypeStruct((N_BRANCH, rows, MIX_W), BF16),
        compiler_params=_cparams(1),
    )(*outs, z, z, z, y_ctx, y_lat, norm_w)


def _merge_kernel(x_ref, g_ref, y_ref, zm_ref, wb_ref, wo_ref, o_ref, acc_ref, *, seqs, tm):
    n = pl.program_id(1)
    r = seqs.mod_row(pl.program_id(0) * tm)
    p = _sigmoid(zm_ref[...]) * jnp.dot(y_ref[...], wb_ref[...], preferred_element_type=F32)

    @pl.when(n == 0)
    def _():
        acc_ref[...] = p

    @pl.when(n > 0)
    def _():
        acc_ref[...] += p

    @pl.when(n == N_BRANCH - 1)
    def _():
        out = jnp.dot(acc_ref[...].astype(BF16), wo_ref[...], preferred_element_type=F32)
        o_ref[...] = x_ref[...] + g_ref[pl.ds(r, 1), :] * out


def _merge(x, mod, ybr, z, wb, wo, seqs, l, tm):
    rows = x.shape[0]
    return pl.pallas_call(
        functools.partial(_merge_kernel, seqs=seqs, tm=tm),
        grid=(rows // tm, N_BRANCH),
        in_specs=[pl.BlockSpec((tm, D_MODEL), lambda m, n: (m, 0), pipeline_mode=pl.Buffered(1)),
                  pl.BlockSpec((None, 8, D_MODEL), lambda m, n: (l, 0, 5)),
                  pl.BlockSpec((None, tm, MIX_W), lambda m, n: (n, m, 0)),
                  pl.BlockSpec((tm, D_MODEL), lambda m, n: (m, n)),
                  pl.BlockSpec((None, None, MIX_W, D_MODEL), lambda m, n: (l, n, 0, 0)),
                  pl.BlockSpec((None, D_MODEL, D_MODEL), lambda m, n: (l, 0, 0), pipeline_mode=pl.Buffered(1))],
        out_specs=pl.BlockSpec((tm, D_MODEL), lambda m, n: (m, 0)),
        out_shape=jax.ShapeDtypeStruct((rows, D_MODEL), F32),
        scratch_shapes=[pltpu.VMEM((tm, D_MODEL), F32)],
        compiler_params=_cparams(2),
    )(x, mod, ybr, z, wb, wo)


def _lane_row(vals, lane0):
    v = vals.reshape(-1).astype(F32)
    return jnp.zeros((1, LANE), F32).at[0, lane0:lane0 + v.shape[0]].set(v)


def _rope_tables(seqs):
    t = seqs.tl
    row = (np.arange(t) // GRID_W).astype(np.float32)
    col = (np.arange(t) % GRID_W).astype(np.float32)
    n_pairs = HEAD_DIM // 4
    inv = jnp.asarray(ROPE_THETA, F32) ** (-jnp.arange(n_pairs, dtype=F32) / n_pairs)
    ang = jnp.concatenate([jnp.asarray(row)[:, None] * inv, jnp.asarray(col)[:, None] * inv], axis=-1)
    cos = jnp.repeat(jnp.cos(ang), 2, axis=-1)
    sin = jnp.repeat(jnp.sin(ang), 2, axis=-1) * jnp.asarray(np.tile([-1.0, 1.0], HEAD_DIM // 2), F32)
    cos = jnp.concatenate([jnp.ones((seqs.ctx_rows, HEAD_DIM), F32)] + [cos] * seqs.nl, axis=0)
    sin = jnp.concatenate([jnp.zeros((seqs.ctx_rows, HEAD_DIM), F32)] + [sin] * seqs.nl, axis=0)
    return cos, sin


def _trunk(seqs, x, cvec, cache_k, cache_v, state_gla, state_c, state_n, state_m, state_gdn,
           w_ada, b_ada, w_ffn_in, w_ffn_out, w_in, gla_w2, gla_b2, gla_norm_w, ml_gate_b, ml_norm_w,
           gd_conv_w, gd_a_log, gd_dt_bias, gd_norm_w, q_norm_w, k_norm_w, w_branch, w_out, final_norm_w,
           tm_ffn=1024, tf=512, tm_in=2048, tm_pre=512, tm_mg=512):
    depth = w_in.shape[0]
    rows = seqs.rows
    tm_ffn, tm_in, tm_mg = min(tm_ffn, seqs.tl), min(tm_in, seqs.tl), min(tm_mg, seqs.tl)
    mod = _ada(cvec, w_ada, b_ada)
    w_t = jnp.swapaxes(w_in, 1, 2)
    wb = w_branch.astype(BF16)
    wo = w_out.astype(BF16)
    cos_t, sin_t = _rope_tables(seqs)
    fw = final_norm_w.reshape(1, D_MODEL)
    past = cache_k.shape[2]
    ck = cache_k.reshape(cache_k.shape[:2] + (past, KV_W))
    cv = cache_v.reshape(cache_v.shape[:2] + (past, KV_W))
    zpad = jnp.zeros((LANE - GLA_RANK, MIX_W), F32)
    ctx = []
    for l in range(depth):
        x = _ffn(x, mod, w_ffn_in, w_ffn_out, fw, seqs, l, 0, False, tm_ffn, tf)
        h, zs = _premix(x, mod, w_t, seqs, l, tm_pre)
        z = _inproj(h, w_t, l, tm_in)

        w2f = jnp.concatenate([gla_w2[l, 0], zpad], axis=0).astype(BF16)
        w2b = jnp.concatenate([zpad[:GLA_RANK], gla_w2[l, 1], zpad[:LANE - 2 * GLA_RANK]], axis=0).astype(BF16)
        zq = _gdn_prep(z, jnp.concatenate([gd_conv_w[l], jnp.zeros((8 - CONV_K, 3 * MIX_W), F32)], axis=0), seqs)
        ab_lanes = jnp.concatenate([gd_a_log[l], jnp.zeros((N_DIR, N_HEADS), F32)], axis=1)
        dt_lanes = jnp.concatenate([gd_dt_bias[l], jnp.zeros((N_DIR, N_HEADS), F32)], axis=1)
        gf, gb, mf, mb, df, db, st_gla, st_c, st_n, st_m, st_gd = _scans(
            z, zq, zs, seqs, l, w2f, w2b, gla_b2[l, 0:1], gla_b2[l, 1:2], _lane_row(ml_gate_b[l], L_IF),
            _lane_row(ab_lanes, L_AB), _lane_row(dt_lanes, L_AB),
            state_gla, state_c, state_n, state_m, state_gdn)

        qn, kn = _attn_prep(z, cos_t, sin_t, jnp.tile(q_norm_w[l], N_HEADS)[None, :],
                            jnp.tile(k_norm_w[l], N_KV)[None, :])
        y_ctx = _attend(qn, kn, z, 0, seqs.nc, seqs.tc, seqs.ctx_rows)
        y_lat = _attend(qn, kn, z, seqs.ctx_rows, seqs.nl, seqs.tl, seqs.nl * seqs.tl, cache=(ck, cv), l=l)

        norm_w = jnp.stack([jnp.tile(w[l], N_HEADS) for w in (gla_norm_w, ml_norm_w, gd_norm_w)]
                           + [jnp.zeros((MIX_W,), F32)] * 5, axis=0)
        ybr = _branch_post((gf, gb, mf, mb, df, db), z, y_ctx, y_lat, norm_w)
        x = _merge(x, mod, ybr, z, wb, wo, seqs, l, tm_mg)
        x = _ffn(x, mod, w_ffn_in, w_ffn_out, fw, seqs, l, 1, l == depth - 1, tm_ffn, tf)

        nc, tc = seqs.nc, seqs.tc
        ctx.append(dict(
            k=kn[:seqs.ctx_rows].reshape(nc, tc, N_KV, HEAD_DIM),
            v=z[:seqs.ctx_rows, Z_ATV:Z_ATV + KV_W].reshape(nc, tc, N_KV, HEAD_DIM),
            gla=st_gla, mc=st_c,
            mn=st_n.reshape(nc, N_DIR, N_HEADS, HEAD_DIM),
            mm=st_m[:, :, 0].reshape(nc, N_DIR, N_HEADS),
            gd=st_gd))
    return x, ctx


def kernel(x_prompt, x_sample, cache_k, cache_v, state_gla, state_mlstm_c, state_mlstm_n, state_mlstm_m,
           state_gdn, c, c_ctx, w_ada, b_ada, w_ffn_in, w_ffn_out, w_in, gla_w2, gla_b2, gla_norm_w,
           ml_gate_b, ml_norm_w, gd_conv_w, gd_a_log, gd_dt_bias, gd_norm_w, q_norm_w, k_norm_w,
           w_branch, w_out, final_norm_w):
    nc, tc, _ = x_prompt.shape
    nl, tl, _ = x_sample.shape
    seqs = _Seqs(nc, tc, nl, tl)
    x = jnp.concatenate([x_prompt.reshape(nc * tc, D_MODEL), x_sample.reshape(nl * tl, D_MODEL)], axis=0)
    cvec = jnp.concatenate([c_ctx[None, :], c, jnp.zeros((8 - 1 - nl, D_MODEL), F32)], axis=0)
    y, ctx = _trunk(seqs, x, cvec, cache_k, cache_v, state_gla, state_mlstm_c, state_mlstm_n, state_mlstm_m,
                    state_gdn, w_ada, b_ada, w_ffn_in, w_ffn_out, w_in, gla_w2, gla_b2, gla_norm_w,
                    ml_gate_b, ml_norm_w, gd_conv_w, gd_a_log, gd_dt_bias, gd_norm_w, q_norm_w, k_norm_w,
                    w_branch, w_out, final_norm_w)
    y_prompt = y[:nc * tc].reshape(nc, tc, D_MODEL)
    y_sample = y[nc * tc:].reshape(nl, tl, D_MODEL)
    stack = lambda name: jnp.stack([cx[name] for cx in ctx], axis=1)
    return (y_prompt, y_sample, stack('k'), stack('v'), stack('gla'), stack('mc'), stack('mn'),
            stack('mm'), stack('gd'))
```

```python
import functools
import math

import numpy as np
import jax
import jax.numpy as jnp
from jax import lax
from jax.experimental import pallas as pl
from jax.experimental.pallas import tpu as pltpu

F32 = jnp.float32
BF16 = jnp.bfloat16

D_MODEL = 2048
DEPTH = 4
GRID_W = 64
N_HEADS = 4
HEAD_DIM = 128
MIX_W = N_HEADS * HEAD_DIM
N_KV = 2
KV_W = N_KV * HEAD_DIM
GLA_RANK = 16
GLA_TAU = 16.0
GLA_FAST_MAX = 60.0
CHUNK = 64
SUB = 16
Q_BLOCK = 128
CONV_K = 5
D_FF = 5632
ROPE_THETA = 10000.0
N_BRANCH = 4
N_DIR = 2
N_MOD = 9
EPS = 1e-6
SCALE = HEAD_DIM ** -0.5

Z_MERGE = 0
Z_GLA = N_BRANCH * D_MODEL
Z_ML = Z_GLA + 4 * MIX_W
Z_GDQKV = Z_ML + 4 * MIX_W
Z_GDG = Z_GDQKV + 3 * MIX_W
Z_ATQ = Z_GDG + MIX_W
Z_ATK = Z_ATQ + MIX_W
Z_ATV = Z_ATK + KV_W
NZ = Z_ATV + KV_W
LANE = 128
SUBLANES = 8
L_LR = 0
L_IF = N_DIR * GLA_RANK
L_AB = L_IF + N_DIR * 2 * N_HEADS

W_LR = 4 * MIX_W
W_ML = W_LR + N_DIR * GLA_RANK
W_IF = W_ML + 4 * MIX_W
W_GD = W_IF + N_DIR * 2 * N_HEADS
W_AB = W_GD + 4 * MIX_W
W_AT = W_AB + N_DIR * 2 * N_HEADS
W_MG = W_AT + MIX_W + 2 * KV_W
IN_TILE = 512
TILE_SRC = ([W_MG + IN_TILE * j for j in range(N_BRANCH * D_MODEL // IN_TILE)]
            + [IN_TILE * j for j in range(4)] + [W_ML + IN_TILE * j for j in range(4)]
            + [W_GD + IN_TILE * j for j in range(4)] + [W_AT + IN_TILE * j for j in range(2)])

VMEM_LIMIT = 56 * 1024 * 1024
FFN_VMEM_LIMIT = 60 * 1024 * 1024


def _cparams(n_axes, vmem_limit=VMEM_LIMIT):
    return pltpu.CompilerParams(dimension_semantics=("arbitrary",) * n_axes,
                                vmem_limit_bytes=vmem_limit)


def _dot(a, b):
    return jnp.dot(a.astype(BF16), b.astype(BF16), preferred_element_type=F32)


def _dot_nt(a, b):
    return lax.dot_general(a.astype(BF16), b.astype(BF16), (((1,), (1,)), ((), ())),
                           preferred_element_type=F32)


def _dot_tn(a, b):
    return lax.dot_general(a.astype(BF16), b.astype(BF16), (((0,), (0,)), ((), ())),
                           preferred_element_type=F32)


def _split(a):
    hi = a.astype(BF16)
    return hi, (a - hi.astype(F32)).astype(BF16)


def _dot_split(a, b):
    ah, al = a
    bh, bl = b
    return (jnp.dot(ah, bh, preferred_element_type=F32) + jnp.dot(ah, bl, preferred_element_type=F32)
            + jnp.dot(al, bh, preferred_element_type=F32))


def _cumsum_mask(mask, x):
    m = mask.astype(BF16)
    x0 = x.astype(BF16)
    r1 = x - x0.astype(F32)
    x1 = r1.astype(BF16)
    x2 = (r1 - x1.astype(F32)).astype(BF16)
    return (jnp.dot(m, x0, preferred_element_type=F32) + jnp.dot(m, x1, preferred_element_type=F32)
            + jnp.dot(m, x2, preferred_element_type=F32))


def _sigmoid(x):
    return 1.0 / (1.0 + jnp.exp(-x))


def _silu(x):
    return x * _sigmoid(x)


def _softplus(x):
    return jnp.maximum(x, 0.0) + jnp.log1p(jnp.exp(-jnp.abs(x)))


def _log_sigmoid(x):
    return -_softplus(-x)


def _rmsmod(x, sc, sh):
    ms = jnp.mean(x * x, axis=-1, keepdims=True)
    return x * lax.rsqrt(ms + EPS) * (1.0 + sc) + sh


def _head_rms(x, w):
    outs = []
    for h in range(x.shape[1] // HEAD_DIM):
        seg = x[:, h * HEAD_DIM:(h + 1) * HEAD_DIM]
        ms = jnp.mean(seg * seg, axis=-1, keepdims=True)
        outs.append(seg * lax.rsqrt(ms + EPS))
    return jnp.concatenate(outs, axis=1) * w


class _Seqs:
    def __init__(self, nc, tc, nl, tl):
        self.nc, self.tc, self.nl, self.tl = nc, tc, nl, tl
        self.ctx_rows = nc * tc
        self.rows = nc * tc + nl * tl

    def mod_row(self, start):
        return jnp.where(start < self.ctx_rows, 0, 1 + (start - self.ctx_rows) // self.tl)

    def scan_tables(self):
        fwd, bwd, flg, sq = [], [], [], []
        base = 0
        for s in range(self.nc + self.nl):
            lat = s >= self.nc
            n = (self.tl if lat else self.tc) // CHUNK
            for j in range(n):
                fwd.append(base + j)
                bwd.append(base + n - 1 - j)
                flg.append((1 if j == 0 else 0) | (2 if j == n - 1 else 0) | (4 if lat else 0))
                sq.append(s)
            base += n
        return tuple(jnp.asarray(np.array(a, np.int32)) for a in (fwd, bwd, flg, sq))


def _ada_kernel(c_ref, w_ref, b_ref, o_ref):
    o_ref[...] = _dot(_silu(c_ref[...]), w_ref[...]) + b_ref[...]


def _ada(cvec, w_ada, b_ada):
    tn = 1024
    nmod = N_MOD * D_MODEL
    depth = w_ada.shape[0]
    return pl.pallas_call(
        _ada_kernel,
        grid=(depth, nmod // tn),
        in_specs=[pl.BlockSpec((8, D_MODEL), lambda l, j: (0, 0)),
                  pl.BlockSpec((None, D_MODEL, tn), lambda l, j: (l, 0, j)),
                  pl.BlockSpec((None, 1, tn), lambda l, j: (l, 0, j))],
        out_specs=pl.BlockSpec((None, 8, tn), lambda l, j: (l, 0, j)),
        out_shape=jax.ShapeDtypeStruct((depth, 8, nmod), F32),
        compiler_params=_cparams(2),
    )(cvec, w_ada, b_ada.reshape(depth, 1, nmod))


def _ffn_kernel(x_ref, sh_ref, sc_ref, g_ref, wg_ref, wu_ref, wo_ref, fw_ref, o_ref, h_ref,
                *, seqs, tm, nf, final):
    f = pl.program_id(1)
    r = seqs.mod_row(pl.program_id(0) * tm)

    @pl.when(f == 0)
    def _():
        h_ref[...] = _rmsmod(x_ref[...], sc_ref[pl.ds(r, 1), :], sh_ref[pl.ds(r, 1), :]).astype(BF16)
        o_ref[...] = jnp.zeros_like(o_ref)

    h = h_ref[...]
    g = jnp.dot(h, wg_ref[...].astype(BF16), preferred_element_type=F32)
    u = jnp.dot(h, wu_ref[...].astype(BF16), preferred_element_type=F32)
    a = (_silu(g) * u).astype(BF16)
    o_ref[...] += jnp.dot(a, wo_ref[...].astype(BF16), preferred_element_type=F32)

    @pl.when(f == nf - 1)
    def _():
        y = x_ref[...] + 0.5 * g_ref[pl.ds(r, 1), :] * o_ref[...]
        if final:
            ms = jnp.mean(y * y, axis=-1, keepdims=True)
            y = y * lax.rsqrt(ms + EPS) * fw_ref[...]
        o_ref[...] = y


def _ffn(x, mod, w_in, w_out, fw, seqs, l, i, final, tm, tf):
    rows = x.shape[0]
    nf = D_FF // tf
    j0 = 0 if i == 0 else 6
    kern = functools.partial(_ffn_kernel, seqs=seqs, tm=tm, nf=nf, final=final)
    modspec = lambda j: pl.BlockSpec((None, 8, D_MODEL), lambda m, f: (l, 0, j))
    return pl.pallas_call(
        kern,
        grid=(rows // tm, nf),
        in_specs=[pl.BlockSpec((tm, D_MODEL), lambda m, f: (m, 0), pipeline_mode=pl.Buffered(1)),
                  modspec(j0), modspec(j0 + 1), modspec(j0 + 2),
                  pl.BlockSpec((None, None, D_MODEL, tf), lambda m, f: (l, i, 0, f)),
                  pl.BlockSpec((None, None, D_MODEL, tf), lambda m, f: (l, i, 0, nf + f)),
                  pl.BlockSpec((None, None, tf, D_MODEL), lambda m, f: (l, i, f, 0)),
                  pl.BlockSpec((1, D_MODEL), lambda m, f: (0, 0))],
        out_specs=pl.BlockSpec((tm, D_MODEL), lambda m, f: (m, 0), pipeline_mode=pl.Buffered(1)),
        out_shape=jax.ShapeDtypeStruct((rows, D_MODEL), F32),
        scratch_shapes=[pltpu.VMEM((tm, D_MODEL), BF16)],
        compiler_params=_cparams(2, FFN_VMEM_LIMIT),
    )(x, mod, mod, mod, w_in, w_in, w_out, fw)


def _premix_kernel(x_ref, sh_ref, sc_ref, wlr_ref, wif_ref, wab_ref, h_ref, zs_ref, *, seqs, tm):
    r = seqs.mod_row(pl.program_id(0) * tm)
    h = _rmsmod(x_ref[...], sc_ref[pl.ds(r, 1), :], sh_ref[pl.ds(r, 1), :]).astype(BF16)
    h_ref[...] = h
    used = wlr_ref.shape[1] + wif_ref.shape[1] + wab_ref.shape[1]
    ws = jnp.concatenate([wlr_ref[0], wif_ref[0], wab_ref[0], jnp.zeros((LANE - used, D_MODEL), F32)],
                         axis=0).astype(BF16)
    zs_ref[...] = _dot_nt(h, ws)


def _premix(x, mod, w_t, seqs, l, tm):
    rows = x.shape[0]
    modspec = lambda j: pl.BlockSpec((None, 8, D_MODEL), lambda m: (l, 0, j))
    wspec = lambda start, n: pl.BlockSpec((pl.Element(1), pl.Element(n), pl.Element(D_MODEL)),
                                          lambda m: (l, start, 0))
    return pl.pallas_call(
        functools.partial(_premix_kernel, seqs=seqs, tm=tm),
        grid=(rows // tm,),
        in_specs=[pl.BlockSpec((tm, D_MODEL), lambda m: (m, 0)), modspec(3), modspec(4),
                  wspec(W_LR, W_ML - W_LR), wspec(W_IF, W_GD - W_IF), wspec(W_AB, W_AT - W_AB)],
        out_specs=(pl.BlockSpec((tm, D_MODEL), lambda m: (m, 0)), pl.BlockSpec((tm, LANE), lambda m: (m, 0))),
        out_shape=(jax.ShapeDtypeStruct((rows, D_MODEL), BF16), jax.ShapeDtypeStruct((rows, LANE), F32)),
        compiler_params=_cparams(1),
    )(x, mod, mod, w_t, w_t, w_t)


def _inproj_kernel(row_ref, h_ref, w_ref, o_ref, w_scr):
    @pl.when(pl.program_id(1) == 0)
    def _():
        w_scr[...] = w_ref[0].astype(BF16)

    o_ref[...] = lax.dot_general(h_ref[...], w_scr[...], (((1,), (1,)), ((), ())),
                                 preferred_element_type=F32)


def _inproj(h, w_t, l, tm):
    rows = h.shape[0]
    assert all(c % SUBLANES == 0 for c in TILE_SRC)
    src = jnp.asarray(np.array(TILE_SRC, np.int32) // SUBLANES)
    gs = pltpu.PrefetchScalarGridSpec(
        num_scalar_prefetch=1, grid=(len(TILE_SRC), rows // tm),
        in_specs=[pl.BlockSpec((tm, D_MODEL), lambda n, m, c: (m, 0)),
                  pl.BlockSpec((pl.Element(1), pl.Element(IN_TILE), pl.Element(D_MODEL)),
                               lambda n, m, c: (l, c[n] * SUBLANES, 0))],
        out_specs=pl.BlockSpec((tm, IN_TILE), lambda n, m, c: (m, n)),
        scratch_shapes=[pltpu.VMEM((IN_TILE, D_MODEL), BF16)])
    return pl.pallas_call(
        _inproj_kernel, grid_spec=gs,
        out_shape=jax.ShapeDtypeStruct((rows, NZ), F32),
        compiler_params=_cparams(2),
    )(src, h, w_t)


def _tri_masks(rev):
    ri = lax.broadcasted_iota(jnp.int32, (CHUNK, CHUNK), 0)
    ci = lax.broadcasted_iota(jnp.int32, (CHUNK, CHUNK), 1)
    if rev:
        return ri <= ci, ri < ci
    return ri >= ci, ri > ci


def _flags(flg):
    first = (flg & 1) != 0
    last = (flg & 2) != 0
    lat = (flg & 4) != 0
    return first, last, lat


def _hs(h):
    return slice(h * HEAD_DIM, (h + 1) * HEAD_DIM)


def _gla_init(lat, init_ref, st_scr):
    if lat:
        for d in range(N_DIR):
            for h in range(N_HEADS):
                st_scr[d, h] = init_ref[d, h].T
    else:
        st_scr[...] = jnp.zeros_like(st_scr)


def _gla_final(st_out, st_scr):
    for d in range(N_DIR):
        for h in range(N_HEADS):
            st_out[d, h] = st_scr[d, h].T


def _gla_pre(s_ref, w2, b2):
    return jnp.dot(s_ref[...].astype(BF16), w2[...], preferred_element_type=F32) + b2[...]


GLA_FAST_PRE_MIN = math.log(2.0) - GLA_FAST_MAX


def _gla_step(pres, fast, qf, kf, vf, qb, kb, vb, of_ref, ob_ref, st_scr, b_scr):
    dirs = ((qf, kf, vf, of_ref), (qb, kb, vb, ob_ref))
    for d, (q_ref, k_ref, v_ref, o_ref) in enumerate(dirs):
        rev = d == 1
        incl, _ = _tri_masks(rev)
        lg = _log_sigmoid(pres[d]) / GLA_TAU
        b = _cumsum_mask(incl, lg)
        if not fast:
            b_scr[d] = b
        bend = jnp.sum(lg, axis=0, keepdims=True)
        yield
        q = q_ref[...] * SCALE
        k = k_ref[...]
        v = v_ref[...]
        qd = q * jnp.exp(b)
        kd = k * jnp.exp(bend - b)
        eb = jnp.exp(bend)

        ri = lax.broadcasted_iota(jnp.int32, (CHUNK, CHUNK), 0)
        ci = lax.broadcasted_iota(jnp.int32, (CHUNK, CHUNK), 1)
        if fast:
            att_mask = (ri <= ci) if rev else (ri >= ci)
        elif rev:
            att_mask = ci >= (ri // SUB + 1) * SUB
        else:
            att_mask = ci < (ri // SUB) * SUB
        nsub = CHUNK // SUB
        att_rows = [[] for _ in range(N_HEADS)]
        for blk in range(nsub):
            r0 = blk * SUB
            has_edge = (blk < nsub - 1) if rev else (blk > 0)
            if not has_edge and not fast:
                for h in range(N_HEADS):
                    att_rows[h].append(jnp.zeros((SUB, CHUNK), F32))
                continue
            edge = r0 + SUB if rev else r0 - 1
            bref = b[edge:edge + 1, :] if has_edge else jnp.zeros((1, MIX_W), F32)
            qe = q[r0:r0 + SUB, :] * jnp.exp(b[r0:r0 + SUB, :] - bref)
            ke = k * jnp.exp(jnp.minimum(bref - b, GLA_FAST_MAX if fast else 0.0))
            for h in range(N_HEADS):
                att_rows[h].append(_dot_nt(qe[:, _hs(h)], ke[:, _hs(h)]))
            if fast:
                yield
        o_main = []
        for h in range(N_HEADS):
            att = jnp.where(att_mask, jnp.concatenate(att_rows[h], axis=0), 0.0)
            o_main.append(_dot_nt(qd[:, _hs(h)], st_scr[d, h]) + _dot(att, v[:, _hs(h)]))
        o_main = jnp.concatenate(o_main, axis=1)
        yield

        if fast:
            o_ref[...] = o_main
        else:
            rowi = lax.broadcasted_iota(jnp.int32, (SUBLANES, 1), 0)
            for r0 in range(0, CHUNK, SUBLANES):
                blk0 = (r0 // SUB) * SUB
                bb = b[r0:r0 + SUBLANES, :]
                qq = q[r0:r0 + SUBLANES, :]
                acc = None
                for s in range(blk0, blk0 + SUB):
                    if (s > r0 + SUBLANES - 1 and not rev) or (s < r0 and rev):
                        continue
                    bs = b_scr[d, s:s + 1, :]
                    ks = k_ref[s:s + 1, :]
                    vs = v_ref[s:s + 1, :]
                    p = qq * ks * jnp.exp(bb - bs)
                    valid = (rowi <= s - r0) if rev else (rowi >= s - r0)
                    parts = []
                    for h in range(N_HEADS):
                        c = jnp.sum(p[:, _hs(h)], axis=-1, keepdims=True)
                        parts.append(jnp.where(valid, c, 0.0) * vs[:, _hs(h)])
                    term = jnp.concatenate(parts, axis=1)
                    acc = term if acc is None else acc + term
                    if (s - blk0) % 4 == 3:
                        yield
                o_ref[r0:r0 + SUBLANES, :] = o_main[r0:r0 + SUBLANES, :] + acc
                yield

        for h in range(N_HEADS):
            st_scr[d, h] = st_scr[d, h] * eb[:, _hs(h)] + _dot_tn(v[:, _hs(h)], kd[:, _hs(h)])
        yield


def _copy_or_zero(lat, pairs):
    for src, dst in pairs:
        dst[...] = src[...] if lat else jnp.zeros_like(dst)


def _mlstm_step(qf, kf, vf, sf, qb, kb, vb, sb, bias_ref, of_ref, ob_ref, c_scr, n_scr, m_scr):
    dirs = ((qf, kf, vf, sf, of_ref), (qb, kb, vb, sb, ob_ref))
    for d, (q_ref, k_ref, v_ref, s_ref, o_ref) in enumerate(dirs):
        rev = d == 1
        incl, _ = _tri_masks(rev)
        last_row = 0 if rev else CHUNK - 1
        gates = s_ref[...] + bias_ref[...]
        lf_all = _log_sigmoid(gates)
        f_all = _cumsum_mask(incl, lf_all)
        f_all_t = f_all.T
        gates_t = gates.T
        q = q_ref[...]
        k = k_ref[...] * SCALE
        v = v_ref[...]
        for h in range(N_HEADS):
            ji = L_IF + d * 2 * N_HEADS + h
            jf = ji + N_HEADS
            r = d * N_HEADS + h
            fc = f_all[:, jf:jf + 1]
            frow = f_all_t[jf:jf + 1, :]
            igc = gates[:, ji:ji + 1]
            igrow = gates_t[ji:ji + 1, :]
            m_prev = m_scr[r:r + 1, 0:1]
            n_prev = n_scr[r:r + 1, :]
            c_prev = c_scr[d, h]
            qh, kh, vh = q[:, _hs(h)], k[:, _hs(h)], v[:, _hs(h)]
            log_d = jnp.where(incl, fc - frow + igrow, -jnp.inf)
            inter = fc + m_prev
            m_t = jnp.maximum(inter, jnp.max(log_d, axis=-1, keepdims=True))
            dmat = jnp.exp(log_d - m_t)
            a_in = jnp.exp(inter - m_t)
            s = _dot_nt(qh, kh) * dmat
            num = a_in * _dot(qh, c_prev) + _dot(s, vh)
            den = a_in * jnp.sum(qh * n_prev, axis=-1, keepdims=True) + jnp.sum(s, axis=-1, keepdims=True)
            o_ref[:, _hs(h)] = num / jnp.maximum(jnp.abs(den), jnp.exp(-m_t))
            m_new = m_t[last_row:last_row + 1, :]
            f_end = fc[last_row:last_row + 1, :]
            w_end = jnp.exp(f_end - fc + igc - m_new)
            a_end = jnp.exp(f_end + m_prev - m_new)
            kw = kh * w_end
            c_scr[d, h] = a_end * c_prev + _dot_tn(kw, vh)
            n_scr[r:r + 1, :] = a_end * n_prev + jnp.sum(kw, axis=0, keepdims=True)
            m_scr[r:r + 1, :] = jnp.broadcast_to(m_new, (1, LANE))
            yield


def _unit_tri_inverse_all(ns):
    ri = lax.broadcasted_iota(jnp.int32, (CHUNK, CHUNK), 0)
    ci = lax.broadcasted_iota(jnp.int32, (CHUNK, CHUNK), 1)
    same16 = (ri // SUB) == (ci // SUB)
    same32 = (ri // (2 * SUB)) == (ci // (2 * SUB))
    eye = (ri == ci).astype(F32)
    nd = [jnp.where(same16, n, 0.0) for n in ns]
    n1 = [_split(jnp.where(jnp.logical_and(same32, jnp.logical_not(same16)), n, 0.0)) for n in ns]
    n2 = [_split(jnp.where(same32, 0.0, n)) for n in ns]
    t = [eye - x for x in nd]
    nds = [_split(x) for x in nd]
    p = [_dot_split(x, x) for x in nds]
    yield
    for level in range(3):
        ps = [_split(x) for x in p]
        t = [x + _dot_split(_split(x), y) for x, y in zip(t, ps)]
        if level < 2:
            p = [_dot_split(y, y) for y in ps]
        yield
    for nn in (n1, n2):
        ts = [_split(x) for x in t]
        a = [_dot_split(y, x) for x, y in zip(ts, nn)]
        yield
        t = [x - _dot_split(xs, _split(y)) for x, xs, y in zip(t, ts, a)]
        yield
    return t


def _gdn_step(qf, kf, vf, sf, qb, kb, vb, sb, alog_ref, dtb_ref, of_ref, ob_ref, s_scr):
    units = []
    dirs = ((qf, kf, vf, sf, of_ref), (qb, kb, vb, sb, ob_ref))
    for d, (q_ref, k_ref, v_ref, s_ref, o_ref) in enumerate(dirs):
        rev = d == 1
        incl, strict = _tri_masks(rev)
        last_row = 0 if rev else CHUNK - 1
        small = s_ref[...]
        g_all = -jnp.exp(alog_ref[...]) * _softplus(small + dtb_ref[...])
        beta_all = _sigmoid(small)
        gam_all = _cumsum_mask(incl, g_all)
        gam_t = gam_all.T
        q = q_ref[...]
        k = k_ref[...]
        v = v_ref[...]
        for h in range(N_HEADS):
            jg = L_AB + d * 2 * N_HEADS + h
            jb = jg + N_HEADS
            gc = gam_all[:, jg:jg + 1]
            units.append(dict(
                d=d, h=h, o_ref=o_ref, incl=incl, strict=strict, gc=gc, grow=gam_t[jg:jg + 1, :],
                beta=beta_all[:, jb:jb + 1], g_end=gc[last_row:last_row + 1, :],
                q=q[:, _hs(h)], k=k[:, _hs(h)], v=v[:, _hs(h)]))

    for u in units:
        u['decay'] = jnp.exp(jnp.where(u['incl'], u['gc'] - u['grow'], -jnp.inf))
    kk = [_dot_nt(u['k'], u['k']) for u in units]
    qk = [_dot_nt(u['q'], u['k']) for u in units]
    ns = [jnp.where(u['strict'], u['beta'] * x * u['decay'], 0.0) for u, x in zip(units, kk)]
    yield
    ts = yield from _unit_tri_inverse_all(ns)
    rhs = [_split(jnp.concatenate([u['v'] * u['beta'], u['k'] * (u['beta'] * jnp.exp(u['gc']))], axis=1))
           for u in units]
    sol = [_dot_split(_split(t), r) for t, r in zip(ts, rhs)]
    yield
    s_prev = [s_scr[u['d'], u['h']] for u in units]
    w_new = [x[:, :HEAD_DIM] - _dot(x[:, HEAD_DIM:], sp) for x, sp in zip(sol, s_prev)]
    yield
    for u, x, w, sp in zip(units, qk, w_new, s_prev):
        u['o_ref'][:, _hs(u['h'])] = _dot(u['q'] * jnp.exp(u['gc']), sp) + _dot(x * u['decay'], w)
    for u, w, sp in zip(units, w_new, s_prev):
        s_scr[u['d'], u['h']] = (jnp.exp(u['g_end']) * sp
                                 + _dot_tn(u['k'] * jnp.exp(u['g_end'] - u['gc']), w))


N_UNITS = N_DIR * N_HEADS
_DONE = object()
ST_SHAPE = (N_DIR, N_HEADS, HEAD_DIM, HEAD_DIM)


def _scan_kernel(fwd_ref, bwd_ref, flg_ref, sq_ref,
                 g_f, g_b, m_f, m_b, d_f, d_b, sf, sb,
                 w2f, w2b, b2f, b2b, mbias, alog, dtb,
                 g0, c0, n0, m0, s0,
                 g_of, g_ob, m_of, m_ob, d_of, d_ob, g_out, c_out, n_out, m_out, s_out,
                 g_scr, b_scr, c_scr, n_scr, m_scr, s_scr):
    first, last, lat = _flags(flg_ref[pl.program_id(0)])
    carried = ((c0, c_scr), (n0, n_scr), (m0, m_scr), (s0, s_scr))
    qkv = lambda ref: tuple(ref.at[:, j * MIX_W:(j + 1) * MIX_W] for j in range(3))
    (gqf, gkf, gvf), (gqb, gkb, gvb) = qkv(g_f), qkv(g_b)
    (mqf, mkf, mvf), (mqb, mkb, mvb) = qkv(m_f), qkv(m_b)
    (dqf, dkf, dvf), (dqb, dkb, dvb) = qkv(d_f), qkv(d_b)

    @pl.when(jnp.logical_and(first, jnp.logical_not(lat)))
    def _():
        _gla_init(False, g0, g_scr)
        _copy_or_zero(False, carried)

    @pl.when(jnp.logical_and(first, lat))
    def _():
        _gla_init(True, g0, g_scr)
        _copy_or_zero(True, carried)

    pres = (_gla_pre(sf, w2f, b2f), _gla_pre(sb, w2b, b2b))
    fast = jnp.minimum(jnp.min(pres[0]), jnp.min(pres[1])) > GLA_FAST_PRE_MIN

    def run(gla_fast):
        stages = [_gdn_step(dqf, dkf, dvf, sf, dqb, dkb, dvb, sb, alog, dtb, d_of, d_ob, s_scr),
                  _mlstm_step(mqf, mkf, mvf, sf, mqb, mkb, mvb, sb, mbias, m_of, m_ob, c_scr, n_scr, m_scr),
                  _gla_step(pres, gla_fast, gqf, gkf, gvf, gqb, gkb, gvb, g_of, g_ob, g_scr, b_scr)]
        strides = {stages[0]: 1, stages[1]: 1, stages[2]: 1 if gla_fast else 4}
        while stages:
            for g in list(stages):
                for _ in range(strides[g]):
                    if g in stages and next(g, _DONE) is _DONE:
                        stages.remove(g)

    @pl.when(fast)
    def _():
        run(True)

    @pl.when(jnp.logical_not(fast))
    def _():
        run(False)

    @pl.when(jnp.logical_and(last, jnp.logical_not(lat)))
    def _():
        _gla_final(g_out, g_scr)
        for src, dst in ((c_scr, c_out), (n_scr, n_out), (m_scr, m_out), (s_scr, s_out)):
            dst[...] = src[...]


def _scans(z, zq, zs, seqs, l, w2f, w2b, b2f, b2b, mbias, alog_row, dtb_row,
           state_gla, state_c, state_n, state_m, state_gdn):
    rows = z.shape[0]
    tabs = seqs.scan_tables()
    nsteps = int(tabs[0].shape[0])

    def blk(which, col, width):
        if which == 0:
            return pl.BlockSpec((CHUNK, width), lambda i, fw, bw, fl, sq: (fw[i], col))
        return pl.BlockSpec((CHUNK, width), lambda i, fw, bw, fl, sq: (bw[i], col))

    def const(shape):
        nd = len(shape)
        return pl.BlockSpec(shape, lambda i, fw, bw, fl, sq: (0,) * nd)

    lat_i = lambda sq, i: jnp.maximum(sq[i] - seqs.nc, 0)
    ctx_i = lambda sq, i: jnp.minimum(sq[i], seqs.nc - 1)
    in_specs, args = [], []
    def qkv_blk(which, col):
        shape = (pl.Element(CHUNK), pl.Element(3 * MIX_W))
        if which == 0:
            return pl.BlockSpec(shape, lambda i, fw, bw, fl, sq: (fw[i] * CHUNK, col))
        return pl.BlockSpec(shape, lambda i, fw, bw, fl, sq: (bw[i] * CHUNK, col))

    for src, col in ((z, Z_GLA), (z, Z_ML), (zq, 0)):
        in_specs += [qkv_blk(0, col), qkv_blk(1, col)]
        args += [src] * 2
    in_specs += [blk(0, 0, LANE), blk(1, 0, LANE)]
    args += [zs, zs]
    in_specs += [const((LANE, MIX_W)), const((LANE, MIX_W)), const((1, MIX_W)), const((1, MIX_W)),
                 const((1, LANE)), const((1, LANE)), const((1, LANE))]
    args += [w2f, w2b, b2f, b2b, mbias, alog_row, dtb_row]
    st_in = pl.BlockSpec((None, None) + ST_SHAPE, lambda i, fw, bw, fl, sq: (lat_i(sq, i), l, 0, 0, 0, 0))
    vec_in = pl.BlockSpec((None, None, N_UNITS, LANE), lambda i, fw, bw, fl, sq: (lat_i(sq, i), l, 0, 0))
    depth = state_c.shape[1]
    n0 = state_n.reshape(seqs.nl, depth, N_UNITS, HEAD_DIM)
    m0 = jnp.broadcast_to(state_m.reshape(seqs.nl, depth, N_UNITS, 1), (seqs.nl, depth, N_UNITS, LANE))
    in_specs += [st_in, st_in, vec_in, vec_in, st_in]
    args += [state_gla, state_c, n0, m0, state_gdn]

    o_f = pl.BlockSpec((CHUNK, MIX_W), lambda i, fw, bw, fl, sq: (fw[i], 0))
    o_b = pl.BlockSpec((CHUNK, MIX_W), lambda i, fw, bw, fl, sq: (bw[i], 0))
    st_out = pl.BlockSpec((None,) + ST_SHAPE, lambda i, fw, bw, fl, sq: (ctx_i(sq, i), 0, 0, 0, 0))
    vec_out = pl.BlockSpec((None, N_UNITS, LANE), lambda i, fw, bw, fl, sq: (ctx_i(sq, i), 0, 0))
    o_sds = jax.ShapeDtypeStruct((rows, MIX_W), F32)
    st_sds = jax.ShapeDtypeStruct((seqs.nc,) + ST_SHAPE, F32)
    vec_sds = jax.ShapeDtypeStruct((seqs.nc, N_UNITS, LANE), F32)
    gs = pltpu.PrefetchScalarGridSpec(
        num_scalar_prefetch=4, grid=(nsteps,), in_specs=in_specs,
        out_specs=(o_f, o_b, o_f, o_b, o_f, o_b, st_out, st_out, vec_out, vec_out, st_out),
        scratch_shapes=[pltpu.VMEM(ST_SHAPE, F32), pltpu.VMEM((N_DIR, CHUNK, MIX_W), F32),
                        pltpu.VMEM(ST_SHAPE, F32), pltpu.VMEM((N_UNITS, LANE), F32),
                        pltpu.VMEM((N_UNITS, LANE), F32), pltpu.VMEM(ST_SHAPE, F32)])
    return pl.pallas_call(
        _scan_kernel, grid_spec=gs,
        out_shape=(o_sds,) * 6 + (st_sds, st_sds, vec_sds, vec_sds, st_sds),
        compiler_params=_cparams(1),
    )(*tabs, *args)


CONV_ROWS = 256
HALO = 8


def _conv_kernel(x_ref, prev_ref, next_ref, w_ref, o_ref, xe_ref, *, seqs):
    i = pl.program_id(0)
    start = i * CONV_ROWS
    in_lat = start >= seqs.ctx_rows
    off = jnp.where(in_lat, (start - seqs.ctx_rows) % seqs.tl, start % seqs.tc)
    seq_len = jnp.where(in_lat, seqs.tl, seqs.tc)
    xe_ref[0:HALO, :] = jnp.where(off > 0, prev_ref[...], 0.0)
    xe_ref[HALO:HALO + CONV_ROWS, :] = x_ref[...]
    xe_ref[HALO + CONV_ROWS:, :] = jnp.where(off + CONV_ROWS < seq_len, next_ref[...], 0.0)
    pad = CONV_K // 2
    y = jnp.zeros((CONV_ROWS, 3 * MIX_W), F32)
    for j in range(CONV_K):
        y = y + xe_ref[pl.ds(HALO - pad + j, CONV_ROWS), :] * w_ref[j:j + 1, :]
    y = _silu(y)
    outs = []
    for h in range(3 * N_HEADS):
        seg = y[:, _hs(h)]
        if h < 2 * N_HEADS:
            seg = seg * lax.rsqrt(jnp.sum(seg * seg, axis=-1, keepdims=True) + EPS)
            if h < N_HEADS:
                seg = seg * SCALE
        outs.append(seg)
    o_ref[...] = jnp.concatenate(outs, axis=1)


def _gdn_prep(z, conv_w, seqs):
    rows = z.shape[0]
    cw = 3 * MIX_W
    cb = Z_GDQKV // cw
    nb = rows // CONV_ROWS
    per = CONV_ROWS // HALO
    last8 = rows // HALO - 1
    return pl.pallas_call(
        functools.partial(_conv_kernel, seqs=seqs),
        grid=(nb,),
        in_specs=[pl.BlockSpec((CONV_ROWS, cw), lambda i: (i, cb)),
                  pl.BlockSpec((HALO, cw), lambda i: (jnp.maximum(i * per - 1, 0), cb)),
                  pl.BlockSpec((HALO, cw), lambda i: (jnp.minimum((i + 1) * per, last8), cb)),
                  pl.BlockSpec((8, cw), lambda i: (0, 0))],
        out_specs=pl.BlockSpec((CONV_ROWS, cw), lambda i: (i, 0)),
        out_shape=jax.ShapeDtypeStruct((rows, cw), F32),
        scratch_shapes=[pltpu.VMEM((CONV_ROWS + 2 * HALO, cw), F32)],
        compiler_params=_cparams(1),
    )(z, z, z, conv_w)


ATT_ROWS = 256


def _rope(y, cos, sin_signed):
    n = y.shape[1]
    lane = lax.broadcasted_iota(jnp.int32, y.shape, 1)
    partner = jnp.where(lane % 2 == 0, pltpu.roll(y, n - 1, axis=1), pltpu.roll(y, 1, axis=1))
    reps = n // HEAD_DIM
    c = jnp.concatenate([cos] * reps, axis=1)
    s = jnp.concatenate([sin_signed] * reps, axis=1)
    return y * c + partner * s


def _attn_prep_kernel(q_ref, k_ref, cos_ref, sin_ref, qw_ref, kw_ref, qo_ref, ko_ref):
    cos = cos_ref[...]
    sin = sin_ref[...]
    qo_ref[...] = _rope(_head_rms(q_ref[...], qw_ref[...]), cos, sin)
    ko_ref[...] = _rope(_head_rms(k_ref[...], kw_ref[...]), cos, sin)


def _attn_prep(z, cos_t, sin_t, qw, kw):
    rows = z.shape[0]
    return pl.pallas_call(
        _attn_prep_kernel,
        grid=(rows // ATT_ROWS,),
        in_specs=[pl.BlockSpec((ATT_ROWS, MIX_W), lambda i: (i, Z_ATQ // MIX_W)),
                  pl.BlockSpec((ATT_ROWS, KV_W), lambda i: (i, Z_ATK // KV_W)),
                  pl.BlockSpec((ATT_ROWS, HEAD_DIM), lambda i: (i, 0)),
                  pl.BlockSpec((ATT_ROWS, HEAD_DIM), lambda i: (i, 0)),
                  pl.BlockSpec((1, MIX_W), lambda i: (0, 0)),
                  pl.BlockSpec((1, KV_W), lambda i: (0, 0))],
        out_specs=(pl.BlockSpec((ATT_ROWS, MIX_W), lambda i: (i, 0)),
                   pl.BlockSpec((ATT_ROWS, KV_W), lambda i: (i, 0))),
        out_shape=(jax.ShapeDtypeStruct((rows, MIX_W), F32), jax.ShapeDtypeStruct((rows, KV_W), F32)),
        compiler_params=_cparams(1),
    )(z, z, cos_t, sin_t, qw, kw)


def _attn_kernel(*refs, has_cache):
    if has_cache:
        q_ref, k_ref, v_ref, ck_ref, cv_ref, o_ref = refs
    else:
        q_ref, k_ref, v_ref, o_ref = refs
    k = k_ref[...]
    v = v_ref[...]
    for g in range(N_HEADS // N_KV):
        qh = q_ref[:, _hs(g)]
        s = _dot_nt(qh, k) * SCALE
        m = jnp.max(s, axis=-1, keepdims=True)
        if has_cache:
            sc = _dot_nt(qh, ck_ref[...]) * SCALE
            m = jnp.maximum(m, jnp.max(sc, axis=-1, keepdims=True))
            pc = jnp.exp(sc - m)
        p = jnp.exp(s - m)
        den = jnp.sum(p, axis=-1, keepdims=True)
        if has_cache:
            den = den + jnp.sum(pc, axis=-1, keepdims=True)
        inv = 1.0 / den
        o = _dot(p * inv, v)
        if has_cache:
            o = o + _dot(pc * inv, cv_ref[...])
        o_ref[:, _hs(g)] = o


def _attend(qn, kn, z, row0, nseq, t, y_rows, cache=None, l=0):
    gw = (N_HEADS // N_KV) * HEAD_DIM
    nqb = t // Q_BLOCK
    rb0 = row0 // Q_BLOCK
    sb0 = row0 // t
    vcol = Z_ATV // HEAD_DIM
    in_specs = [pl.BlockSpec((Q_BLOCK, gw), lambda b, kv, qi: (rb0 + b * nqb + qi, kv)),
                pl.BlockSpec((t, HEAD_DIM), lambda b, kv, qi: (sb0 + b, kv)),
                pl.BlockSpec((t, HEAD_DIM), lambda b, kv, qi: (sb0 + b, vcol + kv))]
    args = [qn, kn, z]
    if cache is not None:
        ck, cv = cache
        past = ck.shape[2]
        cspec = pl.BlockSpec((None, None, past, HEAD_DIM), lambda b, kv, qi: (b, l, 0, kv))
        in_specs += [cspec, cspec]
        args += [ck, cv]
    return pl.pallas_call(
        functools.partial(_attn_kernel, has_cache=cache is not None),
        grid=(nseq, N_KV, nqb),
        in_specs=in_specs,
        out_specs=pl.BlockSpec((Q_BLOCK, gw), lambda b, kv, qi: (b * nqb + qi, kv)),
        out_shape=jax.ShapeDtypeStruct((y_rows, MIX_W), F32),
        compiler_params=_cparams(3),
    )(*args)


POST_ROWS = 512


def _post_kernel(gf, gb, mf, mb, df, db, zg, zo, zd, atc, atl, nw_ref, o_ref, *, ctx_blocks):
    nw = nw_ref[...]
    o_ref[0] = (_head_rms(gf[...] + gb[...], nw[0:1, :]) * _silu(zg[...])).astype(BF16)
    o_ref[1] = (_head_rms(mf[...] + mb[...], nw[1:2, :]) * _sigmoid(zo[...])).astype(BF16)
    o_ref[2] = (_head_rms(df[...] + db[...], nw[2:3, :]) * _silu(zd[...])).astype(BF16)
    at = jnp.where(pl.program_id(0) < ctx_blocks, atc[...], atl[...])
    o_ref[3] = at.astype(BF16)


def _branch_post(outs, z, y_ctx, y_lat, norm_w):
    rows = z.shape[0]
    tm = min(POST_ROWS, y_ctx.shape[0], y_lat.shape[0])
    ctx_blocks = y_ctx.shape[0] // tm
    row = lambda c: pl.BlockSpec((tm, MIX_W), lambda i: (i, c))
    in_specs = [row(0)] * 6 + [row(Z_GLA // MIX_W + 3), row(Z_ML // MIX_W + 3), row(Z_GDG // MIX_W),
                               pl.BlockSpec((tm, MIX_W), lambda i: (jnp.minimum(i, ctx_blocks - 1), 0)),
                               pl.BlockSpec((tm, MIX_W), lambda i: (jnp.maximum(i - ctx_blocks, 0), 0)),
                               pl.BlockSpec((8, MIX_W), lambda i: (0, 0))]
    return pl.pallas_call(
        functools.partial(_post_kernel, ctx_blocks=ctx_blocks),
        grid=(rows // tm,),
        in_specs=in_specs,
        out_specs=pl.BlockSpec((N_BRANCH, tm, MIX_W), lambda i: (0, i, 0)),
        out_shape=jax.ShapeDtypeStruct((N_BRANCH, rows, MIX_W), BF16),
        compiler_params=_cparams(1),
    )(*outs, z, z, z, y_ctx, y_lat, norm_w)


def _merge_kernel(x_ref, g_ref, y_ref, zm_ref, wb_ref, wo_ref, o_ref, acc_ref, *, seqs, tm):
    n = pl.program_id(1)
    r = seqs.mod_row(pl.program_id(0) * tm)
    p = _sigmoid(zm_ref[...]) * jnp.dot(y_ref[...], wb_ref[...], preferred_element_type=F32)

    @pl.when(n == 0)
    def _():
        acc_ref[...] = p

    @pl.when(n > 0)
    def _():
        acc_ref[...] += p

    @pl.when(n == N_BRANCH - 1)
    def _():
        out = jnp.dot(acc_ref[...].astype(BF16), wo_ref[...], preferred_element_type=F32)
        o_ref[...] = x_ref[...] + g_ref[pl.ds(r, 1), :] * out


def _merge(x, mod, ybr, z, wb, wo, seqs, l, tm):
    rows = x.shape[0]
    return pl.pallas_call(
        functools.partial(_merge_kernel, seqs=seqs, tm=tm),
        grid=(rows // tm, N_BRANCH),
        in_specs=[pl.BlockSpec((tm, D_MODEL), lambda m, n: (m, 0), pipeline_mode=pl.Buffered(1)),
                  pl.BlockSpec((None, 8, D_MODEL), lambda m, n: (l, 0, 5)),
                  pl.BlockSpec((None, tm, MIX_W), lambda m, n: (n, m, 0)),
                  pl.BlockSpec((tm, D_MODEL), lambda m, n: (m, n)),
                  pl.BlockSpec((None, None, MIX_W, D_MODEL), lambda m, n: (l, n, 0, 0)),
                  pl.BlockSpec((None, D_MODEL, D_MODEL), lambda m, n: (l, 0, 0), pipeline_mode=pl.Buffered(1))],
        out_specs=pl.BlockSpec((tm, D_MODEL), lambda m, n: (m, 0)),
        out_shape=jax.ShapeDtypeStruct((rows, D_MODEL), F32),
        scratch_shapes=[pltpu.VMEM((tm, D_MODEL), F32)],
        compiler_params=_cparams(2),
    )(x, mod, ybr, z, wb, wo)


def _lane_row(vals, lane0):
    v = vals.reshape(-1).astype(F32)
    return jnp.zeros((1, LANE), F32).at[0, lane0:lane0 + v.shape[0]].set(v)


def _rope_tables(seqs):
    t = seqs.tl
    row = (np.arange(t) // GRID_W).astype(np.float32)
    col = (np.arange(t) % GRID_W).astype(np.float32)
    n_pairs = HEAD_DIM // 4
    inv = jnp.asarray(ROPE_THETA, F32) ** (-jnp.arange(n_pairs, dtype=F32) / n_pairs)
    ang = jnp.concatenate([jnp.asarray(row)[:, None] * inv, jnp.asarray(col)[:, None] * inv], axis=-1)
    cos = jnp.repeat(jnp.cos(ang), 2, axis=-1)
    sin = jnp.repeat(jnp.sin(ang), 2, axis=-1) * jnp.asarray(np.tile([-1.0, 1.0], HEAD_DIM // 2), F32)
    cos = jnp.concatenate([jnp.ones((seqs.ctx_rows, HEAD_DIM), F32)] + [cos] * seqs.nl, axis=0)
    sin = jnp.concatenate([jnp.zeros((seqs.ctx_rows, HEAD_DIM), F32)] + [sin] * seqs.nl, axis=0)
    return cos, sin


def _trunk(seqs, x, cvec, cache_k, cache_v, state_gla, state_c, state_n, state_m, state_gdn,
           w_ada, b_ada, w_ffn_in, w_ffn_out, w_in, gla_w2, gla_b2, gla_norm_w, ml_gate_b, ml_norm_w,
           gd_conv_w, gd_a_log, gd_dt_bias, gd_norm_w, q_norm_w, k_norm_w, w_branch, w_out, final_norm_w,
           tm_ffn=1024, tf=512, tm_in=2048, tm_pre=512, tm_mg=512):
    depth = w_in.shape[0]
    rows = seqs.rows
    tm_ffn, tm_in, tm_mg = min(tm_ffn, seqs.tl), min(tm_in, seqs.tl), min(tm_mg, seqs.tl)
    mod = _ada(cvec, w_ada, b_ada)
    w_t = jnp.swapaxes(w_in, 1, 2)
    wb = w_branch.astype(BF16)
    wo = w_out.astype(BF16)
    cos_t, sin_t = _rope_tables(seqs)
    fw = final_norm_w.reshape(1, D_MODEL)
    past = cache_k.shape[2]
    ck = cache_k.reshape(cache_k.shape[:2] + (past, KV_W))
    cv = cache_v.reshape(cache_v.shape[:2] + (past, KV_W))
    zpad = jnp.zeros((LANE - GLA_RANK, MIX_W), F32)
    ctx = []
    for l in range(depth):
        x = _ffn(x, mod, w_ffn_in, w_ffn_out, fw, seqs, l, 0, False, tm_ffn, tf)
        h, zs = _premix(x, mod, w_t, seqs, l, tm_pre)
        z = _inproj(h, w_t, l, tm_in)

        w2f = jnp.concatenate([gla_w2[l, 0], zpad], axis=0).astype(BF16)
        w2b = jnp.concatenate([zpad[:GLA_RANK], gla_w2[l, 1], zpad[:LANE - 2 * GLA_RANK]], axis=0).astype(BF16)
        zq = _gdn_prep(z, jnp.concatenate([gd_conv_w[l], jnp.zeros((8 - CONV_K, 3 * MIX_W), F32)], axis=0), seqs)
        ab_lanes = jnp.concatenate([gd_a_log[l], jnp.zeros((N_DIR, N_HEADS), F32)], axis=1)
        dt_lanes = jnp.concatenate([gd_dt_bias[l], jnp.zeros((N_DIR, N_HEADS), F32)], axis=1)
        gf, gb, mf, mb, df, db, st_gla, st_c, st_n, st_m, st_gd = _scans(
            z, zq, zs, seqs, l, w2f, w2b, gla_b2[l, 0:1], gla_b2[l, 1:2], _lane_row(ml_gate_b[l], L_IF),
            _lane_row(ab_lanes, L_AB), _lane_row(dt_lanes, L_AB),
            state_gla, state_c, state_n, state_m, state_gdn)

        qn, kn = _attn_prep(z, cos_t, sin_t, jnp.tile(q_norm_w[l], N_HEADS)[None, :],
                            jnp.tile(k_norm_w[l], N_KV)[None, :])
        y_ctx = _attend(qn, kn, z, 0, seqs.nc, seqs.tc, seqs.ctx_rows)
        y_lat = _attend(qn, kn, z, seqs.ctx_rows, seqs.nl, seqs.tl, seqs.nl * seqs.tl, cache=(ck, cv), l=l)

        norm_w = jnp.stack([jnp.tile(w[l], N_HEADS) for w in (gla_norm_w, ml_norm_w, gd_norm_w)]
                           + [jnp.zeros((MIX_W,), F32)] * 5, axis=0)
        ybr = _branch_post((gf, gb, mf, mb, df, db), z, y_ctx, y_lat, norm_w)
        x = _merge(x, mod, ybr, z, wb, wo, seqs, l, tm_mg)
        x = _ffn(x, mod, w_ffn_in, w_ffn_out, fw, seqs, l, 1, l == depth - 1, tm_ffn, tf)

        nc, tc = seqs.nc, seqs.tc
        ctx.append(dict(
            k=kn[:seqs.ctx_rows].reshape(nc, tc, N_KV, HEAD_DIM),
            v=z[:seqs.ctx_rows, Z_ATV:Z_ATV + KV_W].reshape(nc, tc, N_KV, HEAD_DIM),
            gla=st_gla, mc=st_c,
            mn=st_n.reshape(nc, N_DIR, N_HEADS, HEAD_DIM),
            mm=st_m[:, :, 0].reshape(nc, N_DIR, N_HEADS),
            gd=st_gd))
    return x, ctx


def kernel(x_prompt, x_sample, cache_k, cache_v, state_gla, state_mlstm_c, state_mlstm_n, state_mlstm_m,
           state_gdn, c, c_ctx, w_ada, b_ada, w_ffn_in, w_ffn_out, w_in, gla_w2, gla_b2, gla_norm_w,
           ml_gate_b, ml_norm_w, gd_conv_w, gd_a_log, gd_dt_bias, gd_norm_w, q_norm_w, k_norm_w,
           w_branch, w_out, final_norm_w):
    nc, tc, _ = x_prompt.shape
    nl, tl, _ = x_sample.shape
    seqs = _Seqs(nc, tc, nl, tl)
    x = jnp.concatenate([x_prompt.reshape(nc * tc, D_MODEL), x_sample.reshape(nl * tl, D_MODEL)], axis=0)
    cvec = jnp.concatenate([c_ctx[None, :], c, jnp.zeros((8 - 1 - nl, D_MODEL), F32)], axis=0)
    y, ctx = _trunk(seqs, x, cvec, cache_k, cache_v, state_gla, state_mlstm_c, state_mlstm_n, state_mlstm_m,
                    state_gdn, w_ada, b_ada, w_ffn_in, w_ffn_out, w_in, gla_w2, gla_b2, gla_norm_w,
                    ml_gate_b, ml_norm_w, gd_conv_w, gd_a_log, gd_dt_bias, gd_norm_w, q_norm_w, k_norm_w,
                    w_branch, w_out, final_norm_w)
    y_prompt = y[:nc * tc].reshape(nc, tc, D_MODEL)
    y_sample = y[nc * tc:].reshape(nl, tl, D_MODEL)
    stack = lambda name: jnp.stack([cx[name] for cx in ctx], axis=1)
    return (y_prompt, y_sample, stack('k'), stack('v'), stack('gla'), stack('mc'), stack('mn'),
            stack('mm'), stack('gd'))
```

```python
import functools
import math

import numpy as np
import jax
import jax.numpy as jnp
from jax import lax
from jax.experimental import pallas as pl
from jax.experimental.pallas import tpu as pltpu

F32 = jnp.float32
BF16 = jnp.bfloat16

D_MODEL = 2048
DEPTH = 4
GRID_W = 64
N_HEADS = 4
HEAD_DIM = 128
MIX_W = N_HEADS * HEAD_DIM
N_KV = 2
KV_W = N_KV * HEAD_DIM
GLA_RANK = 16
GLA_TAU = 16.0
GLA_FAST_MAX = 60.0
CHUNK = 64
SUB = 16
Q_BLOCK = 128
CONV_K = 5
D_FF = 5632
ROPE_THETA = 10000.0
N_BRANCH = 4
N_DIR = 2
N_MOD = 9
EPS = 1e-6
SCALE = HEAD_DIM ** -0.5

Z_MERGE = 0
Z_GLA = N_BRANCH * D_MODEL
Z_ML = Z_GLA + 4 * MIX_W
Z_GDQKV = Z_ML + 4 * MIX_W
Z_GDG = Z_GDQKV + 3 * MIX_W
Z_ATQ = Z_GDG + MIX_W
Z_ATK = Z_ATQ + MIX_W
Z_ATV = Z_ATK + KV_W
NZ = Z_ATV + KV_W
LANE = 128
SUBLANES = 8
L_LR = 0
L_IF = N_DIR * GLA_RANK
L_AB = L_IF + N_DIR * 2 * N_HEADS

W_LR = 4 * MIX_W
W_ML = W_LR + N_DIR * GLA_RANK
W_IF = W_ML + 4 * MIX_W
W_GD = W_IF + N_DIR * 2 * N_HEADS
W_AB = W_GD + 4 * MIX_W
W_AT = W_AB + N_DIR * 2 * N_HEADS
W_MG = W_AT + MIX_W + 2 * KV_W
IN_TILE = 512
TILE_SRC = ([W_MG + IN_TILE * j for j in range(N_BRANCH * D_MODEL // IN_TILE)]
            + [IN_TILE * j for j in range(4)] + [W_ML + IN_TILE * j for j in range(4)]
            + [W_GD + IN_TILE * j for j in range(4)] + [W_AT + IN_TILE * j for j in range(2)])

VMEM_LIMIT = 56 * 1024 * 1024
FFN_VMEM_LIMIT = 60 * 1024 * 1024


def _cparams(n_axes, vmem_limit=VMEM_LIMIT):
    return pltpu.CompilerParams(dimension_semantics=("arbitrary",) * n_axes,
                                vmem_limit_bytes=vmem_limit)


def _dot(a, b):
    return jnp.dot(a.astype(BF16), b.astype(BF16), preferred_element_type=F32)


def _dot_nt(a, b):
    return lax.dot_general(a.astype(BF16), b.astype(BF16), (((1,), (1,)), ((), ())),
                           preferred_element_type=F32)


def _dot_tn(a, b):
    return lax.dot_general(a.astype(BF16), b.astype(BF16), (((0,), (0,)), ((), ())),
                           preferred_element_type=F32)


def _split(a):
    hi = a.astype(BF16)
    return hi, (a - hi.astype(F32)).astype(BF16)


def _dot_split(a, b):
    ah, al = a
    bh, bl = b
    return (jnp.dot(ah, bh, preferred_element_type=F32) + jnp.dot(ah, bl, preferred_element_type=F32)
            + jnp.dot(al, bh, preferred_element_type=F32))


def _cumsum_mask(mask, x):
    m = mask.astype(BF16)
    x0 = x.astype(BF16)
    r1 = x - x0.astype(F32)
    x1 = r1.astype(BF16)
    x2 = (r1 - x1.astype(F32)).astype(BF16)
    return (jnp.dot(m, x0, preferred_element_type=F32) + jnp.dot(m, x1, preferred_element_type=F32)
            + jnp.dot(m, x2, preferred_element_type=F32))


def _sigmoid(x):
    return 1.0 / (1.0 + jnp.exp(-x))


def _silu(x):
    return x * _sigmoid(x)


def _softplus(x):
    return jnp.maximum(x, 0.0) + jnp.log1p(jnp.exp(-jnp.abs(x)))


def _log_sigmoid(x):
    return -_softplus(-x)


def _rmsmod(x, sc, sh):
    ms = jnp.mean(x * x, axis=-1, keepdims=True)
    return x * lax.rsqrt(ms + EPS) * (1.0 + sc) + sh


def _head_rms(x, w):
    outs = []
    for h in range(x.shape[1] // HEAD_DIM):
        seg = x[:, h * HEAD_DIM:(h + 1) * HEAD_DIM]
        ms = jnp.mean(seg * seg, axis=-1, keepdims=True)
        outs.append(seg * lax.rsqrt(ms + EPS))
    return jnp.concatenate(outs, axis=1) * w


class _Seqs:
    def __init__(self, nc, tc, nl, tl):
        self.nc, self.tc, self.nl, self.tl = nc, tc, nl, tl
        self.ctx_rows = nc * tc
        self.rows = nc * tc + nl * tl

    def mod_row(self, start):
        return jnp.where(start < self.ctx_rows, 0, 1 + (start - self.ctx_rows) // self.tl)

    def scan_tables(self):
        fwd, bwd, flg, sq = [], [], [], []
        base = 0
        for s in range(self.nc + self.nl):
            lat = s >= self.nc
            n = (self.tl if lat else self.tc) // CHUNK
            for j in range(n):
                fwd.append(base + j)
                bwd.append(base + n - 1 - j)
                flg.append((1 if j == 0 else 0) | (2 if j == n - 1 else 0) | (4 if lat else 0))
                sq.append(s)
            base += n
        return tuple(jnp.asarray(np.array(a, np.int32)) for a in (fwd, bwd, flg, sq))


def _ada_kernel(c_ref, w_ref, b_ref, o_ref):
    o_ref[...] = _dot(_silu(c_ref[...]), w_ref[...]) + b_ref[...]


def _ada(cvec, w_ada, b_ada):
    tn = 1024
    nmod = N_MOD * D_MODEL
    depth = w_ada.shape[0]
    return pl.pallas_call(
        _ada_kernel,
        grid=(depth, nmod // tn),
        in_specs=[pl.BlockSpec((8, D_MODEL), lambda l, j: (0, 0)),
                  pl.BlockSpec((None, D_MODEL, tn), lambda l, j: (l, 0, j)),
                  pl.BlockSpec((None, 1, tn), lambda l, j: (l, 0, j))],
        out_specs=pl.BlockSpec((None, 8, tn), lambda l, j: (l, 0, j)),
        out_shape=jax.ShapeDtypeStruct((depth, 8, nmod), F32),
        compiler_params=_cparams(2),
    )(cvec, w_ada, b_ada.reshape(depth, 1, nmod))


FFN_SUB = 256


def _ffn_kernel(x_ref, sh_ref, sc_ref, g_ref, wg_ref, wu_ref, wo_ref, fw_ref, o_ref, h_ref,
                *, seqs, tm, nf, final):
    f = pl.program_id(1)
    r = seqs.mod_row(pl.program_id(0) * tm)

    @pl.when(f == 0)
    def _():
        h_ref[...] = _rmsmod(x_ref[...], sc_ref[pl.ds(r, 1), :], sh_ref[pl.ds(r, 1), :]).astype(BF16)
        o_ref[...] = jnp.zeros_like(o_ref)

    wg = wg_ref[...].astype(BF16)
    wu = wu_ref[...].astype(BF16)
    wo = wo_ref[...].astype(BF16)
    for r0 in range(0, tm, FFN_SUB):
        h = h_ref[r0:r0 + FFN_SUB, :]
        g = jnp.dot(h, wg, preferred_element_type=F32)
        u = jnp.dot(h, wu, preferred_element_type=F32)
        a = (_silu(g) * u).astype(BF16)
        o_ref[r0:r0 + FFN_SUB, :] += jnp.dot(a, wo, preferred_element_type=F32)

    @pl.when(f == nf - 1)
    def _():
        y = x_ref[...] + 0.5 * g_ref[pl.ds(r, 1), :] * o_ref[...]
        if final:
            ms = jnp.mean(y * y, axis=-1, keepdims=True)
            y = y * lax.rsqrt(ms + EPS) * fw_ref[...]
        o_ref[...] = y


def _ffn(x, mod, w_in, w_out, fw, seqs, l, i, final, tm, tf):
    rows = x.shape[0]
    nf = D_FF // tf
    j0 = 0 if i == 0 else 6
    kern = functools.partial(_ffn_kernel, seqs=seqs, tm=tm, nf=nf, final=final)
    modspec = lambda j: pl.BlockSpec((None, 8, D_MODEL), lambda m, f: (l, 0, j))
    return pl.pallas_call(
        kern,
        grid=(rows // tm, nf),
        in_specs=[pl.BlockSpec((tm, D_MODEL), lambda m, f: (m, 0), pipeline_mode=pl.Buffered(1)),
                  modspec(j0), modspec(j0 + 1), modspec(j0 + 2),
                  pl.BlockSpec((None, None, D_MODEL, tf), lambda m, f: (l, i, 0, f)),
                  pl.BlockSpec((None, None, D_MODEL, tf), lambda m, f: (l, i, 0, nf + f)),
                  pl.BlockSpec((None, None, tf, D_MODEL), lambda m, f: (l, i, f, 0)),
                  pl.BlockSpec((1, D_MODEL), lambda m, f: (0, 0))],
        out_specs=pl.BlockSpec((tm, D_MODEL), lambda m, f: (m, 0), pipeline_mode=pl.Buffered(1)),
        out_shape=jax.ShapeDtypeStruct((rows, D_MODEL), F32),
        scratch_shapes=[pltpu.VMEM((tm, D_MODEL), BF16)],
        compiler_params=_cparams(2, FFN_VMEM_LIMIT),
    )(x, mod, mod, mod, w_in, w_in, w_out, fw)


def _premix_kernel(x_ref, sh_ref, sc_ref, wlr_ref, wif_ref, wab_ref, h_ref, zs_ref, *, seqs, tm):
    r = seqs.mod_row(pl.program_id(0) * tm)
    h = _rmsmod(x_ref[...], sc_ref[pl.ds(r, 1), :], sh_ref[pl.ds(r, 1), :]).astype(BF16)
    h_ref[...] = h
    used = wlr_ref.shape[1] + wif_ref.shape[1] + wab_ref.shape[1]
    ws = jnp.concatenate([wlr_ref[0], wif_ref[0], wab_ref[0], jnp.zeros((LANE - used, D_MODEL), F32)],
                         axis=0).astype(BF16)
    zs_ref[...] = _dot_nt(h, ws)


def _premix(x, mod, w_t, seqs, l, tm):
    rows = x.shape[0]
    modspec = lambda j: pl.BlockSpec((None, 8, D_MODEL), lambda m: (l, 0, j))
    wspec = lambda start, n: pl.BlockSpec((pl.Element(1), pl.Element(n), pl.Element(D_MODEL)),
                                          lambda m: (l, start, 0))
    return pl.pallas_call(
        functools.partial(_premix_kernel, seqs=seqs, tm=tm),
        grid=(rows // tm,),
        in_specs=[pl.BlockSpec((tm, D_MODEL), lambda m: (m, 0)), modspec(3), modspec(4),
                  wspec(W_LR, W_ML - W_LR), wspec(W_IF, W_GD - W_IF), wspec(W_AB, W_AT - W_AB)],
        out_specs=(pl.BlockSpec((tm, D_MODEL), lambda m: (m, 0)), pl.BlockSpec((tm, LANE), lambda m: (m, 0))),
        out_shape=(jax.ShapeDtypeStruct((rows, D_MODEL), BF16), jax.ShapeDtypeStruct((rows, LANE), F32)),
        compiler_params=_cparams(1),
    )(x, mod, mod, w_t, w_t, w_t)


def _inproj_kernel(row_ref, h_ref, w_ref, o_ref, w_scr):
    @pl.when(pl.program_id(1) == 0)
    def _():
        w_scr[...] = w_ref[0].astype(BF16)

    w = w_scr[...]
    for r0 in range(0, h_ref.shape[0], FFN_SUB):
        o_ref[r0:r0 + FFN_SUB, :] = lax.dot_general(h_ref[r0:r0 + FFN_SUB, :], w, (((1,), (1,)), ((), ())),
                                                    preferred_element_type=F32)


def _inproj(h, w_t, l, tm):
    rows = h.shape[0]
    assert all(c % SUBLANES == 0 for c in TILE_SRC)
    src = jnp.asarray(np.array(TILE_SRC, np.int32) // SUBLANES)
    gs = pltpu.PrefetchScalarGridSpec(
        num_scalar_prefetch=1, grid=(len(TILE_SRC), rows // tm),
        in_specs=[pl.BlockSpec((tm, D_MODEL), lambda n, m, c: (m, 0)),
                  pl.BlockSpec((pl.Element(1), pl.Element(IN_TILE), pl.Element(D_MODEL)),
                               lambda n, m, c: (l, c[n] * SUBLANES, 0))],
        out_specs=pl.BlockSpec((tm, IN_TILE), lambda n, m, c: (m, n)),
        scratch_shapes=[pltpu.VMEM((IN_TILE, D_MODEL), BF16)])
    return pl.pallas_call(
        _inproj_kernel, grid_spec=gs,
        out_shape=jax.ShapeDtypeStruct((rows, NZ), F32),
        compiler_params=_cparams(2),
    )(src, h, w_t)


def _tri_masks(rev):
    ri = lax.broadcasted_iota(jnp.int32, (CHUNK, CHUNK), 0)
    ci = lax.broadcasted_iota(jnp.int32, (CHUNK, CHUNK), 1)
    if rev:
        return ri <= ci, ri < ci
    return ri >= ci, ri > ci


def _flags(flg):
    first = (flg & 1) != 0
    last = (flg & 2) != 0
    lat = (flg & 4) != 0
    return first, last, lat


def _hs(h):
    return slice(h * HEAD_DIM, (h + 1) * HEAD_DIM)


def _gla_init(lat, init_ref, st_scr):
    if lat:
        for d in range(N_DIR):
            for h in range(N_HEADS):
                st_scr[d, h] = init_ref[d, h].T
    else:
        st_scr[...] = jnp.zeros_like(st_scr)


def _gla_final(st_out, st_scr):
    for d in range(N_DIR):
        for h in range(N_HEADS):
            st_out[d, h] = st_scr[d, h].T


def _gla_pre(s_ref, w2, b2):
    return jnp.dot(s_ref[...].astype(BF16), w2[...], preferred_element_type=F32) + b2[...]


GLA_FAST_PRE_MIN = math.log(2.0) - GLA_FAST_MAX


def _gla_step(pres, fast, qf, kf, vf, qb, kb, vb, of_ref, ob_ref, st_scr, b_scr):
    dirs = ((qf, kf, vf, of_ref), (qb, kb, vb, ob_ref))
    for d, (q_ref, k_ref, v_ref, o_ref) in enumerate(dirs):
        rev = d == 1
        incl, _ = _tri_masks(rev)
        lg = _log_sigmoid(pres[d]) / GLA_TAU
        b = _cumsum_mask(incl, lg)
        if not fast:
            b_scr[d] = b
        bend = jnp.sum(lg, axis=0, keepdims=True)
        yield
        q = q_ref[...] * SCALE
        k = k_ref[...]
        v = v_ref[...]
        qd = q * jnp.exp(b)
        kd = k * jnp.exp(bend - b)
        eb = jnp.exp(bend)

        ri = lax.broadcasted_iota(jnp.int32, (CHUNK, CHUNK), 0)
        ci = lax.broadcasted_iota(jnp.int32, (CHUNK, CHUNK), 1)
        if fast:
            att_mask = (ri <= ci) if rev else (ri >= ci)
        elif rev:
            att_mask = ci >= (ri // SUB + 1) * SUB
        else:
            att_mask = ci < (ri // SUB) * SUB
        nsub = CHUNK // SUB
        att_rows = [[] for _ in range(N_HEADS)]
        for blk in range(nsub):
            r0 = blk * SUB
            has_edge = (blk < nsub - 1) if rev else (blk > 0)
            if not has_edge and not fast:
                for h in range(N_HEADS):
                    att_rows[h].append(jnp.zeros((SUB, CHUNK), F32))
                continue
            edge = r0 + SUB if rev else r0 - 1
            bref = b[edge:edge + 1, :] if has_edge else jnp.zeros((1, MIX_W), F32)
            qe = q[r0:r0 + SUB, :] * jnp.exp(b[r0:r0 + SUB, :] - bref)
            ke = k * jnp.exp(jnp.minimum(bref - b, GLA_FAST_MAX if fast else 0.0))
            for h in range(N_HEADS):
                att_rows[h].append(_dot_nt(qe[:, _hs(h)], ke[:, _hs(h)]))
            if fast:
                yield
        o_main = []
        for h in range(N_HEADS):
            att = jnp.where(att_mask, jnp.concatenate(att_rows[h], axis=0), 0.0)
            o_main.append(_dot_nt(qd[:, _hs(h)], st_scr[d, h]) + _dot(att, v[:, _hs(h)]))
        o_main = jnp.concatenate(o_main, axis=1)
        yield

        if fast:
            o_ref[...] = o_main
        else:
            rowi = lax.broadcasted_iota(jnp.int32, (SUBLANES, 1), 0)
            for r0 in range(0, CHUNK, SUBLANES):
                blk0 = (r0 // SUB) * SUB
                bb = b[r0:r0 + SUBLANES, :]
                qq = q[r0:r0 + SUBLANES, :]
                acc = None
                for s in range(blk0, blk0 + SUB):
                    if (s > r0 + SUBLANES - 1 and not rev) or (s < r0 and rev):
                        continue
                    bs = b_scr[d, s:s + 1, :]
                    ks = k_ref[s:s + 1, :]
                    vs = v_ref[s:s + 1, :]
                    p = qq * ks * jnp.exp(bb - bs)
                    valid = (rowi <= s - r0) if rev else (rowi >= s - r0)
                    parts = []
                    for h in range(N_HEADS):
                        c = jnp.sum(p[:, _hs(h)], axis=-1, keepdims=True)
                        parts.append(jnp.where(valid, c, 0.0) * vs[:, _hs(h)])
                    term = jnp.concatenate(parts, axis=1)
                    acc = term if acc is None else acc + term
                    if (s - blk0) % 4 == 3:
                        yield
                o_ref[r0:r0 + SUBLANES, :] = o_main[r0:r0 + SUBLANES, :] + acc
                yield

        for h in range(N_HEADS):
            st_scr[d, h] = st_scr[d, h] * eb[:, _hs(h)] + _dot_tn(v[:, _hs(h)], kd[:, _hs(h)])
        yield


def _copy_or_zero(lat, pairs):
    for src, dst in pairs:
        dst[...] = src[...] if lat else jnp.zeros_like(dst)


def _mlstm_step(qf, kf, vf, sf, qb, kb, vb, sb, bias_ref, of_ref, ob_ref, c_scr, n_scr, m_scr):
    dirs = ((qf, kf, vf, sf, of_ref), (qb, kb, vb, sb, ob_ref))
    for d, (q_ref, k_ref, v_ref, s_ref, o_ref) in enumerate(dirs):
        rev = d == 1
        incl, _ = _tri_masks(rev)
        last_row = 0 if rev else CHUNK - 1
        gates = s_ref[...] + bias_ref[...]
        lf_all = _log_sigmoid(gates)
        f_all = _cumsum_mask(incl, lf_all)
        f_all_t = f_all.T
        gates_t = gates.T
        q = q_ref[...]
        k = k_ref[...] * SCALE
        v = v_ref[...]
        for h in range(N_HEADS):
            ji = L_IF + d * 2 * N_HEADS + h
            jf = ji + N_HEADS
            r = d * N_HEADS + h
            fc = f_all[:, jf:jf + 1]
            frow = f_all_t[jf:jf + 1, :]
            igc = gates[:, ji:ji + 1]
            igrow = gates_t[ji:ji + 1, :]
            m_prev = m_scr[r:r + 1, 0:1]
            n_prev = n_scr[r:r + 1, :]
            c_prev = c_scr[d, h]
            qh, kh, vh = q[:, _hs(h)], k[:, _hs(h)], v[:, _hs(h)]
            log_d = jnp.where(incl, fc - frow + igrow, -jnp.inf)
            inter = fc + m_prev
            m_t = jnp.maximum(inter, jnp.max(log_d, axis=-1, keepdims=True))
            dmat = jnp.exp(log_d - m_t)
            a_in = jnp.exp(inter - m_t)
            s = _dot_nt(qh, kh) * dmat
            num = a_in * _dot(qh, c_prev) + _dot(s, vh)
            den = a_in * jnp.sum(qh * n_prev, axis=-1, keepdims=True) + jnp.sum(s, axis=-1, keepdims=True)
            o_ref[:, _hs(h)] = num / jnp.maximum(jnp.abs(den), jnp.exp(-m_t))
            m_new = m_t[last_row:last_row + 1, :]
            f_end = fc[last_row:last_row + 1, :]
            w_end = jnp.exp(f_end - fc + igc - m_new)
            a_end = jnp.exp(f_end + m_prev - m_new)
            kw = kh * w_end
            c_scr[d, h] = a_end * c_prev + _dot_tn(kw, vh)
            n_scr[r:r + 1, :] = a_end * n_prev + jnp.sum(kw, axis=0, keepdims=True)
            m_scr[r:r + 1, :] = jnp.broadcast_to(m_new, (1, LANE))
            yield


def _unit_tri_inverse_all(ns):
    ri = lax.broadcasted_iota(jnp.int32, (CHUNK, CHUNK), 0)
    ci = lax.broadcasted_iota(jnp.int32, (CHUNK, CHUNK), 1)
    same16 = (ri // SUB) == (ci // SUB)
    same32 = (ri // (2 * SUB)) == (ci // (2 * SUB))
    eye = (ri == ci).astype(F32)
    nd = [jnp.where(same16, n, 0.0) for n in ns]
    n1 = [_split(jnp.where(jnp.logical_and(same32, jnp.logical_not(same16)), n, 0.0)) for n in ns]
    n2 = [_split(jnp.where(same32, 0.0, n)) for n in ns]
    t = [eye - x for x in nd]
    nds = [_split(x) for x in nd]
    p = [_dot_split(x, x) for x in nds]
    yield
    for level in range(3):
        ps = [_split(x) for x in p]
        t = [x + _dot_split(_split(x), y) for x, y in zip(t, ps)]
        if level < 2:
            p = [_dot_split(y, y) for y in ps]
        yield
    for nn in (n1, n2):
        ts = [_split(x) for x in t]
        a = [_dot_split(y, x) for x, y in zip(ts, nn)]
        yield
        t = [x - _dot_split(xs, _split(y)) for x, xs, y in zip(t, ts, a)]
        yield
    return t


def _gdn_step(qf, kf, vf, sf, qb, kb, vb, sb, alog_ref, dtb_ref, of_ref, ob_ref, s_scr):
    units = []
    dirs = ((qf, kf, vf, sf, of_ref), (qb, kb, vb, sb, ob_ref))
    for d, (q_ref, k_ref, v_ref, s_ref, o_ref) in enumerate(dirs):
        rev = d == 1
        incl, strict = _tri_masks(rev)
        last_row = 0 if rev else CHUNK - 1
        small = s_ref[...]
        g_all = -jnp.exp(alog_ref[...]) * _softplus(small + dtb_ref[...])
        beta_all = _sigmoid(small)
        gam_all = _cumsum_mask(incl, g_all)
        gam_t = gam_all.T
        q = q_ref[...]
        k = k_ref[...]
        v = v_ref[...]
        for h in range(N_HEADS):
            jg = L_AB + d * 2 * N_HEADS + h
            jb = jg + N_HEADS
            gc = gam_all[:, jg:jg + 1]
            units.append(dict(
                d=d, h=h, o_ref=o_ref, incl=incl, strict=strict, gc=gc, grow=gam_t[jg:jg + 1, :],
                beta=beta_all[:, jb:jb + 1], g_end=gc[last_row:last_row + 1, :],
                q=q[:, _hs(h)], k=k[:, _hs(h)], v=v[:, _hs(h)]))

    for u in units:
        u['decay'] = jnp.exp(jnp.where(u['incl'], u['gc'] - u['grow'], -jnp.inf))
    kk = [_dot_nt(u['k'], u['k']) for u in units]
    qk = [_dot_nt(u['q'], u['k']) for u in units]
    ns = [jnp.where(u['strict'], u['beta'] * x * u['decay'], 0.0) for u, x in zip(units, kk)]
    yield
    ts = yield from _unit_tri_inverse_all(ns)
    rhs = [_split(jnp.concatenate([u['v'] * u['beta'], u['k'] * (u['beta'] * jnp.exp(u['gc']))], axis=1))
           for u in units]
    sol = [_dot_split(_split(t), r) for t, r in zip(ts, rhs)]
    yield
    s_prev = [s_scr[u['d'], u['h']] for u in units]
    w_new = [x[:, :HEAD_DIM] - _dot(x[:, HEAD_DIM:], sp) for x, sp in zip(sol, s_prev)]
    yield
    for u, x, w, sp in zip(units, qk, w_new, s_prev):
        u['o_ref'][:, _hs(u['h'])] = _dot(u['q'] * jnp.exp(u['gc']), sp) + _dot(x * u['decay'], w)
    for u, w, sp in zip(units, w_new, s_prev):
        s_scr[u['d'], u['h']] = (jnp.exp(u['g_end']) * sp
                                 + _dot_tn(u['k'] * jnp.exp(u['g_end'] - u['gc']), w))


N_UNITS = N_DIR * N_HEADS
_DONE = object()
ST_SHAPE = (N_DIR, N_HEADS, HEAD_DIM, HEAD_DIM)


def _scan_kernel(fwd_ref, bwd_ref, flg_ref, sq_ref,
                 g_f, g_b, m_f, m_b, d_f, d_b, sf, sb,
                 w2f, w2b, b2f, b2b, mbias, alog, dtb,
                 g0, c0, n0, m0, s0,
                 g_of, g_ob, m_of, m_ob, d_of, d_ob, g_out, c_out, n_out, m_out, s_out,
                 g_scr, b_scr, c_scr, n_scr, m_scr, s_scr):
    first, last, lat = _flags(flg_ref[pl.program_id(0)])
    carried = ((c0, c_scr), (n0, n_scr), (m0, m_scr), (s0, s_scr))
    qkv = lambda ref: tuple(ref.at[:, j * MIX_W:(j + 1) * MIX_W] for j in range(3))
    (gqf, gkf, gvf), (gqb, gkb, gvb) = qkv(g_f), qkv(g_b)
    (mqf, mkf, mvf), (mqb, mkb, mvb) = qkv(m_f), qkv(m_b)
    (dqf, dkf, dvf), (dqb, dkb, dvb) = qkv(d_f), qkv(d_b)

    @pl.when(jnp.logical_and(first, jnp.logical_not(lat)))
    def _():
        _gla_init(False, g0, g_scr)
        _copy_or_zero(False, carried)

    @pl.when(jnp.logical_and(first, lat))
    def _():
        _gla_init(True, g0, g_scr)
        _copy_or_zero(True, carried)

    pres = (_gla_pre(sf, w2f, b2f), _gla_pre(sb, w2b, b2b))
    fast = jnp.minimum(jnp.min(pres[0]), jnp.min(pres[1])) > GLA_FAST_PRE_MIN

    def run(gla_fast):
        stages = [_gdn_step(dqf, dkf, dvf, sf, dqb, dkb, dvb, sb, alog, dtb, d_of, d_ob, s_scr),
                  _mlstm_step(mqf, mkf, mvf, sf, mqb, mkb, mvb, sb, mbias, m_of, m_ob, c_scr, n_scr, m_scr),
                  _gla_step(pres, gla_fast, gqf, gkf, gvf, gqb, gkb, gvb, g_of, g_ob, g_scr, b_scr)]
        strides = {stages[0]: 1, stages[1]: 1, stages[2]: 1 if gla_fast else 4}
        while stages:
            for g in list(stages):
                for _ in range(strides[g]):
                    if g in stages and next(g, _DONE) is _DONE:
                        stages.remove(g)

    @pl.when(fast)
    def _():
        run(True)

    @pl.when(jnp.logical_not(fast))
    def _():
        run(False)

    @pl.when(jnp.logical_and(last, jnp.logical_not(lat)))
    def _():
        _gla_final(g_out, g_scr)
        for src, dst in ((c_scr, c_out), (n_scr, n_out), (m_scr, m_out), (s_scr, s_out)):
            dst[...] = src[...]


def _scans(z, zq, zs, seqs, l, w2f, w2b, b2f, b2b, mbias, alog_row, dtb_row,
           state_gla, state_c, state_n, state_m, state_gdn):
    rows = z.shape[0]
    tabs = seqs.scan_tables()
    nsteps = int(tabs[0].shape[0])

    def blk(which, col, width):
        if which == 0:
            return pl.BlockSpec((CHUNK, width), lambda i, fw, bw, fl, sq: (fw[i], col))
        return pl.BlockSpec((CHUNK, width), lambda i, fw, bw, fl, sq: (bw[i], col))

    def const(shape):
        nd = len(shape)
        return pl.BlockSpec(shape, lambda i, fw, bw, fl, sq: (0,) * nd)

    lat_i = lambda sq, i: jnp.maximum(sq[i] - seqs.nc, 0)
    ctx_i = lambda sq, i: jnp.minimum(sq[i], seqs.nc - 1)
    in_specs, args = [], []
    def qkv_blk(which, col):
        shape = (pl.Element(CHUNK), pl.Element(3 * MIX_W))
        if which == 0:
            return pl.BlockSpec(shape, lambda i, fw, bw, fl, sq: (fw[i] * CHUNK, col))
        return pl.BlockSpec(shape, lambda i, fw, bw, fl, sq: (bw[i] * CHUNK, col))

    for src, col in ((z, Z_GLA), (z, Z_ML), (zq, 0)):
        in_specs += [qkv_blk(0, col), qkv_blk(1, col)]
        args += [src] * 2
    in_specs += [blk(0, 0, LANE), blk(1, 0, LANE)]
    args += [zs, zs]
    in_specs += [const((LANE, MIX_W)), const((LANE, MIX_W)), const((1, MIX_W)), const((1, MIX_W)),
                 const((1, LANE)), const((1, LANE)), const((1, LANE))]
    args += [w2f, w2b, b2f, b2b, mbias, alog_row, dtb_row]
    st_in = pl.BlockSpec((None, None) + ST_SHAPE, lambda i, fw, bw, fl, sq: (lat_i(sq, i), l, 0, 0, 0, 0))
    vec_in = pl.BlockSpec((None, None, N_UNITS, LANE), lambda i, fw, bw, fl, sq: (lat_i(sq, i), l, 0, 0))
    depth = state_c.shape[1]
    n0 = state_n.reshape(seqs.nl, depth, N_UNITS, HEAD_DIM)
    m0 = jnp.broadcast_to(state_m.reshape(seqs.nl, depth, N_UNITS, 1), (seqs.nl, depth, N_UNITS, LANE))
    in_specs += [st_in, st_in, vec_in, vec_in, st_in]
    args += [state_gla, state_c, n0, m0, state_gdn]

    o_f = pl.BlockSpec((CHUNK, MIX_W), lambda i, fw, bw, fl, sq: (fw[i], 0))
    o_b = pl.BlockSpec((CHUNK, MIX_W), lambda i, fw, bw, fl, sq: (bw[i], 0))
    st_out = pl.BlockSpec((None,) + ST_SHAPE, lambda i, fw, bw, fl, sq: (ctx_i(sq, i), 0, 0, 0, 0))
    vec_out = pl.BlockSpec((None, N_UNITS, LANE), lambda i, fw, bw, fl, sq: (ctx_i(sq, i), 0, 0))
    o_sds = jax.ShapeDtypeStruct((rows, MIX_W), F32)
    st_sds = jax.ShapeDtypeStruct((seqs.nc,) + ST_SHAPE, F32)
    vec_sds = jax.ShapeDtypeStruct((seqs.nc, N_UNITS, LANE), F32)
    gs = pltpu.PrefetchScalarGridSpec(
        num_scalar_prefetch=4, grid=(nsteps,), in_specs=in_specs,
        out_specs=(o_f, o_b, o_f, o_b, o_f, o_b, st_out, st_out, vec_out, vec_out, st_out),
        scratch_shapes=[pltpu.VMEM(ST_SHAPE, F32), pltpu.VMEM((N_DIR, CHUNK, MIX_W), F32),
                        pltpu.VMEM(ST_SHAPE, F32), pltpu.VMEM((N_UNITS, LANE), F32),
                        pltpu.VMEM((N_UNITS, LANE), F32), pltpu.VMEM(ST_SHAPE, F32)])
    return pl.pallas_call(
        _scan_kernel, grid_spec=gs,
        out_shape=(o_sds,) * 6 + (st_sds, st_sds, vec_sds, vec_sds, st_sds),
        compiler_params=_cparams(1),
    )(*tabs, *args)


CONV_ROWS = 256
HALO = 8


def _conv_kernel(x_ref, prev_ref, next_ref, w_ref, o_ref, xe_ref, *, seqs):
    i = pl.program_id(0)
    start = i * CONV_ROWS
    in_lat = start >= seqs.ctx_rows
    off = jnp.where(in_lat, (start - seqs.ctx_rows) % seqs.tl, start % seqs.tc)
    seq_len = jnp.where(in_lat, seqs.tl, seqs.tc)
    xe_ref[0:HALO, :] = jnp.where(off > 0, prev_ref[...], 0.0)
    xe_ref[HALO:HALO + CONV_ROWS, :] = x_ref[...]
    xe_ref[HALO + CONV_ROWS:, :] = jnp.where(off + CONV_ROWS < seq_len, next_ref[...], 0.0)
    pad = CONV_K // 2
    y = jnp.zeros((CONV_ROWS, 3 * MIX_W), F32)
    for j in range(CONV_K):
        y = y + xe_ref[pl.ds(HALO - pad + j, CONV_ROWS), :] * w_ref[j:j + 1, :]
    y = _silu(y)
    outs = []
    for h in range(3 * N_HEADS):
        seg = y[:, _hs(h)]
        if h < 2 * N_HEADS:
            seg = seg * lax.rsqrt(jnp.sum(seg * seg, axis=-1, keepdims=True) + EPS)
            if h < N_HEADS:
                seg = seg * SCALE
        outs.append(seg)
    o_ref[...] = jnp.concatenate(outs, axis=1)


def _gdn_prep(z, conv_w, seqs):
    rows = z.shape[0]
    cw = 3 * MIX_W
    cb = Z_GDQKV // cw
    nb = rows // CONV_ROWS
    per = CONV_ROWS // HALO
    last8 = rows // HALO - 1
    return pl.pallas_call(
        functools.partial(_conv_kernel, seqs=seqs),
        grid=(nb,),
        in_specs=[pl.BlockSpec((CONV_ROWS, cw), lambda i: (i, cb)),
                  pl.BlockSpec((HALO, cw), lambda i: (jnp.maximum(i * per - 1, 0), cb)),
                  pl.BlockSpec((HALO, cw), lambda i: (jnp.minimum((i + 1) * per, last8), cb)),
                  pl.BlockSpec((8, cw), lambda i: (0, 0))],
        out_specs=pl.BlockSpec((CONV_ROWS, cw), lambda i: (i, 0)),
        out_shape=jax.ShapeDtypeStruct((rows, cw), F32),
        scratch_shapes=[pltpu.VMEM((CONV_ROWS + 2 * HALO, cw), F32)],
        compiler_params=_cparams(1),
    )(z, z, z, conv_w)


ATT_ROWS = 256


def _rope(y, cos, sin_signed):
    n = y.shape[1]
    lane = lax.broadcasted_iota(jnp.int32, y.shape, 1)
    partner = jnp.where(lane % 2 == 0, pltpu.roll(y, n - 1, axis=1), pltpu.roll(y, 1, axis=1))
    reps = n // HEAD_DIM
    c = jnp.concatenate([cos] * reps, axis=1)
    s = jnp.concatenate([sin_signed] * reps, axis=1)
    return y * c + partner * s


def _attn_prep_kernel(q_ref, k_ref, cos_ref, sin_ref, qw_ref, kw_ref, qo_ref, ko_ref):
    cos = cos_ref[...]
    sin = sin_ref[...]
    qo_ref[...] = _rope(_head_rms(q_ref[...], qw_ref[...]), cos, sin)
    ko_ref[...] = _rope(_head_rms(k_ref[...], kw_ref[...]), cos, sin)


def _attn_prep(z, cos_t, sin_t, qw, kw):
    rows = z.shape[0]
    return pl.pallas_call(
        _attn_prep_kernel,
        grid=(rows // ATT_ROWS,),
        in_specs=[pl.BlockSpec((ATT_ROWS, MIX_W), lambda i: (i, Z_ATQ // MIX_W)),
                  pl.BlockSpec((ATT_ROWS, KV_W), lambda i: (i, Z_ATK // KV_W)),
                  pl.BlockSpec((ATT_ROWS, HEAD_DIM), lambda i: (i, 0)),
                  pl.BlockSpec((ATT_ROWS, HEAD_DIM), lambda i: (i, 0)),
                  pl.BlockSpec((1, MIX_W), lambda i: (0, 0)),
                  pl.BlockSpec((1, KV_W), lambda i: (0, 0))],
        out_specs=(pl.BlockSpec((ATT_ROWS, MIX_W), lambda i: (i, 0)),
                   pl.BlockSpec((ATT_ROWS, KV_W), lambda i: (i, 0))),
        out_shape=(jax.ShapeDtypeStruct((rows, MIX_W), F32), jax.ShapeDtypeStruct((rows, KV_W), F32)),
        compiler_params=_cparams(1),
    )(z, z, cos_t, sin_t, qw, kw)


def _attn_kernel(*refs, has_cache):
    if has_cache:
        q_ref, k_ref, v_ref, ck_ref, cv_ref, o_ref = refs
    else:
        q_ref, k_ref, v_ref, o_ref = refs
    k = k_ref[...]
    v = v_ref[...]
    for g in range(N_HEADS // N_KV):
        qh = q_ref[:, _hs(g)]
        s = _dot_nt(qh, k) * SCALE
        m = jnp.max(s, axis=-1, keepdims=True)
        if has_cache:
            sc = _dot_nt(qh, ck_ref[...]) * SCALE
            m = jnp.maximum(m, jnp.max(sc, axis=-1, keepdims=True))
            pc = jnp.exp(sc - m)
        p = jnp.exp(s - m)
        den = jnp.sum(p, axis=-1, keepdims=True)
        if has_cache:
            den = den + jnp.sum(pc, axis=-1, keepdims=True)
        inv = 1.0 / den
        o = _dot(p * inv, v)
        if has_cache:
            o = o + _dot(pc * inv, cv_ref[...])
        o_ref[:, _hs(g)] = o


def _attend(qn, kn, z, row0, nseq, t, y_rows, cache=None, l=0):
    gw = (N_HEADS // N_KV) * HEAD_DIM
    nqb = t // Q_BLOCK
    rb0 = row0 // Q_BLOCK
    sb0 = row0 // t
    vcol = Z_ATV // HEAD_DIM
    in_specs = [pl.BlockSpec((Q_BLOCK, gw), lambda b, kv, qi: (rb0 + b * nqb + qi, kv)),
                pl.BlockSpec((t, HEAD_DIM), lambda b, kv, qi: (sb0 + b, kv)),
                pl.BlockSpec((t, HEAD_DIM), lambda b, kv, qi: (sb0 + b, vcol + kv))]
    args = [qn, kn, z]
    if cache is not None:
        ck, cv = cache
        past = ck.shape[2]
        cspec = pl.BlockSpec((None, None, past, HEAD_DIM), lambda b, kv, qi: (b, l, 0, kv))
        in_specs += [cspec, cspec]
        args += [ck, cv]
    return pl.pallas_call(
        functools.partial(_attn_kernel, has_cache=cache is not None),
        grid=(nseq, N_KV, nqb),
        in_specs=in_specs,
        out_specs=pl.BlockSpec((Q_BLOCK, gw), lambda b, kv, qi: (b * nqb + qi, kv)),
        out_shape=jax.ShapeDtypeStruct((y_rows, MIX_W), F32),
        compiler_params=_cparams(3),
    )(*args)


POST_ROWS = 512


def _post_kernel(gf, gb, mf, mb, df, db, zg, zo, zd, atc, atl, nw_ref, o_ref, *, ctx_blocks):
    nw = nw_ref[...]
    o_ref[0] = (_head_rms(gf[...] + gb[...], nw[0:1, :]) * _silu(zg[...])).astype(BF16)
    o_ref[1] = (_head_rms(mf[...] + mb[...], nw[1:2, :]) * _sigmoid(zo[...])).astype(BF16)
    o_ref[2] = (_head_rms(df[...] + db[...], nw[2:3, :]) * _silu(zd[...])).astype(BF16)
    at = jnp.where(pl.program_id(0) < ctx_blocks, atc[...], atl[...])
    o_ref[3] = at.astype(BF16)


def _branch_post(outs, z, y_ctx, y_lat, norm_w):
    rows = z.shape[0]
    tm = min(POST_ROWS, y_ctx.shape[0], y_lat.shape[0])
    ctx_blocks = y_ctx.shape[0] // tm
    row = lambda c: pl.BlockSpec((tm, MIX_W), lambda i: (i, c))
    in_specs = [row(0)] * 6 + [row(Z_GLA // MIX_W + 3), row(Z_ML // MIX_W + 3), row(Z_GDG // MIX_W),
                               pl.BlockSpec((tm, MIX_W), lambda i: (jnp.minimum(i, ctx_blocks - 1), 0)),
                               pl.BlockSpec((tm, MIX_W), lambda i: (jnp.maximum(i - ctx_blocks, 0), 0)),
                               pl.BlockSpec((8, MIX_W), lambda i: (0, 0))]
    return pl.pallas_call(
        functools.partial(_post_kernel, ctx_blocks=ctx_blocks),
        grid=(rows // tm,),
        in_specs=in_specs,
        out_specs=pl.BlockSpec((N_BRANCH, tm, MIX_W), lambda i: (0, i, 0)),
        out_shape=jax.ShapeDtypeStruct((N_BRANCH, rows, MIX_W), BF16),
        compiler_params=_cparams(1),
    )(*outs, z, z, z, y_ctx, y_lat, norm_w)


def _merge_kernel(x_ref, g_ref, y_ref, zm_ref, wb_ref, wo_ref, o_ref, acc_ref, *, seqs, tm):
    n = pl.program_id(1)
    r = seqs.mod_row(pl.program_id(0) * tm)
    p = _sigmoid(zm_ref[...]) * jnp.dot(y_ref[...], wb_ref[...], preferred_element_type=F32)

    @pl.when(n == 0)
    def _():
        acc_ref[...] = p

    @pl.when(n > 0)
    def _():
        acc_ref[...] += p

    @pl.when(n == N_BRANCH - 1)
    def _():
        out = jnp.dot(acc_ref[...].astype(BF16), wo_ref[...], preferred_element_type=F32)
        o_ref[...] = x_ref[...] + g_ref[pl.ds(r, 1), :] * out


def _merge(x, mod, ybr, z, wb, wo, seqs, l, tm):
    rows = x.shape[0]
    return pl.pallas_call(
        functools.partial(_merge_kernel, seqs=seqs, tm=tm),
        grid=(rows // tm, N_BRANCH),
        in_specs=[pl.BlockSpec((tm, D_MODEL), lambda m, n: (m, 0), pipeline_mode=pl.Buffered(1)),
                  pl.BlockSpec((None, 8, D_MODEL), lambda m, n: (l, 0, 5)),
                  pl.BlockSpec((None, tm, MIX_W), lambda m, n: (n, m, 0)),
                  pl.BlockSpec((tm, D_MODEL), lambda m, n: (m, n)),
                  pl.BlockSpec((None, None, MIX_W, D_MODEL), lambda m, n: (l, n, 0, 0)),
                  pl.BlockSpec((None, D_MODEL, D_MODEL), lambda m, n: (l, 0, 0), pipeline_mode=pl.Buffered(1))],
        out_specs=pl.BlockSpec((tm, D_MODEL), lambda m, n: (m, 0)),
        out_shape=jax.ShapeDtypeStruct((rows, D_MODEL), F32),
        scratch_shapes=[pltpu.VMEM((tm, D_MODEL), F32)],
        compiler_params=_cparams(2),
    )(x, mod, ybr, z, wb, wo)


def _lane_row(vals, lane0):
    v = vals.reshape(-1).astype(F32)
    return jnp.zeros((1, LANE), F32).at[0, lane0:lane0 + v.shape[0]].set(v)


def _rope_tables(seqs):
    t = seqs.tl
    row = (np.arange(t) // GRID_W).astype(np.float32)
    col = (np.arange(t) % GRID_W).astype(np.float32)
    n_pairs = HEAD_DIM // 4
    inv = jnp.asarray(ROPE_THETA, F32) ** (-jnp.arange(n_pairs, dtype=F32) / n_pairs)
    ang = jnp.concatenate([jnp.asarray(row)[:, None] * inv, jnp.asarray(col)[:, None] * inv], axis=-1)
    cos = jnp.repeat(jnp.cos(ang), 2, axis=-1)
    sin = jnp.repeat(jnp.sin(ang), 2, axis=-1) * jnp.asarray(np.tile([-1.0, 1.0], HEAD_DIM // 2), F32)
    cos = jnp.concatenate([jnp.ones((seqs.ctx_rows, HEAD_DIM), F32)] + [cos] * seqs.nl, axis=0)
    sin = jnp.concatenate([jnp.zeros((seqs.ctx_rows, HEAD_DIM), F32)] + [sin] * seqs.nl, axis=0)
    return cos, sin


def _trunk(seqs, x, cvec, cache_k, cache_v, state_gla, state_c, state_n, state_m, state_gdn,
           w_ada, b_ada, w_ffn_in, w_ffn_out, w_in, gla_w2, gla_b2, gla_norm_w, ml_gate_b, ml_norm_w,
           gd_conv_w, gd_a_log, gd_dt_bias, gd_norm_w, q_norm_w, k_norm_w, w_branch, w_out, final_norm_w,
           tm_ffn=1024, tf=512, tm_in=2048, tm_pre=512, tm_mg=512):
    depth = w_in.shape[0]
    rows = seqs.rows
    tm_ffn, tm_in, tm_mg = min(tm_ffn, seqs.tl), min(tm_in, seqs.tl), min(tm_mg, seqs.tl)
    mod = _ada(cvec, w_ada, b_ada)
    w_t = jnp.swapaxes(w_in, 1, 2)
    wb = w_branch.astype(BF16)
    wo = w_out.astype(BF16)
    cos_t, sin_t = _rope_tables(seqs)
    fw = final_norm_w.reshape(1, D_MODEL)
    past = cache_k.shape[2]
    ck = cache_k.reshape(cache_k.shape[:2] + (past, KV_W))
    cv = cache_v.reshape(cache_v.shape[:2] + (past, KV_W))
    zpad = jnp.zeros((LANE - GLA_RANK, MIX_W), F32)
    ctx = []
    for l in range(depth):
        x = _ffn(x, mod, w_ffn_in, w_ffn_out, fw, seqs, l, 0, False, tm_ffn, tf)
        h, zs = _premix(x, mod, w_t, seqs, l, tm_pre)
        z = _inproj(h, w_t, l, tm_in)

        w2f = jnp.concatenate([gla_w2[l, 0], zpad], axis=0).astype(BF16)
        w2b = jnp.concatenate([zpad[:GLA_RANK], gla_w2[l, 1], zpad[:LANE - 2 * GLA_RANK]], axis=0).astype(BF16)
        zq = _gdn_prep(z, jnp.concatenate([gd_conv_w[l], jnp.zeros((8 - CONV_K, 3 * MIX_W), F32)], axis=0), seqs)
        ab_lanes = jnp.concatenate([gd_a_log[l], jnp.zeros((N_DIR, N_HEADS), F32)], axis=1)
        dt_lanes = jnp.concatenate([gd_dt_bias[l], jnp.zeros((N_DIR, N_HEADS), F32)], axis=1)
        gf, gb, mf, mb, df, db, st_gla, st_c, st_n, st_m, st_gd = _scans(
            z, zq, zs, seqs, l, w2f, w2b, gla_b2[l, 0:1], gla_b2[l, 1:2], _lane_row(ml_gate_b[l], L_IF),
            _lane_row(ab_lanes, L_AB), _lane_row(dt_lanes, L_AB),
            state_gla, state_c, state_n, state_m, state_gdn)

        qn, kn = _attn_prep(z, cos_t, sin_t, jnp.tile(q_norm_w[l], N_HEADS)[None, :],
                            jnp.tile(k_norm_w[l], N_KV)[None, :])
        y_ctx = _attend(qn, kn, z, 0, seqs.nc, seqs.tc, seqs.ctx_rows)
        y_lat = _attend(qn, kn, z, seqs.ctx_rows, seqs.nl, seqs.tl, seqs.nl * seqs.tl, cache=(ck, cv), l=l)

        norm_w = jnp.stack([jnp.tile(w[l], N_HEADS) for w in (gla_norm_w, ml_norm_w, gd_norm_w)]
                           + [jnp.zeros((MIX_W,), F32)] * 5, axis=0)
        ybr = _branch_post((gf, gb, mf, mb, df, db), z, y_ctx, y_lat, norm_w)
        x = _merge(x, mod, ybr, z, wb, wo, seqs, l, tm_mg)
        x = _ffn(x, mod, w_ffn_in, w_ffn_out, fw, seqs, l, 1, l == depth - 1, tm_ffn, tf)

        nc, tc = seqs.nc, seqs.tc
        ctx.append(dict(
            k=kn[:seqs.ctx_rows].reshape(nc, tc, N_KV, HEAD_DIM),
            v=z[:seqs.ctx_rows, Z_ATV:Z_ATV + KV_W].reshape(nc, tc, N_KV, HEAD_DIM),
            gla=st_gla, mc=st_c,
            mn=st_n.reshape(nc, N_DIR, N_HEADS, HEAD_DIM),
            mm=st_m[:, :, 0].reshape(nc, N_DIR, N_HEADS),
            gd=st_gd))
    return x, ctx


def kernel(x_prompt, x_sample, cache_k, cache_v, state_gla, state_mlstm_c, state_mlstm_n, state_mlstm_m,
           state_gdn, c, c_ctx, w_ada, b_ada, w_ffn_in, w_ffn_out, w_in, gla_w2, gla_b2, gla_norm_w,
           ml_gate_b, ml_norm_w, gd_conv_w, gd_a_log, gd_dt_bias, gd_norm_w, q_norm_w, k_norm_w,
           w_branch, w_out, final_norm_w):
    nc, tc, _ = x_prompt.shape
    nl, tl, _ = x_sample.shape
    seqs = _Seqs(nc, tc, nl, tl)
    x = jnp.concatenate([x_prompt.reshape(nc * tc, D_MODEL), x_sample.reshape(nl * tl, D_MODEL)], axis=0)
    cvec = jnp.concatenate([c_ctx[None, :], c, jnp.zeros((8 - 1 - nl, D_MODEL), F32)], axis=0)
    y, ctx = _trunk(seqs, x, cvec, cache_k, cache_v, state_gla, state_mlstm_c, state_mlstm_n, state_mlstm_m,
                    state_gdn, w_ada, b_ada, w_ffn_in, w_ffn_out, w_in, gla_w2, gla_b2, gla_norm_w,
                    ml_gate_b, ml_norm_w, gd_conv_w, gd_a_log, gd_dt_bias, gd_norm_w, q_norm_w, k_norm_w,
                    w_branch, w_out, final_norm_w)
    y_prompt = y[:nc * tc].reshape(nc, tc, D_MODEL)
    y_sample = y[nc * tc:].reshape(nl, tl, D_MODEL)
    stack = lambda name: jnp.stack([cx[name] for cx in ctx], axis=1)
    return (y_prompt, y_sample, stack('k'), stack('v'), stack('gla'), stack('mc'), stack('mn'),
            stack('mm'), stack('gd'))
```

```python
import functools
import math

import numpy as np
import jax
import jax.numpy as jnp
from jax import lax
from jax.experimental import pallas as pl
from jax.experimental.pallas import tpu as pltpu

F32 = jnp.float32
BF16 = jnp.bfloat16

D_MODEL = 2048
DEPTH = 4
GRID_W = 64
N_HEADS = 4
HEAD_DIM = 128
MIX_W = N_HEADS * HEAD_DIM
N_KV = 2
KV_W = N_KV * HEAD_DIM
GLA_RANK = 16
GLA_TAU = 16.0
GLA_FAST_MAX = 60.0
CHUNK = 64
SUB = 16
Q_BLOCK = 256
CONV_K = 5
D_FF = 5632
ROPE_THETA = 10000.0
N_BRANCH = 4
N_DIR = 2
N_MOD = 9
EPS = 1e-6
SCALE = HEAD_DIM ** -0.5

Z_MERGE = 0
Z_GLA = N_BRANCH * D_MODEL
Z_ML = Z_GLA + 4 * MIX_W
Z_GDQKV = Z_ML + 4 * MIX_W
Z_GDG = Z_GDQKV + 3 * MIX_W
Z_ATQ = Z_GDG + MIX_W
Z_ATK = Z_ATQ + MIX_W
Z_ATV = Z_ATK + KV_W
NZ = Z_ATV + KV_W
LANE = 128
SUBLANES = 8
L_LR = 0
L_IF = N_DIR * GLA_RANK
L_AB = L_IF + N_DIR * 2 * N_HEADS

W_LR = 4 * MIX_W
W_ML = W_LR + N_DIR * GLA_RANK
W_IF = W_ML + 4 * MIX_W
W_GD = W_IF + N_DIR * 2 * N_HEADS
W_AB = W_GD + 4 * MIX_W
W_AT = W_AB + N_DIR * 2 * N_HEADS
W_MG = W_AT + MIX_W + 2 * KV_W
IN_TILE = 512
TILE_SRC = ([W_MG + IN_TILE * j for j in range(N_BRANCH * D_MODEL // IN_TILE)]
            + [IN_TILE * j for j in range(4)] + [W_ML + IN_TILE * j for j in range(4)]
            + [W_GD + IN_TILE * j for j in range(4)] + [W_AT + IN_TILE * j for j in range(2)])

VMEM_LIMIT = 56 * 1024 * 1024
FFN_VMEM_LIMIT = 60 * 1024 * 1024


def _cparams(n_axes, vmem_limit=VMEM_LIMIT):
    return pltpu.CompilerParams(dimension_semantics=("arbitrary",) * n_axes,
                                vmem_limit_bytes=vmem_limit)


def _dot(a, b):
    return jnp.dot(a.astype(BF16), b.astype(BF16), preferred_element_type=F32)


def _dot_nt(a, b):
    return lax.dot_general(a.astype(BF16), b.astype(BF16), (((1,), (1,)), ((), ())),
                           preferred_element_type=F32)


def _dot_tn(a, b):
    return lax.dot_general(a.astype(BF16), b.astype(BF16), (((0,), (0,)), ((), ())),
                           preferred_element_type=F32)


def _split(a):
    hi = a.astype(BF16)
    return hi, (a - hi.astype(F32)).astype(BF16)


def _dot_split(a, b):
    ah, al = a
    bh, bl = b
    return (jnp.dot(ah, bh, preferred_element_type=F32) + jnp.dot(ah, bl, preferred_element_type=F32)
            + jnp.dot(al, bh, preferred_element_type=F32))


def _cumsum_mask(mask, x):
    m = mask.astype(BF16)
    x0 = x.astype(BF16)
    r1 = x - x0.astype(F32)
    x1 = r1.astype(BF16)
    x2 = (r1 - x1.astype(F32)).astype(BF16)
    return (jnp.dot(m, x0, preferred_element_type=F32) + jnp.dot(m, x1, preferred_element_type=F32)
            + jnp.dot(m, x2, preferred_element_type=F32))


def _sigmoid(x):
    return 0.5 * jnp.tanh(0.5 * x) + 0.5


def _silu(x):
    return x * _sigmoid(x)


def _softplus(x):
    return jnp.maximum(x, 0.0) + jnp.log1p(jnp.exp(-jnp.abs(x)))


def _log_sigmoid(x):
    return -_softplus(-x)


def _rmsmod(x, sc, sh):
    ms = jnp.mean(x * x, axis=-1, keepdims=True)
    return x * lax.rsqrt(ms + EPS) * (1.0 + sc) + sh


def _head_rms(x, w):
    outs = []
    for h in range(x.shape[1] // HEAD_DIM):
        seg = x[:, h * HEAD_DIM:(h + 1) * HEAD_DIM]
        ms = jnp.mean(seg * seg, axis=-1, keepdims=True)
        outs.append(seg * lax.rsqrt(ms + EPS))
    return jnp.concatenate(outs, axis=1) * w


class _Seqs:
    def __init__(self, nc, tc, nl, tl):
        self.nc, self.tc, self.nl, self.tl = nc, tc, nl, tl
        self.ctx_rows = nc * tc
        self.rows = nc * tc + nl * tl

    def mod_row(self, start):
        return jnp.where(start < self.ctx_rows, 0, 1 + (start - self.ctx_rows) // self.tl)

    def scan_tables(self):
        fwd, bwd, flg, sq = [], [], [], []
        base = 0
        for s in range(self.nc + self.nl):
            lat = s >= self.nc
            n = (self.tl if lat else self.tc) // CHUNK
            for j in range(n):
                fwd.append(base + j)
                bwd.append(base + n - 1 - j)
                flg.append((1 if j == 0 else 0) | (2 if j == n - 1 else 0) | (4 if lat else 0))
                sq.append(s)
            base += n
        return tuple(jnp.asarray(np.array(a, np.int32)) for a in (fwd, bwd, flg, sq))


def _ada_kernel(c_ref, w_ref, b_ref, o_ref):
    o_ref[...] = _dot(_silu(c_ref[...]), w_ref[...]) + b_ref[...]


def _ada(cvec, w_ada, b_ada):
    tn = 1024
    nmod = N_MOD * D_MODEL
    depth = w_ada.shape[0]
    return pl.pallas_call(
        _ada_kernel,
        grid=(depth, nmod // tn),
        in_specs=[pl.BlockSpec((8, D_MODEL), lambda l, j: (0, 0)),
                  pl.BlockSpec((None, D_MODEL, tn), lambda l, j: (l, 0, j)),
                  pl.BlockSpec((None, 1, tn), lambda l, j: (l, 0, j))],
        out_specs=pl.BlockSpec((None, 8, tn), lambda l, j: (l, 0, j)),
        out_shape=jax.ShapeDtypeStruct((depth, 8, nmod), F32),
        compiler_params=_cparams(2),
    )(cvec, w_ada, b_ada.reshape(depth, 1, nmod))


def _ffn_kernel(x_ref, sh_ref, sc_ref, g_ref, wg_ref, wu_ref, wo_ref, fw_ref, o_ref, h_ref,
                *, seqs, tm, nf, final):
    f = pl.program_id(1)
    r = seqs.mod_row(pl.program_id(0) * tm)

    @pl.when(f == 0)
    def _():
        h_ref[...] = _rmsmod(x_ref[...], sc_ref[pl.ds(r, 1), :], sh_ref[pl.ds(r, 1), :]).astype(BF16)
        o_ref[...] = jnp.zeros_like(o_ref)

    h = h_ref[...]
    g = jnp.dot(h, wg_ref[...].astype(BF16), preferred_element_type=F32)
    u = jnp.dot(h, wu_ref[...].astype(BF16), preferred_element_type=F32)
    a = (_silu(g) * u).astype(BF16)
    o_ref[...] += jnp.dot(a, wo_ref[...].astype(BF16), preferred_element_type=F32)

    @pl.when(f == nf - 1)
    def _():
        y = x_ref[...] + 0.5 * g_ref[pl.ds(r, 1), :] * o_ref[...]
        if final:
            ms = jnp.mean(y * y, axis=-1, keepdims=True)
            y = y * lax.rsqrt(ms + EPS) * fw_ref[...]
        o_ref[...] = y


def _ffn(x, mod, w_in, w_out, fw, seqs, l, i, final, tm, tf):
    rows = x.shape[0]
    nf = D_FF // tf
    j0 = 0 if i == 0 else 6
    kern = functools.partial(_ffn_kernel, seqs=seqs, tm=tm, nf=nf, final=final)
    modspec = lambda j: pl.BlockSpec((None, 8, D_MODEL), lambda m, f: (l, 0, j))
    return pl.pallas_call(
        kern,
        grid=(rows // tm, nf),
        in_specs=[pl.BlockSpec((tm, D_MODEL), lambda m, f: (m, 0), pipeline_mode=pl.Buffered(1)),
                  modspec(j0), modspec(j0 + 1), modspec(j0 + 2),
                  pl.BlockSpec((None, None, D_MODEL, tf), lambda m, f: (l, i, 0, f)),
                  pl.BlockSpec((None, None, D_MODEL, tf), lambda m, f: (l, i, 0, nf + f)),
                  pl.BlockSpec((None, None, tf, D_MODEL), lambda m, f: (l, i, f, 0)),
                  pl.BlockSpec((1, D_MODEL), lambda m, f: (0, 0))],
        out_specs=pl.BlockSpec((tm, D_MODEL), lambda m, f: (m, 0), pipeline_mode=pl.Buffered(1)),
        out_shape=jax.ShapeDtypeStruct((rows, D_MODEL), F32),
        scratch_shapes=[pltpu.VMEM((tm, D_MODEL), BF16)],
        compiler_params=_cparams(2, FFN_VMEM_LIMIT),
    )(x, mod, mod, mod, w_in, w_in, w_out, fw)


def _premix_kernel(x_ref, sh_ref, sc_ref, wlr_ref, wif_ref, wab_ref, h_ref, zs_ref, *, seqs, tm):
    r = seqs.mod_row(pl.program_id(0) * tm)
    h = _rmsmod(x_ref[...], sc_ref[pl.ds(r, 1), :], sh_ref[pl.ds(r, 1), :]).astype(BF16)
    h_ref[...] = h
    used = wlr_ref.shape[1] + wif_ref.shape[1] + wab_ref.shape[1]
    ws = jnp.concatenate([wlr_ref[0], wif_ref[0], wab_ref[0], jnp.zeros((LANE - used, D_MODEL), F32)],
                         axis=0).astype(BF16)
    zs_ref[...] = _dot_nt(h, ws)


def _premix(x, mod, w_t, seqs, l, tm):
    rows = x.shape[0]
    modspec = lambda j: pl.BlockSpec((None, 8, D_MODEL), lambda m: (l, 0, j))
    wspec = lambda start, n: pl.BlockSpec((pl.Element(1), pl.Element(n), pl.Element(D_MODEL)),
                                          lambda m: (l, start, 0))
    return pl.pallas_call(
        functools.partial(_premix_kernel, seqs=seqs, tm=tm),
        grid=(rows // tm,),
        in_specs=[pl.BlockSpec((tm, D_MODEL), lambda m: (m, 0)), modspec(3), modspec(4),
                  wspec(W_LR, W_ML - W_LR), wspec(W_IF, W_GD - W_IF), wspec(W_AB, W_AT - W_AB)],
        out_specs=(pl.BlockSpec((tm, D_MODEL), lambda m: (m, 0)), pl.BlockSpec((tm, LANE), lambda m: (m, 0))),
        out_shape=(jax.ShapeDtypeStruct((rows, D_MODEL), BF16), jax.ShapeDtypeStruct((rows, LANE), F32)),
        compiler_params=_cparams(1),
    )(x, mod, mod, w_t, w_t, w_t)


def _inproj_kernel(row_ref, h_ref, w_ref, o_ref, w_scr):
    @pl.when(pl.program_id(1) == 0)
    def _():
        w_scr[...] = w_ref[0].astype(BF16)

    o_ref[...] = lax.dot_general(h_ref[...], w_scr[...], (((1,), (1,)), ((), ())),
                                 preferred_element_type=F32)


def _inproj(h, w_t, l, tm):
    rows = h.shape[0]
    assert all(c % SUBLANES == 0 for c in TILE_SRC)
    src = jnp.asarray(np.array(TILE_SRC, np.int32) // SUBLANES)
    gs = pltpu.PrefetchScalarGridSpec(
        num_scalar_prefetch=1, grid=(len(TILE_SRC), rows // tm),
        in_specs=[pl.BlockSpec((tm, D_MODEL), lambda n, m, c: (m, 0)),
                  pl.BlockSpec((pl.Element(1), pl.Element(IN_TILE), pl.Element(D_MODEL)),
                               lambda n, m, c: (l, c[n] * SUBLANES, 0))],
        out_specs=pl.BlockSpec((tm, IN_TILE), lambda n, m, c: (m, n)),
        scratch_shapes=[pltpu.VMEM((IN_TILE, D_MODEL), BF16)])
    return pl.pallas_call(
        _inproj_kernel, grid_spec=gs,
        out_shape=jax.ShapeDtypeStruct((rows, NZ), F32),
        compiler_params=_cparams(2),
    )(src, h, w_t)


def _tri_masks(rev):
    ri = lax.broadcasted_iota(jnp.int32, (CHUNK, CHUNK), 0)
    ci = lax.broadcasted_iota(jnp.int32, (CHUNK, CHUNK), 1)
    if rev:
        return ri <= ci, ri < ci
    return ri >= ci, ri > ci


def _flags(flg):
    first = (flg & 1) != 0
    last = (flg & 2) != 0
    lat = (flg & 4) != 0
    return first, last, lat


def _hs(h):
    return slice(h * HEAD_DIM, (h + 1) * HEAD_DIM)


def _gla_init(lat, init_ref, st_scr):
    if lat:
        for d in range(N_DIR):
            for h in range(N_HEADS):
                st_scr[d, h] = init_ref[d, h].T
    else:
        st_scr[...] = jnp.zeros_like(st_scr)


def _gla_final(st_out, st_scr):
    for d in range(N_DIR):
        for h in range(N_HEADS):
            st_out[d, h] = st_scr[d, h].T


def _gla_pre(s_ref, w2, b2):
    return jnp.dot(s_ref[...].astype(BF16), w2[...], preferred_element_type=F32) + b2[...]


GLA_FAST_PRE_MIN = math.log(2.0) - GLA_FAST_MAX


def _gla_step(pres, fast, qf, kf, vf, qb, kb, vb, of_ref, ob_ref, st_scr, b_scr):
    dirs = ((qf, kf, vf, of_ref), (qb, kb, vb, ob_ref))
    for d, (q_ref, k_ref, v_ref, o_ref) in enumerate(dirs):
        rev = d == 1
        incl, _ = _tri_masks(rev)
        lg = _log_sigmoid(pres[d]) / GLA_TAU
        b = _cumsum_mask(incl, lg)
        if not fast:
            b_scr[d] = b
        bend = jnp.sum(lg, axis=0, keepdims=True)
        yield
        q = q_ref[...] * SCALE
        k = k_ref[...]
        v = v_ref[...]
        qd = q * jnp.exp(b)
        kd = k * jnp.exp(bend - b)
        eb = jnp.exp(bend)

        ri = lax.broadcasted_iota(jnp.int32, (CHUNK, CHUNK), 0)
        ci = lax.broadcasted_iota(jnp.int32, (CHUNK, CHUNK), 1)
        if fast:
            att_mask = (ri <= ci) if rev else (ri >= ci)
        elif rev:
            att_mask = ci >= (ri // SUB + 1) * SUB
        else:
            att_mask = ci < (ri // SUB) * SUB
        nsub = CHUNK // SUB
        att_rows = [[] for _ in range(N_HEADS)]
        for blk in range(nsub):
            r0 = blk * SUB
            has_edge = (blk < nsub - 1) if rev else (blk > 0)
            if not has_edge and not fast:
                for h in range(N_HEADS):
                    att_rows[h].append(jnp.zeros((SUB, CHUNK), F32))
                continue
            edge = r0 + SUB if rev else r0 - 1
            bref = b[edge:edge + 1, :] if has_edge else jnp.zeros((1, MIX_W), F32)
            qe = q[r0:r0 + SUB, :] * jnp.exp(b[r0:r0 + SUB, :] - bref)
            ke = k * jnp.exp(jnp.minimum(bref - b, GLA_FAST_MAX if fast else 0.0))
            for h in range(N_HEADS):
                att_rows[h].append(_dot_nt(qe[:, _hs(h)], ke[:, _hs(h)]))
            if fast:
                yield
        o_main = []
        for h in range(N_HEADS):
            att = jnp.where(att_mask, jnp.concatenate(att_rows[h], axis=0), 0.0)
            o_main.append(_dot_nt(qd[:, _hs(h)], st_scr[d, h]) + _dot(att, v[:, _hs(h)]))
        o_main = jnp.concatenate(o_main, axis=1)
        yield

        if fast:
            o_ref[...] = o_main
        else:
            rowi = lax.broadcasted_iota(jnp.int32, (SUBLANES, 1), 0)
            for r0 in range(0, CHUNK, SUBLANES):
                blk0 = (r0 // SUB) * SUB
                bb = b[r0:r0 + SUBLANES, :]
                qq = q[r0:r0 + SUBLANES, :]
                acc = None
                for s in range(blk0, blk0 + SUB):
                    if (s > r0 + SUBLANES - 1 and not rev) or (s < r0 and rev):
                        continue
                    bs = b_scr[d, s:s + 1, :]
                    ks = k_ref[s:s + 1, :]
                    vs = v_ref[s:s + 1, :]
                    p = qq * ks * jnp.exp(bb - bs)
                    valid = (rowi <= s - r0) if rev else (rowi >= s - r0)
                    parts = []
                    for h in range(N_HEADS):
                        c = jnp.sum(p[:, _hs(h)], axis=-1, keepdims=True)
                        parts.append(jnp.where(valid, c, 0.0) * vs[:, _hs(h)])
                    term = jnp.concatenate(parts, axis=1)
                    acc = term if acc is None else acc + term
                    if (s - blk0) % 4 == 3:
                        yield
                o_ref[r0:r0 + SUBLANES, :] = o_main[r0:r0 + SUBLANES, :] + acc
                yield

        for h in range(N_HEADS):
            st_scr[d, h] = st_scr[d, h] * eb[:, _hs(h)] + _dot_tn(v[:, _hs(h)], kd[:, _hs(h)])
        yield


def _copy_or_zero(lat, pairs):
    for src, dst in pairs:
        dst[...] = src[...] if lat else jnp.zeros_like(dst)


def _mlstm_step(qf, kf, vf, sf, qb, kb, vb, sb, bias_ref, of_ref, ob_ref, c_scr, n_scr, m_scr):
    dirs = ((qf, kf, vf, sf, of_ref), (qb, kb, vb, sb, ob_ref))
    for d, (q_ref, k_ref, v_ref, s_ref, o_ref) in enumerate(dirs):
        rev = d == 1
        incl, _ = _tri_masks(rev)
        last_row = 0 if rev else CHUNK - 1
        gates = s_ref[...] + bias_ref[...]
        lf_all = _log_sigmoid(gates)
        f_all = _cumsum_mask(incl, lf_all)
        f_all_t = f_all.T
        gates_t = gates.T
        q = q_ref[...]
        k = k_ref[...] * SCALE
        v = v_ref[...]
        for h in range(N_HEADS):
            ji = L_IF + d * 2 * N_HEADS + h
            jf = ji + N_HEADS
            r = d * N_HEADS + h
            fc = f_all[:, jf:jf + 1]
            frow = f_all_t[jf:jf + 1, :]
            igc = gates[:, ji:ji + 1]
            igrow = gates_t[ji:ji + 1, :]
            m_prev = m_scr[r:r + 1, 0:1]
            n_prev = n_scr[r:r + 1, :]
            c_prev = c_scr[d, h]
            qh, kh, vh = q[:, _hs(h)], k[:, _hs(h)], v[:, _hs(h)]
            log_d = jnp.where(incl, fc - frow + igrow, -jnp.inf)
            inter = fc + m_prev
            m_t = jnp.maximum(inter, jnp.max(log_d, axis=-1, keepdims=True))
            dmat = jnp.exp(log_d - m_t)
            a_in = jnp.exp(inter - m_t)
            s = _dot_nt(qh, kh) * dmat
            num = a_in * _dot(qh, c_prev) + _dot(s, vh)
            den = a_in * jnp.sum(qh * n_prev, axis=-1, keepdims=True) + jnp.sum(s, axis=-1, keepdims=True)
            o_ref[:, _hs(h)] = num / jnp.maximum(jnp.abs(den), jnp.exp(-m_t))
            m_new = m_t[last_row:last_row + 1, :]
            f_end = fc[last_row:last_row + 1, :]
            w_end = jnp.exp(f_end - fc + igc - m_new)
            a_end = jnp.exp(f_end + m_prev - m_new)
            kw = kh * w_end
            c_scr[d, h] = a_end * c_prev + _dot_tn(kw, vh)
            n_scr[r:r + 1, :] = a_end * n_prev + jnp.sum(kw, axis=0, keepdims=True)
            m_scr[r:r + 1, :] = jnp.broadcast_to(m_new, (1, LANE))
            yield


def _unit_tri_inverse_all(ns):
    ri = lax.broadcasted_iota(jnp.int32, (CHUNK, CHUNK), 0)
    ci = lax.broadcasted_iota(jnp.int32, (CHUNK, CHUNK), 1)
    same16 = (ri // SUB) == (ci // SUB)
    same32 = (ri // (2 * SUB)) == (ci // (2 * SUB))
    eye = (ri == ci).astype(F32)
    nd = [jnp.where(same16, n, 0.0) for n in ns]
    n1 = [_split(jnp.where(jnp.logical_and(same32, jnp.logical_not(same16)), n, 0.0)) for n in ns]
    n2 = [_split(jnp.where(same32, 0.0, n)) for n in ns]
    t = [eye - x for x in nd]
    nds = [_split(x) for x in nd]
    p = [_dot_split(x, x) for x in nds]
    yield
    for level in range(3):
        ps = [_split(x) for x in p]
        t = [x + _dot_split(_split(x), y) for x, y in zip(t, ps)]
        if level < 2:
            p = [_dot_split(y, y) for y in ps]
        yield
    for nn in (n1, n2):
        ts = [_split(x) for x in t]
        a = [_dot_split(y, x) for x, y in zip(ts, nn)]
        yield
        t = [x - _dot_split(xs, _split(y)) for x, xs, y in zip(t, ts, a)]
        yield
    return t


def _gdn_step(qf, kf, vf, sf, qb, kb, vb, sb, alog_ref, dtb_ref, of_ref, ob_ref, s_scr):
    units = []
    dirs = ((qf, kf, vf, sf, of_ref), (qb, kb, vb, sb, ob_ref))
    for d, (q_ref, k_ref, v_ref, s_ref, o_ref) in enumerate(dirs):
        rev = d == 1
        incl, strict = _tri_masks(rev)
        last_row = 0 if rev else CHUNK - 1
        small = s_ref[...]
        g_all = -jnp.exp(alog_ref[...]) * _softplus(small + dtb_ref[...])
        beta_all = _sigmoid(small)
        gam_all = _cumsum_mask(incl, g_all)
        gam_t = gam_all.T
        q = q_ref[...]
        k = k_ref[...]
        v = v_ref[...]
        for h in range(N_HEADS):
            jg = L_AB + d * 2 * N_HEADS + h
            jb = jg + N_HEADS
            gc = gam_all[:, jg:jg + 1]
            units.append(dict(
                d=d, h=h, o_ref=o_ref, incl=incl, strict=strict, gc=gc, grow=gam_t[jg:jg + 1, :],
                beta=beta_all[:, jb:jb + 1], g_end=gc[last_row:last_row + 1, :],
                q=q[:, _hs(h)], k=k[:, _hs(h)], v=v[:, _hs(h)]))

    for u in units:
        u['decay'] = jnp.exp(jnp.where(u['incl'], u['gc'] - u['grow'], -jnp.inf))
    kk = [_dot_nt(u['k'], u['k']) for u in units]
    qk = [_dot_nt(u['q'], u['k']) for u in units]
    ns = [jnp.where(u['strict'], u['beta'] * x * u['decay'], 0.0) for u, x in zip(units, kk)]
    yield
    ts = yield from _unit_tri_inverse_all(ns)
    rhs = [_split(jnp.concatenate([u['v'] * u['beta'], u['k'] * (u['beta'] * jnp.exp(u['gc']))], axis=1))
           for u in units]
    sol = [_dot_split(_split(t), r) for t, r in zip(ts, rhs)]
    yield
    s_prev = [s_scr[u['d'], u['h']] for u in units]
    w_new = [x[:, :HEAD_DIM] - _dot(x[:, HEAD_DIM:], sp) for x, sp in zip(sol, s_prev)]
    yield
    for u, x, w, sp in zip(units, qk, w_new, s_prev):
        u['o_ref'][:, _hs(u['h'])] = _dot(u['q'] * jnp.exp(u['gc']), sp) + _dot(x * u['decay'], w)
    for u, w, sp in zip(units, w_new, s_prev):
        s_scr[u['d'], u['h']] = (jnp.exp(u['g_end']) * sp
                                 + _dot_tn(u['k'] * jnp.exp(u['g_end'] - u['gc']), w))


N_UNITS = N_DIR * N_HEADS
_DONE = object()
ST_SHAPE = (N_DIR, N_HEADS, HEAD_DIM, HEAD_DIM)


def _scan_kernel(fwd_ref, bwd_ref, flg_ref, sq_ref,
                 g_f, g_b, m_f, m_b, d_f, d_b, sf, sb,
                 w2f, w2b, b2f, b2b, mbias, alog, dtb,
                 g0, c0, n0, m0, s0,
                 g_of, g_ob, m_of, m_ob, d_of, d_ob, g_out, c_out, n_out, m_out, s_out,
                 g_scr, b_scr, c_scr, n_scr, m_scr, s_scr):
    first, last, lat = _flags(flg_ref[pl.program_id(0)])
    carried = ((c0, c_scr), (n0, n_scr), (m0, m_scr), (s0, s_scr))
    qkv = lambda ref: tuple(ref.at[:, j * MIX_W:(j + 1) * MIX_W] for j in range(3))
    (gqf, gkf, gvf), (gqb, gkb, gvb) = qkv(g_f), qkv(g_b)
    (mqf, mkf, mvf), (mqb, mkb, mvb) = qkv(m_f), qkv(m_b)
    (dqf, dkf, dvf), (dqb, dkb, dvb) = qkv(d_f), qkv(d_b)

    @pl.when(jnp.logical_and(first, jnp.logical_not(lat)))
    def _():
        _gla_init(False, g0, g_scr)
        _copy_or_zero(False, carried)

    @pl.when(jnp.logical_and(first, lat))
    def _():
        _gla_init(True, g0, g_scr)
        _copy_or_zero(True, carried)

    pres = (_gla_pre(sf, w2f, b2f), _gla_pre(sb, w2b, b2b))
    fast = jnp.minimum(jnp.min(pres[0]), jnp.min(pres[1])) > GLA_FAST_PRE_MIN

    def run(gla_fast):
        stages = [_gdn_step(dqf, dkf, dvf, sf, dqb, dkb, dvb, sb, alog, dtb, d_of, d_ob, s_scr),
                  _mlstm_step(mqf, mkf, mvf, sf, mqb, mkb, mvb, sb, mbias, m_of, m_ob, c_scr, n_scr, m_scr),
                  _gla_step(pres, gla_fast, gqf, gkf, gvf, gqb, gkb, gvb, g_of, g_ob, g_scr, b_scr)]
        strides = {stages[0]: 1, stages[1]: 1, stages[2]: 1 if gla_fast else 4}
        while stages:
            for g in list(stages):
                for _ in range(strides[g]):
                    if g in stages and next(g, _DONE) is _DONE:
                        stages.remove(g)

    @pl.when(fast)
    def _():
        run(True)

    @pl.when(jnp.logical_not(fast))
    def _():
        run(False)

    @pl.when(jnp.logical_and(last, jnp.logical_not(lat)))
    def _():
        _gla_final(g_out, g_scr)
        for src, dst in ((c_scr, c_out), (n_scr, n_out), (m_scr, m_out), (s_scr, s_out)):
            dst[...] = src[...]


def _scans(z, zq, zs, seqs, l, w2f, w2b, b2f, b2b, mbias, alog_row, dtb_row,
           state_gla, state_c, state_n, state_m, state_gdn):
    rows = z.shape[0]
    tabs = seqs.scan_tables()
    nsteps = int(tabs[0].shape[0])

    def blk(which, col, width):
        if which == 0:
            return pl.BlockSpec((CHUNK, width), lambda i, fw, bw, fl, sq: (fw[i], col))
        return pl.BlockSpec((CHUNK, width), lambda i, fw, bw, fl, sq: (bw[i], col))

    def const(shape):
        nd = len(shape)
        return pl.BlockSpec(shape, lambda i, fw, bw, fl, sq: (0,) * nd)

    lat_i = lambda sq, i: jnp.maximum(sq[i] - seqs.nc, 0)
    ctx_i = lambda sq, i: jnp.minimum(sq[i], seqs.nc - 1)
    in_specs, args = [], []
    def qkv_blk(which, col):
        shape = (pl.Element(CHUNK), pl.Element(3 * MIX_W))
        if which == 0:
            return pl.BlockSpec(shape, lambda i, fw, bw, fl, sq: (fw[i] * CHUNK, col))
        return pl.BlockSpec(shape, lambda i, fw, bw, fl, sq: (bw[i] * CHUNK, col))

    for src, col in ((z, Z_GLA), (z, Z_ML), (zq, 0)):
        in_specs += [qkv_blk(0, col), qkv_blk(1, col)]
        args += [src] * 2
    in_specs += [blk(0, 0, LANE), blk(1, 0, LANE)]
    args += [zs, zs]
    in_specs += [const((LANE, MIX_W)), const((LANE, MIX_W)), const((1, MIX_W)), const((1, MIX_W)),
                 const((1, LANE)), const((1, LANE)), const((1, LANE))]
    args += [w2f, w2b, b2f, b2b, mbias, alog_row, dtb_row]
    st_in = pl.BlockSpec((None, None) + ST_SHAPE, lambda i, fw, bw, fl, sq: (lat_i(sq, i), l, 0, 0, 0, 0))
    vec_in = pl.BlockSpec((None, None, N_UNITS, LANE), lambda i, fw, bw, fl, sq: (lat_i(sq, i), l, 0, 0))
    depth = state_c.shape[1]
    n0 = state_n.reshape(seqs.nl, depth, N_UNITS, HEAD_DIM)
    m0 = jnp.broadcast_to(state_m.reshape(seqs.nl, depth, N_UNITS, 1), (seqs.nl, depth, N_UNITS, LANE))
    in_specs += [st_in, st_in, vec_in, vec_in, st_in]
    args += [state_gla, state_c, n0, m0, state_gdn]

    o_f = pl.BlockSpec((CHUNK, MIX_W), lambda i, fw, bw, fl, sq: (fw[i], 0))
    o_b = pl.BlockSpec((CHUNK, MIX_W), lambda i, fw, bw, fl, sq: (bw[i], 0))
    st_out = pl.BlockSpec((None,) + ST_SHAPE, lambda i, fw, bw, fl, sq: (ctx_i(sq, i), 0, 0, 0, 0))
    vec_out = pl.BlockSpec((None, N_UNITS, LANE), lambda i, fw, bw, fl, sq: (ctx_i(sq, i), 0, 0))
    o_sds = jax.ShapeDtypeStruct((rows, MIX_W), F32)
    st_sds = jax.ShapeDtypeStruct((seqs.nc,) + ST_SHAPE, F32)
    vec_sds = jax.ShapeDtypeStruct((seqs.nc, N_UNITS, LANE), F32)
    gs = pltpu.PrefetchScalarGridSpec(
        num_scalar_prefetch=4, grid=(nsteps,), in_specs=in_specs,
        out_specs=(o_f, o_b, o_f, o_b, o_f, o_b, st_out, st_out, vec_out, vec_out, st_out),
        scratch_shapes=[pltpu.VMEM(ST_SHAPE, F32), pltpu.VMEM((N_DIR, CHUNK, MIX_W), F32),
                        pltpu.VMEM(ST_SHAPE, F32), pltpu.VMEM((N_UNITS, LANE), F32),
                        pltpu.VMEM((N_UNITS, LANE), F32), pltpu.VMEM(ST_SHAPE, F32)])
    return pl.pallas_call(
        _scan_kernel, grid_spec=gs,
        out_shape=(o_sds,) * 6 + (st_sds, st_sds, vec_sds, vec_sds, st_sds),
        compiler_params=_cparams(1),
    )(*tabs, *args)


CONV_ROWS = 256
HALO = 8


def _conv_kernel(x_ref, prev_ref, next_ref, w_ref, o_ref, xe_ref, *, seqs):
    i = pl.program_id(0)
    start = i * CONV_ROWS
    in_lat = start >= seqs.ctx_rows
    off = jnp.where(in_lat, (start - seqs.ctx_rows) % seqs.tl, start % seqs.tc)
    seq_len = jnp.where(in_lat, seqs.tl, seqs.tc)
    xe_ref[0:HALO, :] = jnp.where(off > 0, prev_ref[...], 0.0)
    xe_ref[HALO:HALO + CONV_ROWS, :] = x_ref[...]
    xe_ref[HALO + CONV_ROWS:, :] = jnp.where(off + CONV_ROWS < seq_len, next_ref[...], 0.0)
    pad = CONV_K // 2
    y = jnp.zeros((CONV_ROWS, 3 * MIX_W), F32)
    for j in range(CONV_K):
        y = y + xe_ref[pl.ds(HALO - pad + j, CONV_ROWS), :] * w_ref[j:j + 1, :]
    y = _silu(y)
    outs = []
    for h in range(3 * N_HEADS):
        seg = y[:, _hs(h)]
        if h < 2 * N_HEADS:
            seg = seg * lax.rsqrt(jnp.sum(seg * seg, axis=-1, keepdims=True) + EPS)
            if h < N_HEADS:
                seg = seg * SCALE
        outs.append(seg)
    o_ref[...] = jnp.concatenate(outs, axis=1)


def _gdn_prep(z, conv_w, seqs):
    rows = z.shape[0]
    cw = 3 * MIX_W
    cb = Z_GDQKV // cw
    nb = rows // CONV_ROWS
    per = CONV_ROWS // HALO
    last8 = rows // HALO - 1
    return pl.pallas_call(
        functools.partial(_conv_kernel, seqs=seqs),
        grid=(nb,),
        in_specs=[pl.BlockSpec((CONV_ROWS, cw), lambda i: (i, cb)),
                  pl.BlockSpec((HALO, cw), lambda i: (jnp.maximum(i * per - 1, 0), cb)),
                  pl.BlockSpec((HALO, cw), lambda i: (jnp.minimum((i + 1) * per, last8), cb)),
                  pl.BlockSpec((8, cw), lambda i: (0, 0))],
        out_specs=pl.BlockSpec((CONV_ROWS, cw), lambda i: (i, 0)),
        out_shape=jax.ShapeDtypeStruct((rows, cw), F32),
        scratch_shapes=[pltpu.VMEM((CONV_ROWS + 2 * HALO, cw), F32)],
        compiler_params=_cparams(1),
    )(z, z, z, conv_w)


ATT_ROWS = 256


def _rope(y, cos, sin_signed):
    n = y.shape[1]
    lane = lax.broadcasted_iota(jnp.int32, y.shape, 1)
    partner = jnp.where(lane % 2 == 0, pltpu.roll(y, n - 1, axis=1), pltpu.roll(y, 1, axis=1))
    reps = n // HEAD_DIM
    c = jnp.concatenate([cos] * reps, axis=1)
    s = jnp.concatenate([sin_signed] * reps, axis=1)
    return y * c + partner * s


def _attn_prep_kernel(q_ref, k_ref, cos_ref, sin_ref, qw_ref, kw_ref, qo_ref, ko_ref):
    cos = cos_ref[...]
    sin = sin_ref[...]
    qo_ref[...] = _rope(_head_rms(q_ref[...], qw_ref[...]), cos, sin)
    ko_ref[...] = _rope(_head_rms(k_ref[...], kw_ref[...]), cos, sin)


def _attn_prep(z, cos_t, sin_t, qw, kw):
    rows = z.shape[0]
    return pl.pallas_call(
        _attn_prep_kernel,
        grid=(rows // ATT_ROWS,),
        in_specs=[pl.BlockSpec((ATT_ROWS, MIX_W), lambda i: (i, Z_ATQ // MIX_W)),
                  pl.BlockSpec((ATT_ROWS, KV_W), lambda i: (i, Z_ATK // KV_W)),
                  pl.BlockSpec((ATT_ROWS, HEAD_DIM), lambda i: (i, 0)),
                  pl.BlockSpec((ATT_ROWS, HEAD_DIM), lambda i: (i, 0)),
                  pl.BlockSpec((1, MIX_W), lambda i: (0, 0)),
                  pl.BlockSpec((1, KV_W), lambda i: (0, 0))],
        out_specs=(pl.BlockSpec((ATT_ROWS, MIX_W), lambda i: (i, 0)),
                   pl.BlockSpec((ATT_ROWS, KV_W), lambda i: (i, 0))),
        out_shape=(jax.ShapeDtypeStruct((rows, MIX_W), F32), jax.ShapeDtypeStruct((rows, KV_W), F32)),
        compiler_params=_cparams(1),
    )(z, z, cos_t, sin_t, qw, kw)


def _attn_kernel(*refs, has_cache):
    if has_cache:
        q_ref, k_ref, v_ref, ck_ref, cv_ref, o_ref = refs
    else:
        q_ref, k_ref, v_ref, o_ref = refs
    k = k_ref[...]
    v = v_ref[...]
    for g in range(N_HEADS // N_KV):
        qh = q_ref[:, _hs(g)]
        s = _dot_nt(qh, k) * SCALE
        m = jnp.max(s, axis=-1, keepdims=True)
        if has_cache:
            sc = _dot_nt(qh, ck_ref[...]) * SCALE
            m = jnp.maximum(m, jnp.max(sc, axis=-1, keepdims=True))
            pc = jnp.exp(sc - m)
        p = jnp.exp(s - m)
        den = jnp.sum(p, axis=-1, keepdims=True)
        if has_cache:
            den = den + jnp.sum(pc, axis=-1, keepdims=True)
        inv = 1.0 / den
        o = _dot(p * inv, v)
        if has_cache:
            o = o + _dot(pc * inv, cv_ref[...])
        o_ref[:, _hs(g)] = o


def _attend(qn, kn, z, row0, nseq, t, y_rows, cache=None, l=0):
    gw = (N_HEADS // N_KV) * HEAD_DIM
    nqb = t // Q_BLOCK
    rb0 = row0 // Q_BLOCK
    sb0 = row0 // t
    vcol = Z_ATV // HEAD_DIM
    in_specs = [pl.BlockSpec((Q_BLOCK, gw), lambda b, kv, qi: (rb0 + b * nqb + qi, kv)),
                pl.BlockSpec((t, HEAD_DIM), lambda b, kv, qi: (sb0 + b, kv)),
                pl.BlockSpec((t, HEAD_DIM), lambda b, kv, qi: (sb0 + b, vcol + kv))]
    args = [qn, kn, z]
    if cache is not None:
        ck, cv = cache
        past = ck.shape[2]
        cspec = pl.BlockSpec((None, None, past, HEAD_DIM), lambda b, kv, qi: (b, l, 0, kv))
        in_specs += [cspec, cspec]
        args += [ck, cv]
    return pl.pallas_call(
        functools.partial(_attn_kernel, has_cache=cache is not None),
        grid=(nseq, N_KV, nqb),
        in_specs=in_specs,
        out_specs=pl.BlockSpec((Q_BLOCK, gw), lambda b, kv, qi: (b * nqb + qi, kv)),
        out_shape=jax.ShapeDtypeStruct((y_rows, MIX_W), F32),
        compiler_params=_cparams(3),
    )(*args)


POST_ROWS = 512


def _post_kernel(gf, gb, mf, mb, df, db, zg, zo, zd, atc, atl, nw_ref, o_ref, *, ctx_blocks):
    nw = nw_ref[...]
    o_ref[0] = (_head_rms(gf[...] + gb[...], nw[0:1, :]) * _silu(zg[...])).astype(BF16)
    o_ref[1] = (_head_rms(mf[...] + mb[...], nw[1:2, :]) * _sigmoid(zo[...])).astype(BF16)
    o_ref[2] = (_head_rms(df[...] + db[...], nw[2:3, :]) * _silu(zd[...])).astype(BF16)
    at = jnp.where(pl.program_id(0) < ctx_blocks, atc[...], atl[...])
    o_ref[3] = at.astype(BF16)


def _branch_post(outs, z, y_ctx, y_lat, norm_w):
    rows = z.shape[0]
    tm = min(POST_ROWS, y_ctx.shape[0], y_lat.shape[0])
    ctx_blocks = y_ctx.shape[0] // tm
    row = lambda c: pl.BlockSpec((tm, MIX_W), lambda i: (i, c))
    in_specs = [row(0)] * 6 + [row(Z_GLA // MIX_W + 3), row(Z_ML // MIX_W + 3), row(Z_GDG // MIX_W),
                               pl.BlockSpec((tm, MIX_W), lambda i: (jnp.minimum(i, ctx_blocks - 1), 0)),
                               pl.BlockSpec((tm, MIX_W), lambda i: (jnp.maximum(i - ctx_blocks, 0), 0)),
                               pl.BlockSpec((8, MIX_W), lambda i: (0, 0))]
    return pl.pallas_call(
        functools.partial(_post_kernel, ctx_blocks=ctx_blocks),
        grid=(rows // tm,),
        in_specs=in_specs,
        out_specs=pl.BlockSpec((N_BRANCH, tm, MIX_W), lambda i: (0, i, 0)),
        out_shape=jax.ShapeDtypeStruct((N_BRANCH, rows, MIX_W), BF16),
        compiler_params=_cparams(1),
    )(*outs, z, z, z, y_ctx, y_lat, norm_w)


def _merge_kernel(x_ref, g_ref, y_ref, zm_ref, wb_ref, wo_ref, o_ref, acc_ref, *, seqs, tm):
    n = pl.program_id(1)
    r = seqs.mod_row(pl.program_id(0) * tm)
    p = _sigmoid(zm_ref[...]) * jnp.dot(y_ref[...], wb_ref[...], preferred_element_type=F32)

    @pl.when(n == 0)
    def _():
        acc_ref[...] = p

    @pl.when(n > 0)
    def _():
        acc_ref[...] += p

    @pl.when(n == N_BRANCH - 1)
    def _():
        out = jnp.dot(acc_ref[...].astype(BF16), wo_ref[...], preferred_element_type=F32)
        o_ref[...] = x_ref[...] + g_ref[pl.ds(r, 1), :] * out


def _merge(x, mod, ybr, z, wb, wo, seqs, l, tm):
    rows = x.shape[0]
    return pl.pallas_call(
        functools.partial(_merge_kernel, seqs=seqs, tm=tm),
        grid=(rows // tm, N_BRANCH),
        in_specs=[pl.BlockSpec((tm, D_MODEL), lambda m, n: (m, 0), pipeline_mode=pl.Buffered(1)),
                  pl.BlockSpec((None, 8, D_MODEL), lambda m, n: (l, 0, 5)),
                  pl.BlockSpec((None, tm, MIX_W), lambda m, n: (n, m, 0)),
                  pl.BlockSpec((tm, D_MODEL), lambda m, n: (m, n)),
                  pl.BlockSpec((None, None, MIX_W, D_MODEL), lambda m, n: (l, n, 0, 0)),
                  pl.BlockSpec((None, D_MODEL, D_MODEL), lambda m, n: (l, 0, 0), pipeline_mode=pl.Buffered(1))],
        out_specs=pl.BlockSpec((tm, D_MODEL), lambda m, n: (m, 0)),
        out_shape=jax.ShapeDtypeStruct((rows, D_MODEL), F32),
        scratch_shapes=[pltpu.VMEM((tm, D_MODEL), F32)],
        compiler_params=_cparams(2),
    )(x, mod, ybr, z, wb, wo)


def _lane_row(vals, lane0):
    v = vals.reshape(-1).astype(F32)
    return jnp.zeros((1, LANE), F32).at[0, lane0:lane0 + v.shape[0]].set(v)


def _rope_tables(seqs):
    t = seqs.tl
    row = (np.arange(t) // GRID_W).astype(np.float32)
    col = (np.arange(t) % GRID_W).astype(np.float32)
    n_pairs = HEAD_DIM // 4
    inv = jnp.asarray(ROPE_THETA, F32) ** (-jnp.arange(n_pairs, dtype=F32) / n_pairs)
    ang = jnp.concatenate([jnp.asarray(row)[:, None] * inv, jnp.asarray(col)[:, None] * inv], axis=-1)
    cos = jnp.repeat(jnp.cos(ang), 2, axis=-1)
    sin = jnp.repeat(jnp.sin(ang), 2, axis=-1) * jnp.asarray(np.tile([-1.0, 1.0], HEAD_DIM // 2), F32)
    cos = jnp.concatenate([jnp.ones((seqs.ctx_rows, HEAD_DIM), F32)] + [cos] * seqs.nl, axis=0)
    sin = jnp.concatenate([jnp.zeros((seqs.ctx_rows, HEAD_DIM), F32)] + [sin] * seqs.nl, axis=0)
    return cos, sin


def _trunk(seqs, x, cvec, cache_k, cache_v, state_gla, state_c, state_n, state_m, state_gdn,
           w_ada, b_ada, w_ffn_in, w_ffn_out, w_in, gla_w2, gla_b2, gla_norm_w, ml_gate_b, ml_norm_w,
           gd_conv_w, gd_a_log, gd_dt_bias, gd_norm_w, q_norm_w, k_norm_w, w_branch, w_out, final_norm_w,
           tm_ffn=1024, tf=512, tm_in=2048, tm_pre=512, tm_mg=512):
    depth = w_in.shape[0]
    rows = seqs.rows
    tm_ffn, tm_in, tm_mg = min(tm_ffn, seqs.tl), min(tm_in, seqs.tl), min(tm_mg, seqs.tl)
    mod = _ada(cvec, w_ada, b_ada)
    w_t = jnp.swapaxes(w_in, 1, 2)
    wb = w_branch.astype(BF16)
    wo = w_out.astype(BF16)
    cos_t, sin_t = _rope_tables(seqs)
    fw = final_norm_w.reshape(1, D_MODEL)
    past = cache_k.shape[2]
    ck = cache_k.reshape(cache_k.shape[:2] + (past, KV_W))
    cv = cache_v.reshape(cache_v.shape[:2] + (past, KV_W))
    zpad = jnp.zeros((LANE - GLA_RANK, MIX_W), F32)
    ctx = []
    for l in range(depth):
        x = _ffn(x, mod, w_ffn_in, w_ffn_out, fw, seqs, l, 0, False, tm_ffn, tf)
        h, zs = _premix(x, mod, w_t, seqs, l, tm_pre)
        z = _inproj(h, w_t, l, tm_in)

        w2f = jnp.concatenate([gla_w2[l, 0], zpad], axis=0).astype(BF16)
        w2b = jnp.concatenate([zpad[:GLA_RANK], gla_w2[l, 1], zpad[:LANE - 2 * GLA_RANK]], axis=0).astype(BF16)
        zq = _gdn_prep(z, jnp.concatenate([gd_conv_w[l], jnp.zeros((8 - CONV_K, 3 * MIX_W), F32)], axis=0), seqs)
        ab_lanes = jnp.concatenate([gd_a_log[l], jnp.zeros((N_DIR, N_HEADS), F32)], axis=1)
        dt_lanes = jnp.concatenate([gd_dt_bias[l], jnp.zeros((N_DIR, N_HEADS), F32)], axis=1)
        gf, gb, mf, mb, df, db, st_gla, st_c, st_n, st_m, st_gd = _scans(
            z, zq, zs, seqs, l, w2f, w2b, gla_b2[l, 0:1], gla_b2[l, 1:2], _lane_row(ml_gate_b[l], L_IF),
            _lane_row(ab_lanes, L_AB), _lane_row(dt_lanes, L_AB),
            state_gla, state_c, state_n, state_m, state_gdn)

        qn, kn = _attn_prep(z, cos_t, sin_t, jnp.tile(q_norm_w[l], N_HEADS)[None, :],
                            jnp.tile(k_norm_w[l], N_KV)[None, :])
        y_ctx = _attend(qn, kn, z, 0, seqs.nc, seqs.tc, seqs.ctx_rows)
        y_lat = _attend(qn, kn, z, seqs.ctx_rows, seqs.nl, seqs.tl, seqs.nl * seqs.tl, cache=(ck, cv), l=l)

        norm_w = jnp.stack([jnp.tile(w[l], N_HEADS) for w in (gla_norm_w, ml_norm_w, gd_norm_w)]
                           + [jnp.zeros((MIX_W,), F32)] * 5, axis=0)
        ybr = _branch_post((gf, gb, mf, mb, df, db), z, y_ctx, y_lat, norm_w)
        x = _merge(x, mod, ybr, z, wb, wo, seqs, l, tm_mg)
        x = _ffn(x, mod, w_ffn_in, w_ffn_out, fw, seqs, l, 1, l == depth - 1, tm_ffn, tf)

        nc, tc = seqs.nc, seqs.tc
        ctx.append(dict(
            k=kn[:seqs.ctx_rows].reshape(nc, tc, N_KV, HEAD_DIM),
            v=z[:seqs.ctx_rows, Z_ATV:Z_ATV + KV_W].reshape(nc, tc, N_KV, HEAD_DIM),
            gla=st_gla, mc=st_c,
            mn=st_n.reshape(nc, N_DIR, N_HEADS, HEAD_DIM),
            mm=st_m[:, :, 0].reshape(nc, N_DIR, N_HEADS),
            gd=st_gd))
    return x, ctx


def kernel(x_prompt, x_sample, cache_k, cache_v, state_gla, state_mlstm_c, state_mlstm_n, state_mlstm_m,
           state_gdn, c, c_ctx, w_ada, b_ada, w_ffn_in, w_ffn_out, w_in, gla_w2, gla_b2, gla_norm_w,
           ml_gate_b, ml_norm_w, gd_conv_w, gd_a_log, gd_dt_bias, gd_norm_w, q_norm_w, k_norm_w,
           w_branch, w_out, final_norm_w):
    nc, tc, _ = x_prompt.shape
    nl, tl, _ = x_sample.shape
    seqs = _Seqs(nc, tc, nl, tl)
    x = jnp.concatenate([x_prompt.reshape(nc * tc, D_MODEL), x_sample.reshape(nl * tl, D_MODEL)], axis=0)
    cvec = jnp.concatenate([c_ctx[None, :], c, jnp.zeros((8 - 1 - nl, D_MODEL), F32)], axis=0)
    y, ctx = _trunk(seqs, x, cvec, cache_k, cache_v, state_gla, state_mlstm_c, state_mlstm_n, state_mlstm_m,
                    state_gdn, w_ada, b_ada, w_ffn_in, w_ffn_out, w_in, gla_w2, gla_b2, gla_norm_w,
                    ml_gate_b, ml_norm_w, gd_conv_w, gd_a_log, gd_dt_bias, gd_norm_w, q_norm_w, k_norm_w,
                    w_branch, w_out, final_norm_w)
    y_prompt = y[:nc * tc].reshape(nc, tc, D_MODEL)
    y_sample = y[nc * tc:].reshape(nl, tl, D_MODEL)
    stack = lambda name: jnp.stack([cx[name] for cx in ctx], axis=1)
    return (y_prompt, y_sample, stack('k'), stack('v'), stack('gla'), stack('mc'), stack('mn'),
            stack('mm'), stack('gd'))
```
